```python
import math
import jax, jax.numpy as jnp
from jax import lax
import numpy as np

D_MODEL = 4096
BATCH = 2
SEQ = 8192
DEPTH = 2

CHUNK = 64
EPS = 1e-6
MASK_VALUE = -1e30
TINY = 1e-30

MLA_HEADS = 16
MLA_Q_RANK = 768
MLA_KV_RANK = 512
MLA_NOPE = 128
MLA_ROPE = 64
MLA_V = 128
ROPE_THETA = 10000.0
Q_BLOCK = 128

HG_HEADS = 8
HG_KDIM = 128
HG_VDIM = 128
HG_CHUNK = 16

CA_HEADS = 8
CA_HEAD_DIM = 128
CA_LEFT_CHUNKS = 8
CA_REL_CLIP = 256

MLA_WIDTH = MLA_HEADS * MLA_V
HG_WIDTH = HG_HEADS * HG_VDIM
CA_WIDTH = CA_HEADS * CA_HEAD_DIM
MIX_WIDTH = MLA_WIDTH + HG_WIDTH + CA_WIDTH
D_FF = ((8 * D_MODEL + 3 * 256 - 1) // (3 * 256)) * 256

IN_SIZES = (MLA_Q_RANK, MLA_KV_RANK, MLA_ROPE,
            HG_HEADS * HG_KDIM, HG_HEADS * HG_KDIM, HG_WIDTH, HG_WIDTH,
            CA_WIDTH, CA_WIDTH, CA_WIDTH)
D_IN = sum(IN_SIZES)
IN_OFFSETS = tuple(int(o) for o in np.cumsum(IN_SIZES)[:-1])

kernel_name = "hybrid_mla_hgrn2_chunkattn_block"


def rms_norm(x, g):
    xf = x.astype(jnp.float32)
    y = xf * lax.rsqrt(jnp.mean(xf * xf, axis=-1, keepdims=True) + EPS)
    return (y * g.astype(jnp.float32)).astype(x.dtype)


def rope_tables(positions):
    half = MLA_ROPE // 2
    inv_freq = jnp.exp(-math.log(ROPE_THETA) * 2.0 * jnp.arange(half, dtype=jnp.float32) / MLA_ROPE)
    ang = positions.astype(jnp.float32)[..., None] * inv_freq
    return jnp.cos(ang)[:, :, None, :], jnp.sin(ang)[:, :, None, :]


def apply_rope(t, cos, sin):
    half = t.shape[-1] // 2
    tf = t.astype(jnp.float32)
    t1, t2 = tf[..., :half], tf[..., half:]
    return jnp.concatenate([t1 * cos - t2 * sin, t1 * sin + t2 * cos], axis=-1).astype(t.dtype)


def mla_mixer(c_q, c_kv, k_rope, q_norm, kv_norm, w_uq, w_ukv, out_norm, cos, sin):
    B, S, _ = c_q.shape
    q = (rms_norm(c_q, q_norm) @ w_uq).reshape(B, S, MLA_HEADS, MLA_NOPE + MLA_ROPE)
    kv = (rms_norm(c_kv, kv_norm) @ w_ukv).reshape(B, S, MLA_HEADS, MLA_NOPE + MLA_V)
    q_nope = q[..., :MLA_NOPE]
    q_pe = apply_rope(q[..., MLA_NOPE:], cos, sin)
    k_nope, v = kv[..., :MLA_NOPE], kv[..., MLA_NOPE:]
    k_pe = apply_rope(k_rope[:, :, None, :], cos, sin)[:, :, 0, :]
    scale = (MLA_NOPE + MLA_ROPE) ** -0.5
    nb = S // Q_BLOCK
    qn_blocks = q_nope.reshape(B, nb, Q_BLOCK, MLA_HEADS, MLA_NOPE).transpose(1, 0, 2, 3, 4)
    qp_blocks = q_pe.reshape(B, nb, Q_BLOCK, MLA_HEADS, MLA_ROPE).transpose(1, 0, 2, 3, 4)
    key_chunk = jnp.arange(S) // CHUNK

    def one_block(args):
        qn, qp, blk = args
        s = (jnp.einsum('bqhd,bkhd->bhqk', qn, k_nope)
             + jnp.einsum('bqhr,bkr->bhqk', qp, k_pe)).astype(jnp.float32) * scale
        q_chunk = (blk * Q_BLOCK + jnp.arange(Q_BLOCK)) // CHUNK
        mask = key_chunk[None, :] <= q_chunk[:, None]
        s = jnp.where(mask[None, None], s, MASK_VALUE)
        p = jax.nn.softmax(s, axis=-1).astype(v.dtype)
        return jnp.einsum('bhqk,bkhd->bqhd', p, v)

    o = lax.map(one_block, (qn_blocks, qp_blocks, jnp.arange(nb)))
    o = o.transpose(1, 0, 2, 3, 4).reshape(B, S, MLA_WIDTH)
    return rms_norm(o, out_norm)


def hgrn2_mixer(q, f_pre, i, g, lb, out_norm):
    B, S, _ = q.shape
    L = HG_CHUNK
    nc = S // L
    dtype = q.dtype
    q = jax.nn.silu(q).astype(jnp.float32)
    fp = f_pre.astype(jnp.float32)
    f = lb + (1.0 - lb) * jax.nn.sigmoid(fp)
    log_f = jnp.log(jnp.maximum(f, TINY))
    k = (1.0 - lb) * jax.nn.sigmoid(-fp)

    def heads(t, d):
        return t.reshape(B, nc, L, HG_HEADS, d).transpose(0, 3, 1, 2, 4)

    qh, kh, lfh = heads(q, HG_KDIM), heads(k, HG_KDIM), heads(log_f, HG_KDIM)
    vh = heads(i.astype(jnp.float32), HG_VDIM)
    b = jnp.cumsum(lfh, axis=3)
    causal = jnp.tril(jnp.ones((L, L), dtype=bool))[:, :, None]
    diff = b[..., :, None, :] - b[..., None, :, :]
    decay = jnp.where(causal, jnp.exp(jnp.where(causal, diff, 0.0)), 0.0)
    A = jnp.sum(qh[..., :, None, :] * kh[..., None, :, :] * decay, axis=-1)
    o_intra = jnp.einsum('bhnij,bhnjv->bhniv', A, vh)

    b_last = b[..., -1:, :]
    k_dec = kh * jnp.exp(b_last - b)
    q_dec = qh * jnp.exp(b)
    chunk_decay = jnp.exp(b_last[..., 0, :])

    def step(state, xs):
        qd, kd, vc, cd = xs
        out = jnp.einsum('bhlk,bhkv->bhlv', qd, state)
        state = cd[..., None] * state + jnp.einsum('bhlk,bhlv->bhkv', kd, vc)
        return state, out

    xs = (jnp.moveaxis(q_dec, 2, 0), jnp.moveaxis(k_dec, 2, 0),
          jnp.moveaxis(vh, 2, 0), jnp.moveaxis(chunk_decay, 2, 0))
    state0 = jnp.zeros((B, HG_HEADS, HG_KDIM, HG_VDIM), jnp.float32)
    _, o_inter = lax.scan(step, state0, xs)
    o = o_intra + jnp.moveaxis(o_inter, 0, 2)
    o = o.transpose(0, 2, 3, 1, 4).reshape(B, S, HG_HEADS, HG_VDIM).astype(dtype)
    o = rms_norm(o, out_norm.reshape(HG_HEADS, HG_VDIM)).reshape(B, S, HG_WIDTH)
    return o * jax.nn.silu(g)


def chunk_attn_mixer(q, k, v, rel_bias, out_norm):
    B, S, _ = q.shape
    nc = S // CHUNK
    W = CA_LEFT_CHUNKS + 1

    def heads(t):
        return t.reshape(B, nc, CHUNK, CA_HEADS, CA_HEAD_DIM)

    qc, kc, vc = heads(q), heads(k), heads(v)
    pad = ((0, 0), (CA_LEFT_CHUNKS, 0), (0, 0), (0, 0), (0, 0))
    kp, vp = jnp.pad(kc, pad), jnp.pad(vc, pad)
    k_band = jnp.concatenate([kp[:, w:w + nc] for w in range(W)], axis=2)
    v_band = jnp.concatenate([vp[:, w:w + nc] for w in range(W)], axis=2)
    s = jnp.einsum('bnqhd,bnkhd->bhnqk', qc, k_band).astype(jnp.float32) * (CA_HEAD_DIM ** -0.5)
    a = jnp.arange(CHUNK)
    kidx = jnp.arange(W * CHUNK)
    dist = (CA_LEFT_CHUNKS * CHUNK + a[:, None]) - kidx[None, :]
    bucket = jnp.clip(dist, -CA_REL_CLIP, CA_REL_CLIP) + CA_REL_CLIP
    bias = rel_bias[:, bucket].astype(jnp.float32)
    key_chunk = jnp.arange(nc)[:, None] - CA_LEFT_CHUNKS + kidx[None, :] // CHUNK
    valid = key_chunk >= 0
    s = s + bias[None, :, None]
    s = jnp.where(valid[None, None, :, None, :], s, MASK_VALUE)
    p = jax.nn.softmax(s, axis=-1).astype(v.dtype)
    o = jnp.einsum('bhnqk,bnkhd->bnqhd', p, v_band).reshape(B, S, CA_WIDTH)
    return rms_norm(o, out_norm)


def setup_inputs(seed: int = 0) -> dict:
    key = jax.random.key(seed)
    ks = jax.random.split(key, 24)
    f32 = jnp.float32

    def w(k, shape, fan_in):
        return jax.random.normal(k, shape, f32) * (fan_in ** -0.5)

    def gain(k, shape):
        return 1.0 + 0.05 * jax.random.normal(k, shape, f32)

    x = jax.random.normal(ks[0], (BATCH, SEQ, D_MODEL), f32)
    offset = jax.random.randint(ks[1], (BATCH, 1), 0, 4096, dtype=jnp.int32)
    positions = (offset + jnp.arange(SEQ, dtype=jnp.int32)[None, :]).astype(jnp.int32)
    return {
        "x": x,
        "positions": positions,
        "attn_pre_norm": gain(ks[2], (DEPTH, D_MODEL)),
        "attn_post_norm": gain(ks[3], (DEPTH, D_MODEL)),
        "w_in": w(ks[4], (DEPTH, D_MODEL, D_IN), D_MODEL),
        "mla_q_norm": gain(ks[5], (DEPTH, MLA_Q_RANK)),
        "mla_kv_norm": gain(ks[6], (DEPTH, MLA_KV_RANK)),
        "w_uq": w(ks[7], (DEPTH, MLA_Q_RANK, MLA_HEADS * (MLA_NOPE + MLA_ROPE)), MLA_Q_RANK),
        "w_ukv": w(ks[8], (DEPTH, MLA_KV_RANK, MLA_HEADS * (MLA_NOPE + MLA_V)), MLA_KV_RANK),
        "mla_out_norm": gain(ks[9], (DEPTH, MLA_WIDTH)),
        "hg_lower_bounds": jax.random.normal(ks[10], (DEPTH, HG_HEADS * HG_KDIM), f32),
        "hg_out_norm": gain(ks[11], (DEPTH, HG_WIDTH)),
        "ca_rel_bias": 0.5 * jax.random.normal(ks[12], (DEPTH, CA_HEADS, 2 * CA_REL_CLIP + 1), f32),
        "ca_out_norm": gain(ks[13], (DEPTH, CA_WIDTH)),
        "w_out": w(ks[14], (DEPTH, MIX_WIDTH, D_MODEL), MIX_WIDTH),
        "ffn_pre_norm": gain(ks[15], (DEPTH, D_MODEL)),
        "ffn_post_norm": gain(ks[16], (DEPTH, D_MODEL)),
        "w_gate": w(ks[17], (DEPTH, D_MODEL, D_FF), D_MODEL),
        "w_up": w(ks[18], (DEPTH, D_MODEL, D_FF), D_MODEL),
        "w_down": w(ks[19], (DEPTH, D_FF, D_MODEL), D_FF),
    }


def reference(x, positions, attn_pre_norm, attn_post_norm, w_in, mla_q_norm, mla_kv_norm,
              w_uq, w_ukv, mla_out_norm, hg_lower_bounds, hg_out_norm, ca_rel_bias,
              ca_out_norm, w_out, ffn_pre_norm, ffn_post_norm, w_gate, w_up, w_down):
    cos, sin = rope_tables(positions)
    p = jax.nn.softmax(hg_lower_bounds.astype(jnp.float32), axis=0)
    lower_bounds = jnp.cumsum(p, axis=0) - p[0]
    for l in range(DEPTH):
        h = rms_norm(x, attn_pre_norm[l]) @ w_in[l]
        cq, ckv, kr, hq, hf, hi, hg, aq, ak, av = jnp.split(h, IN_OFFSETS, axis=-1)
        o_mla = mla_mixer(cq, ckv, kr, mla_q_norm[l], mla_kv_norm[l], w_uq[l], w_ukv[l],
                          mla_out_norm[l], cos, sin)
        o_hg = hgrn2_mixer(hq, hf, hi, hg, lower_bounds[l], hg_out_norm[l])
        o_ca = chunk_attn_mixer(aq, ak, av, ca_rel_bias[l], ca_out_norm[l])
        y = jnp.concatenate([o_mla, o_hg.astype(o_mla.dtype), o_ca], axis=-1) @ w_out[l]
        x = x + rms_norm(y, attn_post_norm[l])
        hf2 = rms_norm(x, ffn_pre_norm[l])
        y = (jax.nn.silu(hf2 @ w_gate[l]) * (hf2 @ w_up[l])) @ w_down[l]
        x = x + rms_norm(y, ffn_post_norm[l])
    return x
```

```python
import functools
import math

import jax
import jax.numpy as jnp
from jax import lax
from jax.experimental import pallas as pl
from jax.experimental.pallas import tpu as pltpu

F32 = jnp.float32
BF16 = jnp.bfloat16

EPS = 1e-6
MASK_VALUE = -1e30
TINY = 1e-30
CHUNK = 64

MLA_HEADS = 16
MLA_Q_RANK = 768
MLA_KV_RANK = 512
MLA_NOPE = 128
MLA_ROPE = 64
MLA_V = 128
ROPE_THETA = 10000.0
MLA_QK_PAD = 256

HG_HEADS = 8
HG_DIM = 128
HG_BLOCK = 16
HG_CHUNK = 128
HG_ROWS = 512

CA_HEADS = 8
CA_DIM = 128
CA_LEFT_CHUNKS = 8
CA_REL_CLIP = 256
CA_TQ = 256
CA_WIN = 3 * CA_TQ
CA_BIAS_LEN = 1024

MLA_WIDTH = MLA_HEADS * MLA_V
HG_WIDTH = HG_HEADS * HG_DIM
CA_WIDTH = CA_HEADS * CA_DIM

COL_AQ = 0
COL_AK = 1024
COL_AV = 2048
COL_HQ = 3072
COL_HF = 4096
COL_HI = 5120
COL_HG = 6144
COL_CKV = 7168
COL_CQ = 7680
COL_KR = 8448
D_IN_PAD = 8704

LANE = 128
VMEM_LIMIT = 56 * 1024 * 1024


def _cparams(sem, vmem=VMEM_LIMIT):
    return pltpu.CompilerParams(dimension_semantics=sem, vmem_limit_bytes=vmem)


def _rms_scale(x):
    return lax.rsqrt(jnp.mean(x * x, axis=-1, keepdims=True) + EPS)


NORM_ROWS = 32


def _norm_rows_into(xn_ref, x_ref, g_ref):
    rows = x_ref.shape[0]

    def body(c, carry):
        r = pl.ds(pl.multiple_of(c * NORM_ROWS, NORM_ROWS), NORM_ROWS)
        x = x_ref[r, :]
        xn_ref[r, :] = ((x * _rms_scale(x)) * g_ref[...]).astype(BF16)
        return carry

    lax.fori_loop(0, rows // NORM_ROWS, body, 0)


def _norm_matmul_kernel(x_ref, g_ref, w_ref, o_ref, xn_ref):
    @pl.when(pl.program_id(1) == 0)
    def _():
        _norm_rows_into(xn_ref, x_ref, g_ref)

    o_ref[...] = jnp.dot(xn_ref[...], w_ref[...], preferred_element_type=F32).astype(o_ref.dtype)


def _norm_swiglu_kernel(x_ref, g_ref, wg_ref, wu_ref, o_ref, xn_ref):
    @pl.when(pl.program_id(1) == 0)
    def _():
        _norm_rows_into(xn_ref, x_ref, g_ref)

    xn = xn_ref[...]
    gate = jnp.dot(xn, wg_ref[...], preferred_element_type=F32)
    up = jnp.dot(xn, wu_ref[...], preferred_element_type=F32)
    o_ref[...] = ((gate * jax.nn.sigmoid(gate)) * up).astype(o_ref.dtype)


def _norm_matmul(x, g, ws, out_dtype, tm, tn):
    t, d = x.shape
    n = ws[0].shape[1]
    kern = _norm_matmul_kernel if len(ws) == 1 else _norm_swiglu_kernel
    w_specs = [pl.BlockSpec((d, tn), lambda i, j: (0, j)) for _ in ws]
    return pl.pallas_call(
        kern,
        grid=(t // tm, n // tn),
        in_specs=[pl.BlockSpec((tm, d), lambda i, j: (i, 0)),
                  pl.BlockSpec((1, d), lambda i, j: (0, 0))] + w_specs,
        out_specs=pl.BlockSpec((tm, tn), lambda i, j: (i, j)),
        out_shape=jax.ShapeDtypeStruct((t, n), out_dtype),
        scratch_shapes=[pltpu.VMEM((tm, d), BF16)],
        compiler_params=_cparams(("parallel", "arbitrary")),
        name="norm_matmul" if len(ws) == 1 else "norm_swiglu",
    )(x, g.reshape(1, d), *ws)


def _rope_tables(pos_ref):
    lane = lax.broadcasted_iota(jnp.int32, (1, LANE), 1)
    half = MLA_ROPE // 2
    idx = (lane % half).astype(F32)
    inv_freq = jnp.exp((-math.log(ROPE_THETA) * 2.0) * idx / MLA_ROPE)
    inv_freq = jnp.where(lane < MLA_ROPE, inv_freq, 0.0)
    ang = pos_ref[...].astype(F32) * inv_freq
    cos, sin = jnp.cos(ang), jnp.sin(ang)
    sin_hi = jnp.where((lane >= half) & (lane < MLA_ROPE), sin, 0.0)
    sin_lo = jnp.where(lane < half, -sin, 0.0)
    return cos, sin_hi, sin_lo


def _rope(x, tables):
    cos, sin_hi, sin_lo = tables
    half = MLA_ROPE // 2
    return x * cos + pltpu.roll(x, half, 1) * sin_hi + pltpu.roll(x, LANE - half, 1) * sin_lo


def _mla_proj_kernel(cq_ref, ckv_ref, kr_ref, pos_ref, gq_ref, gkv_ref, wq_ref, wk_ref, wv_ref,
                     q_ref, k_ref, v_ref):
    tables = _rope_tables(pos_ref)
    cq = cq_ref[...]
    cqn = ((cq * _rms_scale(cq)) * gq_ref[...]).astype(BF16)
    ckv = ckv_ref[...]
    ckvn = ((ckv * _rms_scale(ckv)) * gkv_ref[...]).astype(BF16)
    k_pe = _rope(kr_ref[...], tables).astype(BF16)
    for h in range(MLA_HEADS):
        c0 = h * MLA_QK_PAD
        qh = jnp.dot(cqn, wq_ref[:, c0:c0 + MLA_QK_PAD], preferred_element_type=F32)
        q_ref[:, c0:c0 + LANE] = qh[:, :LANE].astype(BF16)
        q_ref[:, c0 + LANE:c0 + MLA_QK_PAD] = _rope(qh[:, LANE:], tables).astype(BF16)
        kh = jnp.dot(ckvn, wk_ref[:, h * LANE:(h + 1) * LANE], preferred_element_type=F32)
        k_ref[:, c0:c0 + LANE] = kh.astype(BF16)
        k_ref[:, c0 + LANE:c0 + MLA_QK_PAD] = k_pe
    v_ref[...] = jnp.dot(ckvn, wv_ref[...], preferred_element_type=F32).astype(BF16)


def _mla_proj(h, positions, gq, gkv, wq, wk, wv, tm):
    t = h.shape[0]
    qk_w = MLA_HEADS * MLA_QK_PAD
    const = lambda i: (0, 0)
    return pl.pallas_call(
        _mla_proj_kernel,
        grid=(t // tm,),
        in_specs=[pl.BlockSpec((tm, MLA_Q_RANK), lambda i: (i, COL_CQ // MLA_Q_RANK)),
                  pl.BlockSpec((tm, MLA_KV_RANK), lambda i: (i, COL_CKV // MLA_KV_RANK)),
                  pl.BlockSpec((tm, LANE), lambda i: (i, COL_KR // LANE)),
                  pl.BlockSpec((tm, 1), lambda i: (i, 0)),
                  pl.BlockSpec((1, MLA_Q_RANK), const),
                  pl.BlockSpec((1, MLA_KV_RANK), const),
                  pl.BlockSpec(wq.shape, const),
                  pl.BlockSpec(wk.shape, const),
                  pl.BlockSpec(wv.shape, const)],
        out_specs=[pl.BlockSpec((tm, qk_w), lambda i: (i, 0)),
                   pl.BlockSpec((tm, qk_w), lambda i: (i, 0)),
                   pl.BlockSpec((tm, MLA_WIDTH), lambda i: (i, 0))],
        out_shape=[jax.ShapeDtypeStruct((t, qk_w), BF16),
                   jax.ShapeDtypeStruct((t, qk_w), BF16),
                   jax.ShapeDtypeStruct((t, MLA_WIDTH), BF16)],
        compiler_params=_cparams(("parallel",)),
        name="mla_proj",
    )(h, h, h, positions, gq.reshape(1, -1), gkv.reshape(1, -1), wq, wk, wv)


MLA_TQ = 512


def _mla_attn_kernel(q_ref, k_ref, v_ref, o_ref, m_ref, l_ref, acc_ref):
    i = pl.program_id(2)
    tq = q_ref.shape[0]
    scale = (MLA_NOPE + MLA_ROPE) ** -0.5
    q = q_ref[...]
    m_ref[...] = jnp.full(m_ref.shape, -jnp.inf, F32)
    l_ref[...] = jnp.zeros(l_ref.shape, F32)
    acc_ref[...] = jnp.zeros(acc_ref.shape, F32)

    def step(j, masked):
        r = pl.ds(pl.multiple_of(j * tq, tq), tq)
        s = lax.dot_general(q, k_ref[r, :], (((1,), (1,)), ((), ())),
                            preferred_element_type=F32) * scale
        if masked:
            qc = lax.broadcasted_iota(jnp.int32, s.shape, 0) // CHUNK
            kc = lax.broadcasted_iota(jnp.int32, s.shape, 1) // CHUNK
            s = jnp.where(kc <= qc, s, MASK_VALUE)
        m_old = m_ref[...]
        m_new = jnp.maximum(m_old, jnp.max(s, axis=-1, keepdims=True))
        alpha = jnp.exp(m_old - m_new)
        p = jnp.exp(s - m_new)
        l_ref[...] = alpha * l_ref[...] + jnp.sum(p, axis=-1, keepdims=True)
        acc_ref[...] = alpha * acc_ref[...] + jnp.dot(p.astype(BF16), v_ref[r, :],
                                                      preferred_element_type=F32)
        m_ref[...] = m_new

    def body(j, carry):
        step(j, False)
        return carry

    lax.fori_loop(0, i, body, 0)
    step(i, True)
    o_ref[...] = acc_ref[...] / l_ref[...]


def _mla_attn(q, k, v, batch, seq):
    tq = MLA_TQ
    nq = seq // tq
    return pl.pallas_call(
        _mla_attn_kernel,
        grid=(batch, MLA_HEADS, nq),
        in_specs=[pl.BlockSpec((tq, MLA_QK_PAD), lambda b, h, i: (b * nq + i, h)),
                  pl.BlockSpec((seq, MLA_QK_PAD), lambda b, h, i: (b, h)),
                  pl.BlockSpec((seq, MLA_V), lambda b, h, i: (b, h))],
        out_specs=pl.BlockSpec((tq, MLA_V), lambda b, h, i: (b * nq + i, h)),
        out_shape=jax.ShapeDtypeStruct((batch * seq, MLA_WIDTH), F32),
        scratch_shapes=[pltpu.VMEM((tq, 1), F32), pltpu.VMEM((tq, 1), F32),
                        pltpu.VMEM((tq, MLA_V), F32)],
        compiler_params=_cparams(("parallel", "parallel", "arbitrary")),
        name="mla_attn",
    )(q, k, v)


def _block_cumsum(x, row):
    r = row % HG_BLOCK
    s = 1
    while s < HG_BLOCK:
        x = x + jnp.where(r >= s, pltpu.roll(x, s, 0), 0.0)
        s *= 2
    return x


def _hgrn_chunk(hq, hf, hi, lb, state):
    c = HG_CHUNK
    nb = c // HG_BLOCK
    row = lax.broadcasted_iota(jnp.int32, (c, HG_DIM), 0)
    col = lax.broadcasted_iota(jnp.int32, (c, HG_DIM), 1)

    q = hq * jax.nn.sigmoid(hq)
    f = lb + (1.0 - lb) * jax.nn.sigmoid(hf)
    k = (1.0 - lb) * jax.nn.sigmoid(-hf)
    b = _block_cumsum(jnp.log(jnp.maximum(f, TINY)), row)
    b3 = b.reshape(nb, HG_BLOCK, HG_DIM)
    b_last3 = jnp.broadcast_to(b3[:, HG_BLOCK - 1:HG_BLOCK, :], b3.shape)
    b_last = b_last3.reshape(c, HG_DIM)

    q3 = q.reshape(nb, HG_BLOCK, HG_DIM)
    k3 = k.reshape(nb, HG_BLOCK, HG_DIM)
    v3 = hi.reshape(nb, HG_BLOCK, HG_DIM)
    irow = lax.broadcasted_iota(jnp.int32, b3.shape, 1)
    o3 = jnp.zeros(b3.shape, F32)
    for j in range(HG_BLOCK):
        diff = jnp.minimum(b3 - b3[:, j:j + 1, :], 0.0)
        w = jnp.where(irow >= j, q3 * k3[:, j:j + 1, :] * jnp.exp(diff), 0.0)
        o3 = o3 + jnp.sum(w, axis=-1, keepdims=True) * v3[:, j:j + 1, :]
    o = o3.reshape(c, HG_DIM)

    q_dec = q * jnp.exp(b)
    k_dec_t = (k * jnp.exp(b_last - b)).T
    b_last_t = b_last.T
    v_bf = hi.astype(BF16)
    blk_of_col = col // HG_BLOCK
    k_stack = jnp.concatenate(
        [jnp.where(blk_of_col == j, k_dec_t, 0.0) for j in range(nb)], axis=0).astype(BF16)
    u_all = jnp.dot(k_stack, v_bf, preferred_element_type=F32)
    states = []
    for j in range(nb):
        states.append(state.astype(BF16))
        decay = jnp.exp(b_last_t[:, j * HG_BLOCK:j * HG_BLOCK + 1])
        state = decay * state + u_all[j * HG_DIM:(j + 1) * HG_DIM, :]
    s_stack = jnp.concatenate(states, axis=0)
    blk_of_row = row // HG_BLOCK
    q_exp = jnp.concatenate(
        [jnp.where(blk_of_row == j, q_dec, 0.0) for j in range(nb)], axis=1).astype(BF16)
    o = o + jnp.dot(q_exp, s_stack, preferred_element_type=F32)
    return o, state


def _hgrn_kernel(hq_ref, hf_ref, hi_ref, hg_ref, lbraw_ref, gn_ref, o_ref, state_ref, *, layer):
    @pl.when(pl.program_id(2) == 0)
    def _():
        state_ref[...] = jnp.zeros(state_ref.shape, F32)

    raw = lbraw_ref[...]
    e = jnp.exp(raw - jnp.max(raw, axis=0, keepdims=True))
    p = e / jnp.sum(e, axis=0, keepdims=True)
    lb = jnp.sum(p[:layer + 1, :], axis=0, keepdims=True) - p[0:1, :]

    def body(ci, carry):
        r = pl.ds(pl.multiple_of(ci * HG_CHUNK, HG_CHUNK), HG_CHUNK)
        o, state = _hgrn_chunk(hq_ref[r, :], hf_ref[r, :], hi_ref[r, :], lb, state_ref[...])
        state_ref[...] = state
        o = (o * _rms_scale(o)) * gn_ref[...]
        g = hg_ref[r, :]
        o_ref[r, :] = (o * (g * jax.nn.sigmoid(g))).astype(o_ref.dtype)
        return carry

    lax.fori_loop(0, hq_ref.shape[0] // HG_CHUNK, body, 0)


def _hgrn(h, lb_raw, gn, layer, batch, seq):
    rows = HG_ROWS
    nr = seq // rows
    depth = lb_raw.shape[0]

    def col_spec(col0):
        return pl.BlockSpec((rows, HG_DIM), lambda b, hh, c: (b * nr + c, col0 // HG_DIM + hh))

    return pl.pallas_call(
        functools.partial(_hgrn_kernel, layer=layer),
        grid=(batch, HG_HEADS, nr),
        in_specs=[col_spec(COL_HQ), col_spec(COL_HF), col_spec(COL_HI), col_spec(COL_HG),
                  pl.BlockSpec((depth, HG_DIM), lambda b, hh, c: (0, hh)),
                  pl.BlockSpec((1, HG_DIM), lambda b, hh, c: (0, hh))],
        out_specs=pl.BlockSpec((rows, HG_DIM), lambda b, hh, c: (b * nr + c, hh)),
        out_shape=jax.ShapeDtypeStruct((batch * seq, HG_WIDTH), BF16),
        scratch_shapes=[pltpu.VMEM((HG_DIM, HG_DIM), F32)],
        compiler_params=_cparams(("parallel", "parallel", "arbitrary")),
        name="hgrn",
    )(h, h, h, h, lb_raw, gn.reshape(1, -1))


def _ca_bias_rows(rel_bias):
    idx = jnp.arange(CA_BIAS_LEN)
    m = jnp.where(idx < CA_WIN, idx, idx - CA_BIAS_LEN)
    bucket = jnp.clip(2 * CA_TQ - m, -CA_REL_CLIP, CA_REL_CLIP) + CA_REL_CLIP
    return rel_bias[:, bucket].astype(F32)


def _ca_kernel(q_ref, k0_ref, k1_ref, k2_ref, v0_ref, v1_ref, v2_ref, brow_ref, gn_ref,
               o_ref, bias_ref):
    t = pl.program_id(1)
    tq = CA_TQ

    @pl.when((pl.program_id(0) == 0) & (t == 0))
    def _():
        qc = lax.broadcasted_iota(jnp.int32, (tq, CA_WIN), 0) // CHUNK
        kc = lax.broadcasted_iota(jnp.int32, (tq, CA_WIN), 1) // CHUNK
        band = (kc >= qc) & (kc <= qc + CA_LEFT_CHUNKS)
        for h in range(CA_HEADS):
            rows = jnp.broadcast_to(brow_ref[h:h + 1, :], (tq, CA_BIAS_LEN))
            rolled = pltpu.roll(rows, 0, 1, stride=1, stride_axis=0)
            bias_ref[h] = jnp.where(band, rolled[:, :CA_WIN], MASK_VALUE)

    first_valid = jnp.maximum(2 * tq - t * tq, 0)
    kk = lax.broadcasted_iota(jnp.int32, (tq, CA_WIN), 1)
    seq_ok = kk >= first_valid
    scale = CA_DIM ** -0.5
    outs = []
    for h in range(CA_HEADS):
        c = slice(h * CA_DIM, (h + 1) * CA_DIM)
        qh = q_ref[:, c].astype(BF16)
        kh = jnp.concatenate([k0_ref[:, c], k1_ref[:, c], k2_ref[:, c]], axis=0).astype(BF16)
        vh = jnp.concatenate([v0_ref[:, c], v1_ref[:, c], v2_ref[:, c]], axis=0).astype(BF16)
        s = lax.dot_general(qh, kh, (((1,), (1,)), ((), ())), preferred_element_type=F32) * scale
        s = jnp.where(seq_ok, s + bias_ref[h], MASK_VALUE)
        p = jnp.exp(s - jnp.max(s, axis=-1, keepdims=True))
        l = jnp.sum(p, axis=-1, keepdims=True)
        outs.append(jnp.dot(p.astype(BF16), vh, preferred_element_type=F32) / l)
    o = jnp.concatenate(outs, axis=1)
    o_ref[...] = ((o * _rms_scale(o)) * gn_ref[...]).astype(o_ref.dtype)


def _chunk_attn(h, rel_bias, gn, batch, seq):
    tq = CA_TQ
    nq = seq // tq
    cq, ck, cv = COL_AQ // CA_WIDTH, COL_AK // CA_WIDTH, COL_AV // CA_WIDTH

    def kv_spec(col, back):
        return pl.BlockSpec((tq, CA_WIDTH), lambda b, t: (b * nq + jnp.maximum(t - back, 0), col))

    return pl.pallas_call(
        _ca_kernel,
        grid=(batch, nq),
        in_specs=[pl.BlockSpec((tq, CA_WIDTH), lambda b, t: (b * nq + t, cq)),
                  kv_spec(ck, 2), kv_spec(ck, 1), kv_spec(ck, 0),
                  kv_spec(cv, 2), kv_spec(cv, 1), kv_spec(cv, 0),
                  pl.BlockSpec((CA_HEADS, CA_BIAS_LEN), lambda b, t: (0, 0)),
                  pl.BlockSpec((1, CA_WIDTH), lambda b, t: (0, 0))],
        out_specs=pl.BlockSpec((tq, CA_WIDTH), lambda b, t: (b * nq + t, 0)),
        out_shape=jax.ShapeDtypeStruct((batch * seq, CA_WIDTH), BF16),
        scratch_shapes=[pltpu.VMEM((CA_HEADS, tq, CA_WIN), F32)],
        compiler_params=_cparams(("arbitrary", "arbitrary")),
        name="chunk_attn",
    )(h, h, h, h, h, h, h, _ca_bias_rows(rel_bias), gn.reshape(1, -1))


def _residual_norm(o_ref, x_ref, g_ref):
    rows = o_ref.shape[0]

    def body(c, carry):
        r = pl.ds(pl.multiple_of(c * NORM_ROWS, NORM_ROWS), NORM_ROWS)
        y = o_ref[r, :]
        o_ref[r, :] = x_ref[r, :] + (y * _rms_scale(y)) * g_ref[...]
        return carry

    lax.fori_loop(0, rows // NORM_ROWS, body, 0)


def _accumulate_then_residual_norm(a, w_ref, x_ref, g_ref, o_ref):
    k = pl.program_id(1)

    @pl.when(k == 0)
    def _():
        o_ref[...] = jnp.zeros(o_ref.shape, F32)

    o_ref[...] += jnp.dot(a, w_ref[...], preferred_element_type=F32)

    @pl.when(k == pl.num_programs(1) - 1)
    def _():
        _residual_norm(o_ref, x_ref, g_ref)


OUT_TK = 1024


def _out_proj_kernel(mla_ref, hg_ref, ca_ref, w_ref, x_ref, gm_ref, gp_ref, o_ref, a_ref):
    @pl.when(pl.program_id(1) == 0)
    def _():
        rows = mla_ref.shape[0]

        def body(c, carry):
            r = pl.ds(pl.multiple_of(c * NORM_ROWS, NORM_ROWS), NORM_ROWS)
            m = mla_ref[r, :]
            mn = ((m * _rms_scale(m)) * gm_ref[...]).astype(BF16)
            for p in range(MLA_WIDTH // OUT_TK):
                a_ref[p, r, :] = mn[:, p * OUT_TK:(p + 1) * OUT_TK]
            return carry

        lax.fori_loop(0, rows // NORM_ROWS, body, 0)
        a_ref[MLA_WIDTH // OUT_TK] = hg_ref[...]
        a_ref[MLA_WIDTH // OUT_TK + 1] = ca_ref[...]

    _accumulate_then_residual_norm(a_ref[pl.program_id(1)], w_ref, x_ref, gp_ref, o_ref)


def _out_proj(o_mla, o_hg, o_ca, x, g_mla, g_post, w, tm):
    t, d = x.shape
    kdim = w.shape[0]
    assert HG_WIDTH == OUT_TK and CA_WIDTH == OUT_TK and MLA_WIDTH % OUT_TK == 0
    assert kdim == MLA_WIDTH + HG_WIDTH + CA_WIDTH
    return pl.pallas_call(
        _out_proj_kernel,
        grid=(t // tm, kdim // OUT_TK),
        in_specs=[pl.BlockSpec((tm, MLA_WIDTH), lambda i, k: (i, 0)),
                  pl.BlockSpec((tm, HG_WIDTH), lambda i, k: (i, 0)),
                  pl.BlockSpec((tm, CA_WIDTH), lambda i, k: (i, 0)),
                  pl.BlockSpec((OUT_TK, d), lambda i, k: (k, 0)),
                  pl.BlockSpec((tm, d), lambda i, k: (i, 0)),
                  pl.BlockSpec((1, MLA_WIDTH), lambda i, k: (0, 0)),
                  pl.BlockSpec((1, d), lambda i, k: (0, 0))],
        out_specs=pl.BlockSpec((tm, d), lambda i, k: (i, 0)),
        out_shape=jax.ShapeDtypeStruct((t, d), F32),
        scratch_shapes=[pltpu.VMEM((kdim // OUT_TK, tm, OUT_TK), BF16)],
        compiler_params=_cparams(("parallel", "arbitrary")),
        name="out_proj",
    )(o_mla, o_hg, o_ca, w, x, g_mla.reshape(1, -1), g_post.reshape(1, -1))


def _down_proj_kernel(a_ref, w_ref, x_ref, g_ref, o_ref):
    _accumulate_then_residual_norm(a_ref[...], w_ref, x_ref, g_ref, o_ref)


def _down_proj(a, w, x, g, tm, tk):
    t, d = x.shape
    kdim = a.shape[1]
    return pl.pallas_call(
        _down_proj_kernel,
        grid=(t // tm, kdim // tk),
        in_specs=[pl.BlockSpec((tm, tk), lambda i, k: (i, k)),
                  pl.BlockSpec((tk, d), lambda i, k: (k, 0)),
                  pl.BlockSpec((tm, d), lambda i, k: (i, 0)),
                  pl.BlockSpec((1, d), lambda i, k: (0, 0))],
        out_specs=pl.BlockSpec((tm, d), lambda i, k: (i, 0)),
        out_shape=jax.ShapeDtypeStruct((t, d), F32),
        compiler_params=_cparams(("parallel", "arbitrary")),
        name="down_proj",
    )(a, w, x, g.reshape(1, -1))


def _prep_w_in(w):
    d = w.shape[0]
    sizes = (MLA_Q_RANK, MLA_KV_RANK, MLA_ROPE, HG_WIDTH, HG_WIDTH, HG_WIDTH, HG_WIDTH,
             CA_WIDTH, CA_WIDTH, CA_WIDTH)
    offs = [0]
    for s in sizes:
        offs.append(offs[-1] + s)
    cq, ckv, kr, hq, hf, hi, hg, aq, ak, av = (w[:, offs[n]:offs[n + 1]] for n in range(len(sizes)))
    z = jnp.zeros((d, D_IN_PAD - COL_KR - MLA_ROPE), w.dtype)
    out = jnp.concatenate([aq, ak, av, hq, hf, hi, hg, ckv, cq, kr, z], axis=1)
    assert out.shape[1] == D_IN_PAD
    return out.astype(BF16)


def _prep_w_uq(w):
    r = w.shape[0]
    w3 = w.reshape(r, MLA_HEADS, MLA_NOPE + MLA_ROPE)
    pad = jnp.zeros((r, MLA_HEADS, MLA_QK_PAD - MLA_NOPE - MLA_ROPE), w.dtype)
    return jnp.concatenate([w3, pad], axis=2).reshape(r, MLA_HEADS * MLA_QK_PAD).astype(BF16)


def _prep_w_ukv(w):
    r = w.shape[0]
    w3 = w.reshape(r, MLA_HEADS, MLA_NOPE + MLA_V)
    wk = w3[:, :, :MLA_NOPE].reshape(r, MLA_HEADS * MLA_NOPE)
    wv = w3[:, :, MLA_NOPE:].reshape(r, MLA_HEADS * MLA_V)
    return wk.astype(BF16), wv.astype(BF16)


FF_ALIGN = 1024


def _pad_cols(w, n):
    return jnp.pad(w, ((0, 0), (0, n - w.shape[1])))


def _pad_rows(w, n):
    return jnp.pad(w, ((0, n - w.shape[0]), (0, 0)))


def kernel(x, positions, attn_pre_norm, attn_post_norm, w_in, mla_q_norm, mla_kv_norm, w_uq, w_ukv,
           mla_out_norm, hg_lower_bounds, hg_out_norm, ca_rel_bias, ca_out_norm, w_out, ffn_pre_norm,
           ffn_post_norm, w_gate, w_up, w_down):
    batch, seq, d = x.shape
    t = batch * seq
    depth = w_in.shape[0]
    d_ff = w_gate.shape[2]
    d_ff_pad = -(-d_ff // FF_ALIGN) * FF_ALIGN
    xf = x.reshape(t, d)
    pos = positions.reshape(t, 1)
    for l in range(depth):
        h = _norm_matmul(xf, attn_pre_norm[l], [_prep_w_in(w_in[l])], F32, tm=512, tn=512)
        wk, wv = _prep_w_ukv(w_ukv[l])
        q, k, v = _mla_proj(h, pos, mla_q_norm[l], mla_kv_norm[l], _prep_w_uq(w_uq[l]), wk, wv, tm=256)
        o_mla = _mla_attn(q, k, v, batch, seq)
        o_hg = _hgrn(h, hg_lower_bounds, hg_out_norm[l], l, batch, seq)
        o_ca = _chunk_attn(h, ca_rel_bias[l], ca_out_norm[l], batch, seq)
        xf = _out_proj(o_mla, o_hg, o_ca, xf, mla_out_norm[l], attn_post_norm[l],
                       w_out[l].astype(BF16), tm=256)
        hid = _norm_matmul(xf, ffn_pre_norm[l],
                           [_pad_cols(w_gate[l], d_ff_pad).astype(BF16),
                            _pad_cols(w_up[l], d_ff_pad).astype(BF16)], BF16, tm=512, tn=512)
        xf = _down_proj(hid, _pad_rows(w_down[l], d_ff_pad).astype(BF16), xf, ffn_post_norm[l],
                        tm=512, tk=1024)
    return xf.reshape(batch, seq, d)
```

```python
import functools
import math

import jax
import jax.numpy as jnp
from jax import lax
from jax.experimental import pallas as pl
from jax.experimental.pallas import tpu as pltpu

F32 = jnp.float32
BF16 = jnp.bfloat16

EPS = 1e-6
MASK_VALUE = -1e30
TINY = 1e-30
CHUNK = 64

MLA_HEADS = 16
MLA_Q_RANK = 768
MLA_KV_RANK = 512
MLA_NOPE = 128
MLA_ROPE = 64
MLA_V = 128
ROPE_THETA = 10000.0
MLA_QK_PAD = 256

HG_HEADS = 8
HG_DIM = 128
HG_BLOCK = 16
HG_CHUNK = 128
HG_ROWS = 512

CA_HEADS = 8
CA_DIM = 128
CA_LEFT_CHUNKS = 8
CA_REL_CLIP = 256
CA_TQ = 256
CA_WIN = 3 * CA_TQ
CA_BIAS_LEN = 1024

MLA_WIDTH = MLA_HEADS * MLA_V
HG_WIDTH = HG_HEADS * HG_DIM
CA_WIDTH = CA_HEADS * CA_DIM

COL_AQ = 0
COL_AK = 1024
COL_AV = 2048
COL_HQ = 3072
COL_HF = 4096
COL_HI = 5120
COL_HG = 6144
COL_CKV = 7168
COL_CQ = 7680
COL_KR = 8448
D_IN_PAD = 8704

LANE = 128
VMEM_LIMIT = 56 * 1024 * 1024


def _cparams(sem, vmem=VMEM_LIMIT):
    return pltpu.CompilerParams(dimension_semantics=sem, vmem_limit_bytes=vmem)


def _rms_scale(x):
    return lax.rsqrt(jnp.mean(x * x, axis=-1, keepdims=True) + EPS)


NORM_ROWS = 32


def _norm_rows_into(xn_ref, x_ref, g_ref):
    rows = x_ref.shape[0]

    def body(c, carry):
        r = pl.ds(pl.multiple_of(c * NORM_ROWS, NORM_ROWS), NORM_ROWS)
        x = x_ref[r, :]
        xn_ref[r, :] = ((x * _rms_scale(x)) * g_ref[...]).astype(BF16)
        return carry

    lax.fori_loop(0, rows // NORM_ROWS, body, 0)


def _norm_matmul_kernel(x_ref, g_ref, w_ref, o_ref, xn_ref):
    @pl.when(pl.program_id(1) == 0)
    def _():
        _norm_rows_into(xn_ref, x_ref, g_ref)

    o_ref[...] = jnp.dot(xn_ref[...], w_ref[...], preferred_element_type=F32).astype(o_ref.dtype)


def _norm_swiglu_kernel(x_ref, g_ref, wg_ref, wu_ref, o_ref, xn_ref):
    @pl.when(pl.program_id(1) == 0)
    def _():
        _norm_rows_into(xn_ref, x_ref, g_ref)

    xn = xn_ref[...]
    gate = jnp.dot(xn, wg_ref[...], preferred_element_type=F32)
    up = jnp.dot(xn, wu_ref[...], preferred_element_type=F32)
    o_ref[...] = ((gate * jax.nn.sigmoid(gate)) * up).astype(o_ref.dtype)


def _norm_matmul(x, g, ws, out_dtype, tm, tn):
    t, d = x.shape
    n = ws[0].shape[1]
    kern = _norm_matmul_kernel if len(ws) == 1 else _norm_swiglu_kernel
    w_specs = [pl.BlockSpec((d, tn), lambda i, j: (0, j)) for _ in ws]
    return pl.pallas_call(
        kern,
        grid=(t // tm, pl.cdiv(n, tn)),
        in_specs=[pl.BlockSpec((tm, d), lambda i, j: (i, 0), pipeline_mode=pl.Buffered(1)),
                  pl.BlockSpec((1, d), lambda i, j: (0, 0))] + w_specs,
        out_specs=pl.BlockSpec((tm, tn), lambda i, j: (i, j)),
        out_shape=jax.ShapeDtypeStruct((t, n), out_dtype),
        scratch_shapes=[pltpu.VMEM((tm, d), BF16)],
        compiler_params=_cparams(("parallel", "arbitrary")),
        name="norm_matmul" if len(ws) == 1 else "norm_swiglu",
    )(x, g.reshape(1, d), *ws)


def _rope_tables(pos_ref):
    lane = lax.broadcasted_iota(jnp.int32, (1, LANE), 1)
    half = MLA_ROPE // 2
    idx = (lane % half).astype(F32)
    inv_freq = jnp.exp((-math.log(ROPE_THETA) * 2.0) * idx / MLA_ROPE)
    inv_freq = jnp.where(lane < MLA_ROPE, inv_freq, 0.0)
    ang = pos_ref[...].astype(F32) * inv_freq
    cos, sin = jnp.cos(ang), jnp.sin(ang)
    sin_hi = jnp.where((lane >= half) & (lane < MLA_ROPE), sin, 0.0)
    sin_lo = jnp.where(lane < half, -sin, 0.0)
    return cos, sin_hi, sin_lo


def _rope(x, tables):
    cos, sin_hi, sin_lo = tables
    half = MLA_ROPE // 2
    return x * cos + pltpu.roll(x, half, 1) * sin_hi + pltpu.roll(x, LANE - half, 1) * sin_lo


def _mla_proj_kernel(cq_ref, ckv_ref, kr_ref, pos_ref, gq_ref, gkv_ref, wq_ref, wk_ref, wvt_ref,
                     q_ref, k_ref, vt_ref):
    tables = _rope_tables(pos_ref)
    cq = cq_ref[...]
    cqn = ((cq * _rms_scale(cq)) * gq_ref[...]).astype(BF16)
    ckv = ckv_ref[...]
    ckvn = ((ckv * _rms_scale(ckv)) * gkv_ref[...]).astype(BF16)
    k_pe = _rope(kr_ref[...], tables).astype(BF16)
    for h in range(MLA_HEADS):
        c0 = h * MLA_QK_PAD
        qh = jnp.dot(cqn, wq_ref[:, c0:c0 + MLA_QK_PAD], preferred_element_type=F32)
        q_ref[:, c0:c0 + LANE] = qh[:, :LANE].astype(BF16)
        q_ref[:, c0 + LANE:c0 + MLA_QK_PAD] = _rope(qh[:, LANE:], tables).astype(BF16)
        kh = jnp.dot(ckvn, wk_ref[:, h * LANE:(h + 1) * LANE], preferred_element_type=F32)
        k_ref[:, c0:c0 + LANE] = kh.astype(BF16)
        k_ref[:, c0 + LANE:c0 + MLA_QK_PAD] = k_pe
    rows = 4 * LANE
    for c0 in range(0, MLA_WIDTH, rows):
        vt = lax.dot_general(wvt_ref[c0:c0 + rows, :], ckvn, (((1,), (1,)), ((), ())),
                             preferred_element_type=F32)
        vt_ref[0, c0:c0 + rows, :] = vt.astype(BF16)


def _mla_proj(h, positions, gq, gkv, wq, wk, wvt, tm):
    t = h.shape[0]
    qk_w = MLA_HEADS * MLA_QK_PAD
    const = lambda i: (0, 0)
    return pl.pallas_call(
        _mla_proj_kernel,
        grid=(t // tm,),
        in_specs=[pl.BlockSpec((tm, MLA_Q_RANK), lambda i: (i, COL_CQ // MLA_Q_RANK)),
                  pl.BlockSpec((tm, MLA_KV_RANK), lambda i: (i, COL_CKV // MLA_KV_RANK)),
                  pl.BlockSpec((tm, LANE), lambda i: (i, COL_KR // LANE)),
                  pl.BlockSpec((tm, 1), lambda i: (i, 0)),
                  pl.BlockSpec((1, MLA_Q_RANK), const),
                  pl.BlockSpec((1, MLA_KV_RANK), const),
                  pl.BlockSpec(wq.shape, const),
                  pl.BlockSpec(wk.shape, const),
                  pl.BlockSpec(wvt.shape, const)],
        out_specs=[pl.BlockSpec((tm, qk_w), lambda i: (i, 0)),
                   pl.BlockSpec((tm, qk_w), lambda i: (i, 0)),
                   pl.BlockSpec((1, MLA_WIDTH, tm), lambda i: (i, 0, 0))],
        out_shape=[jax.ShapeDtypeStruct((t, qk_w), BF16),
                   jax.ShapeDtypeStruct((t, qk_w), BF16),
                   jax.ShapeDtypeStruct((t // tm, MLA_WIDTH, tm), BF16)],
        compiler_params=_cparams(("parallel",)),
        name="mla_proj",
    )(h, h, h, positions, gq.reshape(1, -1), gkv.reshape(1, -1), wq, wk, wvt)


MLA_TQ = 512


MLA_TK = MLA_TQ // 2


def _mla_attn_kernel(q_ref, k_ref, vt_ref, o_ref, qt_ref, sa_ref, sb_ref, pa_ref, pb_ref,
                     m_ref, l_ref, alpha_ref, acc_ref):
    i = pl.program_id(2)
    tq, tk = MLA_TQ, MLA_TK
    qt_ref[...] = q_ref[...].astype(F32).T.astype(BF16)
    c = (MLA_NOPE + MLA_ROPE) ** -0.5 * math.log2(math.e)
    m_ref[...] = jnp.full(m_ref.shape, -jnp.inf, F32)
    l_ref[...] = jnp.zeros(l_ref.shape, F32)
    alpha_ref[...] = jnp.ones(alpha_ref.shape, F32)
    acc_ref[...] = jnp.zeros(acc_ref.shape, F32)
    pb_ref[...] = jnp.zeros(pb_ref.shape, BF16)

    def scores_into(s_ref, t):
        r = pl.ds(pl.multiple_of(t * tk, tk), tk)
        s_ref[...] = jnp.dot(k_ref[r, :], qt_ref[...], preferred_element_type=F32)

    def add_values(p_ref, vt_tile):
        acc_ref[...] = alpha_ref[...] * acc_ref[...] + jnp.dot(vt_tile, p_ref[...],
                                                               preferred_element_type=F32)

    def softmax_terms(s_ref, p_ref, chunk_shift):
        s = s_ref[...]
        if chunk_shift is not None:
            kc = lax.broadcasted_iota(jnp.int32, s.shape, 0) // CHUNK + chunk_shift
            qc = lax.broadcasted_iota(jnp.int32, s.shape, 1) // CHUNK
            s = jnp.where(kc <= qc, s, MASK_VALUE)
        m_old = m_ref[...]
        m_new = jnp.maximum(m_old, jnp.max(s, axis=0, keepdims=True))
        alpha = jnp.exp2((m_old - m_new) * c)
        p = jnp.exp2((s - m_new) * c)
        l_ref[...] = alpha * l_ref[...] + jnp.sum(p, axis=0, keepdims=True)
        m_ref[...] = m_new
        alpha_ref[...] = alpha
        p_ref[...] = p.astype(BF16)

    def pair(u, diagonal):
        scores_into(sb_ref, 2 * u + 1)
        add_values(pb_ref, vt_ref[jnp.maximum(u - 1, 0)][:, tk:])
        softmax_terms(sa_ref, pa_ref, 0 if diagonal else None)
        if not diagonal:
            scores_into(sa_ref, 2 * u + 2)
        add_values(pa_ref, vt_ref[u][:, :tk])
        softmax_terms(sb_ref, pb_ref, tk // CHUNK if diagonal else None)

    scores_into(sa_ref, 0)

    def body(u, carry):
        pair(u, False)
        return carry

    lax.fori_loop(0, i, body, 0)
    pair(i, True)
    add_values(pb_ref, vt_ref[i][:, tk:])
    o_ref[...] = (acc_ref[...] / l_ref[...]).T


def _mla_attn(q, k, vt, batch, seq):
    tq = MLA_TQ
    nq = seq // tq
    assert vt.shape[2] == tq
    return pl.pallas_call(
        _mla_attn_kernel,
        grid=(batch, MLA_HEADS, nq),
        in_specs=[pl.BlockSpec((tq, MLA_QK_PAD), lambda b, h, i: (b * nq + i, h)),
                  pl.BlockSpec((seq, MLA_QK_PAD), lambda b, h, i: (b, h)),
                  pl.BlockSpec((nq, MLA_V, tq), lambda b, h, i: (b, h, 0))],
        out_specs=pl.BlockSpec((tq, MLA_V), lambda b, h, i: (b * nq + i, h)),
        out_shape=jax.ShapeDtypeStruct((batch * seq, MLA_WIDTH), F32),
        scratch_shapes=[pltpu.VMEM((MLA_QK_PAD, tq), BF16),
                        pltpu.VMEM((MLA_TK, tq), F32), pltpu.VMEM((MLA_TK, tq), F32),
                        pltpu.VMEM((MLA_TK, tq), BF16), pltpu.VMEM((MLA_TK, tq), BF16),
                        pltpu.VMEM((1, tq), F32), pltpu.VMEM((1, tq), F32), pltpu.VMEM((1, tq), F32),
                        pltpu.VMEM((MLA_V, tq), F32)],
        compiler_params=_cparams(("parallel", "parallel", "arbitrary")),
        name="mla_attn",
    )(q, k, vt)


def _block_cumsum(x, row):
    r = row % HG_BLOCK
    s = 1
    while s < HG_BLOCK:
        x = x + jnp.where(r >= s, pltpu.roll(x, s, 0), 0.0)
        s *= 2
    return x


def _hgrn_chunk(hq, hf, hi, lb, state):
    c = HG_CHUNK
    nb = c // HG_BLOCK
    row = lax.broadcasted_iota(jnp.int32, (c, HG_DIM), 0)
    col = lax.broadcasted_iota(jnp.int32, (c, HG_DIM), 1)

    q = hq * jax.nn.sigmoid(hq)
    f = lb + (1.0 - lb) * jax.nn.sigmoid(hf)
    k = (1.0 - lb) * jax.nn.sigmoid(-hf)
    b = _block_cumsum(jnp.log(jnp.maximum(f, TINY)), row)
    b3 = b.reshape(nb, HG_BLOCK, HG_DIM)
    b_last3 = jnp.broadcast_to(b3[:, HG_BLOCK - 1:HG_BLOCK, :], b3.shape)
    b_last = b_last3.reshape(c, HG_DIM)

    q3 = q.reshape(nb, HG_BLOCK, HG_DIM)
    k3 = k.reshape(nb, HG_BLOCK, HG_DIM)
    v3 = hi.reshape(nb, HG_BLOCK, HG_DIM)
    irow = lax.broadcasted_iota(jnp.int32, b3.shape, 1)
    sub = 8
    o_parts = [jnp.zeros((nb, sub, HG_DIM), F32) for _ in range(HG_BLOCK // sub)]
    for j in range(HG_BLOCK):
        lo = (j // sub) * sub
        diff = jnp.minimum(b3[:, lo:, :] - b3[:, j:j + 1, :], 0.0)
        w = jnp.where(irow[:, lo:, :] >= j, q3[:, lo:, :] * k3[:, j:j + 1, :] * jnp.exp(diff), 0.0)
        wv = jnp.sum(w, axis=-1, keepdims=True) * v3[:, j:j + 1, :]
        for part in range(lo // sub, HG_BLOCK // sub):
            r0 = part * sub - lo
            o_parts[part] = o_parts[part] + wv[:, r0:r0 + sub, :]
    o = jnp.concatenate(o_parts, axis=1).reshape(c, HG_DIM)

    q_dec = q * jnp.exp(b)
    k_dec_t = (k * jnp.exp(b_last - b)).T
    b_last_t = b_last.T
    v_bf = hi.astype(BF16)
    blk_of_col = col // HG_BLOCK
    k_stack = jnp.concatenate(
        [jnp.where(blk_of_col == j, k_dec_t, 0.0) for j in range(nb)], axis=0).astype(BF16)
    u_all = jnp.dot(k_stack, v_bf, preferred_element_type=F32)
    states = []
    for j in range(nb):
        states.append(state.astype(BF16))
        decay = jnp.exp(b_last_t[:, j * HG_BLOCK:j * HG_BLOCK + 1])
        state = decay * state + u_all[j * HG_DIM:(j + 1) * HG_DIM, :]
    s_stack = jnp.concatenate(states, axis=0)
    blk_of_row = row // HG_BLOCK
    q_exp = jnp.concatenate(
        [jnp.where(blk_of_row == j, q_dec, 0.0) for j in range(nb)], axis=1).astype(BF16)
    o = o + jnp.dot(q_exp, s_stack, preferred_element_type=F32)
    return o, state


def _hgrn_kernel(hq_ref, hf_ref, hi_ref, hg_ref, lbraw_ref, gn_ref, o_ref, state_ref, *, layer):
    @pl.when(pl.program_id(2) == 0)
    def _():
        state_ref[...] = jnp.zeros(state_ref.shape, F32)

    raw = lbraw_ref[...]
    e = jnp.exp(raw - jnp.max(raw, axis=0, keepdims=True))
    p = e / jnp.sum(e, axis=0, keepdims=True)
    lb = jnp.sum(p[:layer + 1, :], axis=0, keepdims=True) - p[0:1, :]

    def body(ci, carry):
        r = pl.ds(pl.multiple_of(ci * HG_CHUNK, HG_CHUNK), HG_CHUNK)
        o, state = _hgrn_chunk(hq_ref[r, :], hf_ref[r, :], hi_ref[r, :], lb, state_ref[...])
        state_ref[...] = state
        o = (o * _rms_scale(o)) * gn_ref[...]
        g = hg_ref[r, :]
        o_ref[r, :] = (o * (g * jax.nn.sigmoid(g))).astype(o_ref.dtype)
        return carry

    lax.fori_loop(0, hq_ref.shape[0] // HG_CHUNK, body, 0)


def _hgrn(h, lb_raw, gn, layer, batch, seq):
    rows = HG_ROWS
    nr = seq // rows
    depth = lb_raw.shape[0]

    def col_spec(col0):
        return pl.BlockSpec((rows, HG_DIM), lambda b, hh, c: (b * nr + c, col0 // HG_DIM + hh))

    return pl.pallas_call(
        functools.partial(_hgrn_kernel, layer=layer),
        grid=(batch, HG_HEADS, nr),
        in_specs=[col_spec(COL_HQ), col_spec(COL_HF), col_spec(COL_HI), col_spec(COL_HG),
                  pl.BlockSpec((depth, HG_DIM), lambda b, hh, c: (0, hh)),
                  pl.BlockSpec((1, HG_DIM), lambda b, hh, c: (0, hh))],
        out_specs=pl.BlockSpec((rows, HG_DIM), lambda b, hh, c: (b * nr + c, hh)),
        out_shape=jax.ShapeDtypeStruct((batch * seq, HG_WIDTH), BF16),
        scratch_shapes=[pltpu.VMEM((HG_DIM, HG_DIM), F32)],
        compiler_params=_cparams(("parallel", "parallel", "arbitrary")),
        name="hgrn",
    )(h, h, h, h, lb_raw, gn.reshape(1, -1))


def _ca_bias_rows(rel_bias):
    idx = jnp.arange(CA_BIAS_LEN)
    m = jnp.where(idx < CA_WIN, idx, idx - CA_BIAS_LEN)
    bucket = jnp.clip(2 * CA_TQ - m, -CA_REL_CLIP, CA_REL_CLIP) + CA_REL_CLIP
    return rel_bias[:, bucket].astype(F32)


def _ca_kernel(q_ref, k0_ref, k1_ref, k2_ref, v0_ref, v1_ref, v2_ref, brow_ref, gn_ref,
               o_ref, bias_ref):
    t = pl.program_id(1)
    tq = CA_TQ

    @pl.when((pl.program_id(0) == 0) & (t == 0))
    def _():
        qc = lax.broadcasted_iota(jnp.int32, (tq, CA_WIN), 0) // CHUNK
        kc = lax.broadcasted_iota(jnp.int32, (tq, CA_WIN), 1) // CHUNK
        band = (kc >= qc) & (kc <= qc + CA_LEFT_CHUNKS)
        for h in range(CA_HEADS):
            rows = jnp.broadcast_to(brow_ref[h:h + 1, :], (tq, CA_BIAS_LEN))
            rolled = pltpu.roll(rows, 0, 1, stride=1, stride_axis=0)
            bias_ref[h] = jnp.where(band, rolled[:, :CA_WIN], MASK_VALUE)

    first_valid = jnp.maximum(2 * tq - t * tq, 0)
    kk = lax.broadcasted_iota(jnp.int32, (tq, CA_WIN), 1)
    seq_ok = kk >= first_valid
    scale = CA_DIM ** -0.5
    outs = []
    for h in range(CA_HEADS):
        c = slice(h * CA_DIM, (h + 1) * CA_DIM)
        qh = q_ref[:, c].astype(BF16)
        kh = jnp.concatenate([k0_ref[:, c], k1_ref[:, c], k2_ref[:, c]], axis=0).astype(BF16)
        vh = jnp.concatenate([v0_ref[:, c], v1_ref[:, c], v2_ref[:, c]], axis=0).astype(BF16)
        s = lax.dot_general(qh, kh, (((1,), (1,)), ((), ())), preferred_element_type=F32) * scale
        s = jnp.where(seq_ok, s + bias_ref[h], MASK_VALUE)
        p = jnp.exp(s - jnp.max(s, axis=-1, keepdims=True))
        l = jnp.sum(p, axis=-1, keepdims=True)
        outs.append(jnp.dot(p.astype(BF16), vh, preferred_element_type=F32) / l)
    o = jnp.concatenate(outs, axis=1)
    o_ref[...] = ((o * _rms_scale(o)) * gn_ref[...]).astype(o_ref.dtype)


def _chunk_attn(h, rel_bias, gn, batch, seq):
    tq = CA_TQ
    nq = seq // tq
    cq, ck, cv = COL_AQ // CA_WIDTH, COL_AK // CA_WIDTH, COL_AV // CA_WIDTH

    def kv_spec(col, back):
        return pl.BlockSpec((tq, CA_WIDTH), lambda b, t: (b * nq + jnp.maximum(t - back, 0), col))

    return pl.pallas_call(
        _ca_kernel,
        grid=(batch, nq),
        in_specs=[pl.BlockSpec((tq, CA_WIDTH), lambda b, t: (b * nq + t, cq)),
                  kv_spec(ck, 2), kv_spec(ck, 1), kv_spec(ck, 0),
                  kv_spec(cv, 2), kv_spec(cv, 1), kv_spec(cv, 0),
                  pl.BlockSpec((CA_HEADS, CA_BIAS_LEN), lambda b, t: (0, 0)),
                  pl.BlockSpec((1, CA_WIDTH), lambda b, t: (0, 0))],
        out_specs=pl.BlockSpec((tq, CA_WIDTH), lambda b, t: (b * nq + t, 0)),
        out_shape=jax.ShapeDtypeStruct((batch * seq, CA_WIDTH), BF16),
        scratch_shapes=[pltpu.VMEM((CA_HEADS, tq, CA_WIN), F32)],
        compiler_params=_cparams(("arbitrary", "arbitrary")),
        name="chunk_attn",
    )(h, h, h, h, h, h, h, _ca_bias_rows(rel_bias), gn.reshape(1, -1))


def _residual_norm(o_ref, x_ref, g_ref):
    rows = o_ref.shape[0]

    def body(c, carry):
        r = pl.ds(pl.multiple_of(c * NORM_ROWS, NORM_ROWS), NORM_ROWS)
        y = o_ref[r, :]
        o_ref[r, :] = x_ref[r, :] + (y * _rms_scale(y)) * g_ref[...]
        return carry

    lax.fori_loop(0, rows // NORM_ROWS, body, 0)


def _accumulate_then_residual_norm(a_ref, w_ref, x_ref, g_ref, o_ref, ragged):
    k = pl.program_id(1)
    last = pl.num_programs(1) - 1

    @pl.when(k == 0)
    def _():
        o_ref[...] = jnp.zeros(o_ref.shape, F32)

    if ragged is None:
        o_ref[...] += jnp.dot(a_ref[...], w_ref[...], preferred_element_type=F32)
    else:
        @pl.when(k < last)
        def _():
            o_ref[...] += jnp.dot(a_ref[...], w_ref[...], preferred_element_type=F32)

        @pl.when(k == last)
        def _():
            a, w = a_ref[...], w_ref[...]
            a = jnp.where(lax.broadcasted_iota(jnp.int32, a.shape, 1) < ragged, a, jnp.zeros_like(a))
            w = jnp.where(lax.broadcasted_iota(jnp.int32, w.shape, 0) < ragged, w, jnp.zeros_like(w))
            o_ref[...] += jnp.dot(a, w, preferred_element_type=F32)

    @pl.when(k == last)
    def _():
        _residual_norm(o_ref, x_ref, g_ref)


OUT_TK = 512


def _out_proj_kernel(mla_ref, hg_ref, ca_ref, w_ref, x_ref, gm_ref, gp_ref, o_ref, a_ref):
    @pl.when(pl.program_id(1) == 0)
    def _():
        rows = mla_ref.shape[0]
        n_mla, n_hg, n_ca = MLA_WIDTH // OUT_TK, HG_WIDTH // OUT_TK, CA_WIDTH // OUT_TK

        def body(c, carry):
            r = pl.ds(pl.multiple_of(c * NORM_ROWS, NORM_ROWS), NORM_ROWS)
            m = mla_ref[r, :]
            mn = ((m * _rms_scale(m)) * gm_ref[...]).astype(BF16)
            for p in range(n_mla):
                a_ref[p, r, :] = mn[:, p * OUT_TK:(p + 1) * OUT_TK]
            return carry

        lax.fori_loop(0, rows // NORM_ROWS, body, 0)
        for p in range(n_hg):
            a_ref[n_mla + p] = hg_ref[:, p * OUT_TK:(p + 1) * OUT_TK]
        for p in range(n_ca):
            a_ref[n_mla + n_hg + p] = ca_ref[:, p * OUT_TK:(p + 1) * OUT_TK]

    _accumulate_then_residual_norm(a_ref.at[pl.program_id(1)], w_ref, x_ref, gp_ref, o_ref, None)


def _out_proj(o_mla, o_hg, o_ca, x, g_mla, g_post, w, tm):
    t, d = x.shape
    kdim = w.shape[0]
    assert MLA_WIDTH % OUT_TK == 0 and HG_WIDTH % OUT_TK == 0 and CA_WIDTH % OUT_TK == 0
    assert kdim == MLA_WIDTH + HG_WIDTH + CA_WIDTH
    once = pl.Buffered(1)
    return pl.pallas_call(
        _out_proj_kernel,
        grid=(t // tm, kdim // OUT_TK),
        in_specs=[pl.BlockSpec((tm, MLA_WIDTH), lambda i, k: (i, 0), pipeline_mode=once),
                  pl.BlockSpec((tm, HG_WIDTH), lambda i, k: (i, 0), pipeline_mode=once),
                  pl.BlockSpec((tm, CA_WIDTH), lambda i, k: (i, 0), pipeline_mode=once),
                  pl.BlockSpec((OUT_TK, d), lambda i, k: (k, 0)),
                  pl.BlockSpec((tm, d), lambda i, k: (i, 0), pipeline_mode=once),
                  pl.BlockSpec((1, MLA_WIDTH), lambda i, k: (0, 0)),
                  pl.BlockSpec((1, d), lambda i, k: (0, 0))],
        out_specs=pl.BlockSpec((tm, d), lambda i, k: (i, 0)),
        out_shape=jax.ShapeDtypeStruct((t, d), F32),
        scratch_shapes=[pltpu.VMEM((kdim // OUT_TK, tm, OUT_TK), BF16)],
        compiler_params=_cparams(("parallel", "arbitrary")),
        name="out_proj",
    )(o_mla, o_hg, o_ca, w, x, g_mla.reshape(1, -1), g_post.reshape(1, -1))


def _down_proj_kernel(a_ref, w_ref, x_ref, g_ref, o_ref, *, ragged):
    _accumulate_then_residual_norm(a_ref, w_ref, x_ref, g_ref, o_ref, ragged)


def _down_proj(a, w, x, g, tm, tk):
    t, d = x.shape
    kdim = a.shape[1]
    ragged = kdim % tk or None
    return pl.pallas_call(
        functools.partial(_down_proj_kernel, ragged=ragged),
        grid=(t // tm, pl.cdiv(kdim, tk)),
        in_specs=[pl.BlockSpec((tm, tk), lambda i, k: (i, k)),
                  pl.BlockSpec((tk, d), lambda i, k: (k, 0)),
                  pl.BlockSpec((tm, d), lambda i, k: (i, 0), pipeline_mode=pl.Buffered(1)),
                  pl.BlockSpec((1, d), lambda i, k: (0, 0))],
        out_specs=pl.BlockSpec((tm, d), lambda i, k: (i, 0)),
        out_shape=jax.ShapeDtypeStruct((t, d), F32),
        compiler_params=_cparams(("parallel", "arbitrary")),
        name="down_proj",
    )(a, w, x, g.reshape(1, -1))


def _prep_w_in(w):
    d = w.shape[0]
    sizes = (MLA_Q_RANK, MLA_KV_RANK, MLA_ROPE, HG_WIDTH, HG_WIDTH, HG_WIDTH, HG_WIDTH,
             CA_WIDTH, CA_WIDTH, CA_WIDTH)
    offs = [0]
    for s in sizes:
        offs.append(offs[-1] + s)
    cq, ckv, kr, hq, hf, hi, hg, aq, ak, av = (w[:, offs[n]:offs[n + 1]].astype(BF16)
                                               for n in range(len(sizes)))
    z = jnp.zeros((d, D_IN_PAD - COL_KR - MLA_ROPE), BF16)
    out = jnp.concatenate([aq, ak, av, hq, hf, hi, hg, ckv, cq, kr, z], axis=1)
    assert out.shape[1] == D_IN_PAD
    return out


def _prep_w_uq(w):
    r = w.shape[0]
    w3 = w.reshape(r, MLA_HEADS, MLA_NOPE + MLA_ROPE)
    pad = jnp.zeros((r, MLA_HEADS, MLA_QK_PAD - MLA_NOPE - MLA_ROPE), w.dtype)
    return jnp.concatenate([w3, pad], axis=2).reshape(r, MLA_HEADS * MLA_QK_PAD).astype(BF16)


def _prep_w_ukv(w):
    r = w.shape[0]
    w3 = w.reshape(r, MLA_HEADS, MLA_NOPE + MLA_V)
    wk = w3[:, :, :MLA_NOPE].reshape(r, MLA_HEADS * MLA_NOPE)
    wv = w3[:, :, MLA_NOPE:].reshape(r, MLA_HEADS * MLA_V)
    return wk.astype(BF16), wv.T.astype(BF16)


def kernel(x, positions, attn_pre_norm, attn_post_norm, w_in, mla_q_norm, mla_kv_norm, w_uq, w_ukv,
           mla_out_norm, hg_lower_bounds, hg_out_norm, ca_rel_bias, ca_out_norm, w_out, ffn_pre_norm,
           ffn_post_norm, w_gate, w_up, w_down):
    batch, seq, d = x.shape
    t = batch * seq
    depth = w_in.shape[0]
    xf = x.reshape(t, d)
    pos = positions.reshape(t, 1)
    for l in range(depth):
        h = _norm_matmul(xf, attn_pre_norm[l], [_prep_w_in(w_in[l])], F32, tm=1024, tn=512)
        wk, wvt = _prep_w_ukv(w_ukv[l])
        q, k, vt = _mla_proj(h, pos, mla_q_norm[l], mla_kv_norm[l], _prep_w_uq(w_uq[l]), wk, wvt,
                             tm=MLA_TQ)
        o_mla = _mla_attn(q, k, vt, batch, seq)
        o_hg = _hgrn(h, hg_lower_bounds, hg_out_norm[l], l, batch, seq)
        o_ca = _chunk_attn(h, ca_rel_bias[l], ca_out_norm[l], batch, seq)
        xf = _out_proj(o_mla, o_hg, o_ca, xf, mla_out_norm[l], attn_post_norm[l],
                       w_out[l].astype(BF16), tm=512)
        hid = _norm_matmul(xf, ffn_pre_norm[l], [w_gate[l].astype(BF16), w_up[l].astype(BF16)],
                           BF16, tm=1024, tn=512)
        xf = _down_proj(hid, w_down[l].astype(BF16), xf, ffn_post_norm[l], tm=512, tk=1024)
    return xf.reshape(batch, seq, d)
```

```python
import functools
import math

import jax
import jax.numpy as jnp
from jax import lax
from jax.experimental import pallas as pl
from jax.experimental.pallas import tpu as pltpu

F32 = jnp.float32
BF16 = jnp.bfloat16

EPS = 1e-6
MASK_VALUE = -1e30
TINY = 1e-30
CHUNK = 64

MLA_HEADS = 16
MLA_Q_RANK = 768
MLA_KV_RANK = 512
MLA_NOPE = 128
MLA_ROPE = 64
MLA_V = 128
ROPE_THETA = 10000.0
MLA_QK_PAD = 256

HG_HEADS = 8
HG_DIM = 128
HG_BLOCK = 16
HG_CHUNK = 128
HG_ROWS = 512

CA_HEADS = 8
CA_DIM = 128
CA_LEFT_CHUNKS = 8
CA_REL_CLIP = 256
CA_TQ = 256
CA_WIN = 3 * CA_TQ
CA_BIAS_LEN = 1024

MLA_WIDTH = MLA_HEADS * MLA_V
HG_WIDTH = HG_HEADS * HG_DIM
CA_WIDTH = CA_HEADS * CA_DIM

COL_AQ = 0
COL_AK = 1024
COL_AV = 2048
COL_HQ = 3072
COL_HF = 4096
COL_HI = 5120
COL_HG = 6144
COL_CKV = 7168
COL_CQ = 7680
COL_KR = 8448
D_IN_PAD = 8704

LANE = 128
VMEM_LIMIT = 56 * 1024 * 1024


def _cparams(sem, vmem=VMEM_LIMIT):
    return pltpu.CompilerParams(dimension_semantics=sem, vmem_limit_bytes=vmem)


def _rms_scale(x):
    return lax.rsqrt(jnp.mean(x * x, axis=-1, keepdims=True) + EPS)


NORM_ROWS = 16
NORM_UNROLL = 4


def _norm_rows_into(xn_ref, x_ref, g_ref):
    rows = x_ref.shape[0]

    def body(c, carry):
        r = pl.ds(pl.multiple_of(c * NORM_ROWS, NORM_ROWS), NORM_ROWS)
        x = x_ref[r, :]
        xn_ref[r, :] = ((x * _rms_scale(x)) * g_ref[...]).astype(BF16)
        return carry

    lax.fori_loop(0, rows // NORM_ROWS, body, 0, unroll=NORM_UNROLL)


def _norm_matmul_kernel(x_ref, g_ref, w_ref, o_ref, xn_ref):
    @pl.when(pl.program_id(1) == 0)
    def _():
        _norm_rows_into(xn_ref, x_ref, g_ref)

    o_ref[...] = jnp.dot(xn_ref[...], w_ref[...], preferred_element_type=F32).astype(o_ref.dtype)


def _norm_swiglu_kernel(x_ref, g_ref, wg_ref, wu_ref, o_ref, xn_ref):
    @pl.when(pl.program_id(1) == 0)
    def _():
        _norm_rows_into(xn_ref, x_ref, g_ref)

    xn = xn_ref[...]
    gate = jnp.dot(xn, wg_ref[...], preferred_element_type=F32)
    up = jnp.dot(xn, wu_ref[...], preferred_element_type=F32)
    o_ref[...] = ((gate * jax.nn.sigmoid(gate)) * up).astype(o_ref.dtype)


def _norm_matmul(x, g, ws, layer, out_dtype, tm, tn):
    t, d = x.shape
    n = ws[0].shape[2]
    kern = _norm_matmul_kernel if len(ws) == 1 else _norm_swiglu_kernel
    w_specs = [pl.BlockSpec((None, d, tn), lambda i, j: (layer, 0, j)) for _ in ws]
    return pl.pallas_call(
        kern,
        grid=(t // tm, pl.cdiv(n, tn)),
        in_specs=[pl.BlockSpec((tm, d), lambda i, j: (i, 0), pipeline_mode=pl.Buffered(1)),
                  pl.BlockSpec((1, d), lambda i, j: (0, 0))] + w_specs,
        out_specs=pl.BlockSpec((tm, tn), lambda i, j: (i, j)),
        out_shape=jax.ShapeDtypeStruct((t, n), out_dtype),
        scratch_shapes=[pltpu.VMEM((tm, d), BF16)],
        compiler_params=_cparams(("parallel", "arbitrary")),
        name="norm_matmul" if len(ws) == 1 else "norm_swiglu",
    )(x, g.reshape(1, d), *ws)


def _rope_tables(pos_ref):
    lane = lax.broadcasted_iota(jnp.int32, (1, LANE), 1)
    half = MLA_ROPE // 2
    idx = (lane % half).astype(F32)
    inv_freq = jnp.exp((-math.log(ROPE_THETA) * 2.0) * idx / MLA_ROPE)
    inv_freq = jnp.where(lane < MLA_ROPE, inv_freq, 0.0)
    ang = pos_ref[...].astype(F32) * inv_freq
    cos, sin = jnp.cos(ang), jnp.sin(ang)
    sin_hi = jnp.where((lane >= half) & (lane < MLA_ROPE), sin, 0.0)
    sin_lo = jnp.where(lane < half, -sin, 0.0)
    return cos, sin_hi, sin_lo


def _rope(x, tables):
    cos, sin_hi, sin_lo = tables
    half = MLA_ROPE // 2
    return x * cos + pltpu.roll(x, half, 1) * sin_hi + pltpu.roll(x, LANE - half, 1) * sin_lo


def _mla_proj_kernel(cq_ref, ckv_ref, kr_ref, pos_ref, gq_ref, gkv_ref, wq_ref, wk_ref, wvt_ref,
                     q_ref, k_ref, vt_ref):
    tables = _rope_tables(pos_ref)
    cq = cq_ref[...]
    cqn = ((cq * _rms_scale(cq)) * gq_ref[...]).astype(BF16)
    ckv = ckv_ref[...]
    ckvn = ((ckv * _rms_scale(ckv)) * gkv_ref[...]).astype(BF16)
    k_pe = _rope(kr_ref[...], tables).astype(BF16)
    for h in range(MLA_HEADS):
        c0 = h * MLA_QK_PAD
        qh = jnp.dot(cqn, wq_ref[:, c0:c0 + MLA_QK_PAD], preferred_element_type=F32)
        q_ref[:, c0:c0 + LANE] = qh[:, :LANE].astype(BF16)
        q_ref[:, c0 + LANE:c0 + MLA_QK_PAD] = _rope(qh[:, LANE:], tables).astype(BF16)
        kh = jnp.dot(ckvn, wk_ref[:, h * LANE:(h + 1) * LANE], preferred_element_type=F32)
        k_ref[:, c0:c0 + LANE] = kh.astype(BF16)
        k_ref[:, c0 + LANE:c0 + MLA_QK_PAD] = k_pe
    rows = 4 * LANE
    for c0 in range(0, MLA_WIDTH, rows):
        vt = lax.dot_general(wvt_ref[c0:c0 + rows, :], ckvn, (((1,), (1,)), ((), ())),
                             preferred_element_type=F32)
        vt_ref[0, c0:c0 + rows, :] = vt.astype(BF16)


def _mla_proj(h, positions, gq, gkv, wq, wk, wvt, layer, tm):
    t = h.shape[0]
    qk_w = MLA_HEADS * MLA_QK_PAD
    const = lambda i: (0, 0)
    return pl.pallas_call(
        _mla_proj_kernel,
        grid=(t // tm,),
        in_specs=[pl.BlockSpec((tm, MLA_Q_RANK), lambda i: (i, COL_CQ // MLA_Q_RANK)),
                  pl.BlockSpec((tm, MLA_KV_RANK), lambda i: (i, COL_CKV // MLA_KV_RANK)),
                  pl.BlockSpec((tm, LANE), lambda i: (i, COL_KR // LANE)),
                  pl.BlockSpec((tm, 1), lambda i: (i, 0)),
                  pl.BlockSpec((1, MLA_Q_RANK), const),
                  pl.BlockSpec((1, MLA_KV_RANK), const),
                  pl.BlockSpec((None,) + wq.shape[1:], lambda i: (layer, 0, 0)),
                  pl.BlockSpec((None,) + wk.shape[1:], lambda i: (layer, 0, 0)),
                  pl.BlockSpec((None,) + wvt.shape[1:], lambda i: (layer, 0, 0))],
        out_specs=[pl.BlockSpec((tm, qk_w), lambda i: (i, 0)),
                   pl.BlockSpec((tm, qk_w), lambda i: (i, 0)),
                   pl.BlockSpec((1, MLA_WIDTH, tm), lambda i: (i, 0, 0))],
        out_shape=[jax.ShapeDtypeStruct((t, qk_w), BF16),
                   jax.ShapeDtypeStruct((t, qk_w), BF16),
                   jax.ShapeDtypeStruct((t // tm, MLA_WIDTH, tm), BF16)],
        compiler_params=_cparams(("parallel",)),
        name="mla_proj",
    )(h, h, h, positions, gq.reshape(1, -1), gkv.reshape(1, -1), wq, wk, wvt)


MLA_TQ = 512


MLA_TK = MLA_TQ // 2


def _mla_attn_kernel(q_ref, k_ref, vt_ref, o_ref, qt_ref, sa_ref, sb_ref, xa_ref, xb_ref, pa_ref, pb_ref,
                     m_ref, l_ref, alpha_ref, acc_ref):
    i = pl.program_id(2)
    tq, tk = MLA_TQ, MLA_TK
    qt_ref[...] = q_ref[...].astype(F32).T.astype(BF16)
    c = (MLA_NOPE + MLA_ROPE) ** -0.5 * math.log2(math.e)
    m_ref[...] = jnp.full(m_ref.shape, -jnp.inf, F32)
    l_ref[...] = jnp.zeros(l_ref.shape, F32)
    alpha_ref[...] = jnp.ones(alpha_ref.shape, F32)
    acc_ref[...] = jnp.zeros(acc_ref.shape, F32)
    pb_ref[...] = jnp.zeros(pb_ref.shape, BF16)

    def scores_into(s_ref, x_ref, t):
        r = pl.ds(pl.multiple_of(t * tk, tk), tk)
        s = jnp.dot(k_ref[r, :], qt_ref[...], preferred_element_type=F32)
        s_ref[...] = s
        x_ref[...] = jnp.max(s, axis=0, keepdims=True)

    def add_values(p_ref, vt_tile):
        acc_ref[...] = alpha_ref[...] * acc_ref[...] + jnp.dot(vt_tile, p_ref[...],
                                                               preferred_element_type=F32)

    def softmax_terms(s_ref, x_ref, p_ref, chunk_shift):
        s = s_ref[...]
        if chunk_shift is None:
            tile_max = x_ref[...]
        else:
            kc = lax.broadcasted_iota(jnp.int32, s.shape, 0) // CHUNK + chunk_shift
            qc = lax.broadcasted_iota(jnp.int32, s.shape, 1) // CHUNK
            s = jnp.where(kc <= qc, s, MASK_VALUE)
            tile_max = jnp.max(s, axis=0, keepdims=True)
        m_old = m_ref[...]
        m_new = jnp.maximum(m_old, tile_max)
        alpha = jnp.exp2((m_old - m_new) * c)
        p = jnp.exp2((s - m_new) * c)
        l_ref[...] = alpha * l_ref[...] + jnp.sum(p, axis=0, keepdims=True)
        m_ref[...] = m_new
        alpha_ref[...] = alpha
        p_ref[...] = p.astype(BF16)

    def pair(u, diagonal):
        scores_into(sb_ref, xb_ref, 2 * u + 1)
        add_values(pb_ref, vt_ref[jnp.maximum(u - 1, 0)][:, tk:])
        softmax_terms(sa_ref, xa_ref, pa_ref, 0 if diagonal else None)
        if not diagonal:
            scores_into(sa_ref, xa_ref, 2 * u + 2)
        add_values(pa_ref, vt_ref[u][:, :tk])
        softmax_terms(sb_ref, xb_ref, pb_ref, tk // CHUNK if diagonal else None)

    scores_into(sa_ref, xa_ref, 0)

    def body(u, carry):
        pair(u, False)
        return carry

    lax.fori_loop(0, i, body, 0)
    pair(i, True)
    add_values(pb_ref, vt_ref[i][:, tk:])
    o_ref[...] = (acc_ref[...] / l_ref[...]).T


def _mla_attn(q, k, vt, batch, seq):
    tq = MLA_TQ
    nq = seq // tq
    assert vt.shape[2] == tq
    return pl.pallas_call(
        _mla_attn_kernel,
        grid=(batch, MLA_HEADS, nq),
        in_specs=[pl.BlockSpec((tq, MLA_QK_PAD), lambda b, h, i: (b * nq + i, h)),
                  pl.BlockSpec((seq, MLA_QK_PAD), lambda b, h, i: (b, h)),
                  pl.BlockSpec((nq, MLA_V, tq), lambda b, h, i: (b, h, 0))],
        out_specs=pl.BlockSpec((tq, MLA_V), lambda b, h, i: (b * nq + i, h)),
        out_shape=jax.ShapeDtypeStruct((batch * seq, MLA_WIDTH), F32),
        scratch_shapes=[pltpu.VMEM((MLA_QK_PAD, tq), BF16),
                        pltpu.VMEM((MLA_TK, tq), F32), pltpu.VMEM((MLA_TK, tq), F32),
                        pltpu.VMEM((1, tq), F32), pltpu.VMEM((1, tq), F32),
                        pltpu.VMEM((MLA_TK, tq), BF16), pltpu.VMEM((MLA_TK, tq), BF16),
                        pltpu.VMEM((1, tq), F32), pltpu.VMEM((1, tq), F32), pltpu.VMEM((1, tq), F32),
                        pltpu.VMEM((MLA_V, tq), F32)],
        compiler_params=_cparams(("parallel", "parallel", "arbitrary")),
        name="mla_attn",
    )(q, k, vt)


def _block_cumsum(x, row):
    r = row % HG_BLOCK
    s = 1
    while s < HG_BLOCK:
        x = x + jnp.where(r >= s, pltpu.roll(x, s, 0), 0.0)
        s *= 2
    return x


def _hgrn_chunk(hq, hf, hi, lb, state):
    c = HG_CHUNK
    nb = c // HG_BLOCK
    row = lax.broadcasted_iota(jnp.int32, (c, HG_DIM), 0)
    col = lax.broadcasted_iota(jnp.int32, (c, HG_DIM), 1)

    q = hq * jax.nn.sigmoid(hq)
    f = lb + (1.0 - lb) * jax.nn.sigmoid(hf)
    k = (1.0 - lb) * jax.nn.sigmoid(-hf)
    b = _block_cumsum(jnp.log(jnp.maximum(f, TINY)), row)
    b3 = b.reshape(nb, HG_BLOCK, HG_DIM)
    b_last3 = jnp.broadcast_to(b3[:, HG_BLOCK - 1:HG_BLOCK, :], b3.shape)
    b_last = b_last3.reshape(c, HG_DIM)

    q3 = q.reshape(nb, HG_BLOCK, HG_DIM)
    k3 = k.reshape(nb, HG_BLOCK, HG_DIM)
    v3 = hi.reshape(nb, HG_BLOCK, HG_DIM)
    irow = lax.broadcasted_iota(jnp.int32, b3.shape, 1)
    sub = 8
    o_parts = [jnp.zeros((nb, sub, HG_DIM), F32) for _ in range(HG_BLOCK // sub)]
    for j in range(HG_BLOCK):
        lo = (j // sub) * sub
        diff = jnp.minimum(b3[:, lo:, :] - b3[:, j:j + 1, :], 0.0)
        w = jnp.where(irow[:, lo:, :] >= j, q3[:, lo:, :] * k3[:, j:j + 1, :] * jnp.exp(diff), 0.0)
        wv = jnp.sum(w, axis=-1, keepdims=True) * v3[:, j:j + 1, :]
        for part in range(lo // sub, HG_BLOCK // sub):
            r0 = part * sub - lo
            o_parts[part] = o_parts[part] + wv[:, r0:r0 + sub, :]
    o = jnp.concatenate(o_parts, axis=1).reshape(c, HG_DIM)

    q_dec = q * jnp.exp(b)
    k_dec_t = (k * jnp.exp(b_last - b)).T
    b_last_t = b_last.T
    v_bf = hi.astype(BF16)
    blk_of_col = col // HG_BLOCK
    k_stack = jnp.concatenate(
        [jnp.where(blk_of_col == j, k_dec_t, 0.0) for j in range(nb)], axis=0).astype(BF16)
    u_all = jnp.dot(k_stack, v_bf, preferred_element_type=F32)
    states = []
    for j in range(nb):
        states.append(state.astype(BF16))
        decay = jnp.exp(b_last_t[:, j * HG_BLOCK:j * HG_BLOCK + 1])
        state = decay * state + u_all[j * HG_DIM:(j + 1) * HG_DIM, :]
    s_stack = jnp.concatenate(states, axis=0)
    blk_of_row = row // HG_BLOCK
    q_exp = jnp.concatenate(
        [jnp.where(blk_of_row == j, q_dec, 0.0) for j in range(nb)], axis=1).astype(BF16)
    o = o + jnp.dot(q_exp, s_stack, preferred_element_type=F32)
    return o, state


def _hgrn_kernel(hq_ref, hf_ref, hi_ref, hg_ref, lbraw_ref, gn_ref, o_ref, state_ref, *, layer):
    @pl.when(pl.program_id(2) == 0)
    def _():
        state_ref[...] = jnp.zeros(state_ref.shape, F32)

    raw = lbraw_ref[...]
    e = jnp.exp(raw - jnp.max(raw, axis=0, keepdims=True))
    p = e / jnp.sum(e, axis=0, keepdims=True)
    lb = jnp.sum(p[:layer + 1, :], axis=0, keepdims=True) - p[0:1, :]

    def body(ci, carry):
        r = pl.ds(pl.multiple_of(ci * HG_CHUNK, HG_CHUNK), HG_CHUNK)
        o, state = _hgrn_chunk(hq_ref[r, :], hf_ref[r, :], hi_ref[r, :], lb, state_ref[...])
        state_ref[...] = state
        o = (o * _rms_scale(o)) * gn_ref[...]
        g = hg_ref[r, :]
        o_ref[r, :] = (o * (g * jax.nn.sigmoid(g))).astype(o_ref.dtype)
        return carry

    lax.fori_loop(0, hq_ref.shape[0] // HG_CHUNK, body, 0)


def _hgrn(h, lb_raw, gn, layer, batch, seq):
    rows = HG_ROWS
    nr = seq // rows
    depth = lb_raw.shape[0]

    def col_spec(col0):
        return pl.BlockSpec((rows, HG_DIM), lambda b, hh, c: (b * nr + c, col0 // HG_DIM + hh))

    return pl.pallas_call(
        functools.partial(_hgrn_kernel, layer=layer),
        grid=(batch, HG_HEADS, nr),
        in_specs=[col_spec(COL_HQ), col_spec(COL_HF), col_spec(COL_HI), col_spec(COL_HG),
                  pl.BlockSpec((depth, HG_DIM), lambda b, hh, c: (0, hh)),
                  pl.BlockSpec((1, HG_DIM), lambda b, hh, c: (0, hh))],
        out_specs=pl.BlockSpec((rows, HG_DIM), lambda b, hh, c: (b * nr + c, hh)),
        out_shape=jax.ShapeDtypeStruct((batch * seq, HG_WIDTH), BF16),
        scratch_shapes=[pltpu.VMEM((HG_DIM, HG_DIM), F32)],
        compiler_params=_cparams(("parallel", "parallel", "arbitrary")),
        name="hgrn",
    )(h, h, h, h, lb_raw, gn.reshape(1, -1))


def _ca_bias_rows(rel_bias):
    idx = jnp.arange(CA_BIAS_LEN)
    m = jnp.where(idx < CA_WIN, idx, idx - CA_BIAS_LEN)
    bucket = jnp.clip(2 * CA_TQ - m, -CA_REL_CLIP, CA_REL_CLIP) + CA_REL_CLIP
    return rel_bias[:, bucket].astype(F32)


def _ca_kernel(q_ref, k0_ref, k1_ref, k2_ref, v0_ref, v1_ref, v2_ref, brow_ref, gn_ref,
               o_ref, bias_ref):
    t = pl.program_id(1)
    tq = CA_TQ

    @pl.when((pl.program_id(0) == 0) & (t == 0))
    def _():
        qc = lax.broadcasted_iota(jnp.int32, (tq, CA_WIN), 0) // CHUNK
        kc = lax.broadcasted_iota(jnp.int32, (tq, CA_WIN), 1) // CHUNK
        band = (kc >= qc) & (kc <= qc + CA_LEFT_CHUNKS)
        for h in range(CA_HEADS):
            rows = jnp.broadcast_to(brow_ref[h:h + 1, :], (tq, CA_BIAS_LEN))
            rolled = pltpu.roll(rows, 0, 1, stride=1, stride_axis=0)
            bias_ref[h] = jnp.where(band, rolled[:, :CA_WIN], MASK_VALUE)

    first_valid = jnp.maximum(2 * tq - t * tq, 0)
    kk = lax.broadcasted_iota(jnp.int32, (tq, CA_WIN), 1)
    seq_ok = kk >= first_valid
    scale = CA_DIM ** -0.5
    outs = []
    for h in range(CA_HEADS):
        c = slice(h * CA_DIM, (h + 1) * CA_DIM)
        qh = q_ref[:, c].astype(BF16)
        kh = jnp.concatenate([k0_ref[:, c], k1_ref[:, c], k2_ref[:, c]], axis=0).astype(BF16)
        vh = jnp.concatenate([v0_ref[:, c], v1_ref[:, c], v2_ref[:, c]], axis=0).astype(BF16)
        s = lax.dot_general(qh, kh, (((1,), (1,)), ((), ())), preferred_element_type=F32) * scale
        s = jnp.where(seq_ok, s + bias_ref[h], MASK_VALUE)
        p = jnp.exp(s - jnp.max(s, axis=-1, keepdims=True))
        l = jnp.sum(p, axis=-1, keepdims=True)
        outs.append(jnp.dot(p.astype(BF16), vh, preferred_element_type=F32) / l)
    o = jnp.concatenate(outs, axis=1)
    o_ref[...] = ((o * _rms_scale(o)) * gn_ref[...]).astype(o_ref.dtype)


def _chunk_attn(h, rel_bias, gn, batch, seq):
    tq = CA_TQ
    nq = seq // tq
    cq, ck, cv = COL_AQ // CA_WIDTH, COL_AK // CA_WIDTH, COL_AV // CA_WIDTH

    def kv_spec(col, back):
        return pl.BlockSpec((tq, CA_WIDTH), lambda b, t: (b * nq + jnp.maximum(t - back, 0), col))

    return pl.pallas_call(
        _ca_kernel,
        grid=(batch, nq),
        in_specs=[pl.BlockSpec((tq, CA_WIDTH), lambda b, t: (b * nq + t, cq)),
                  kv_spec(ck, 2), kv_spec(ck, 1), kv_spec(ck, 0),
                  kv_spec(cv, 2), kv_spec(cv, 1), kv_spec(cv, 0),
                  pl.BlockSpec((CA_HEADS, CA_BIAS_LEN), lambda b, t: (0, 0)),
                  pl.BlockSpec((1, CA_WIDTH), lambda b, t: (0, 0))],
        out_specs=pl.BlockSpec((tq, CA_WIDTH), lambda b, t: (b * nq + t, 0)),
        out_shape=jax.ShapeDtypeStruct((batch * seq, CA_WIDTH), BF16),
        scratch_shapes=[pltpu.VMEM((CA_HEADS, tq, CA_WIN), F32)],
        compiler_params=_cparams(("arbitrary", "arbitrary")),
        name="chunk_attn",
    )(h, h, h, h, h, h, h, _ca_bias_rows(rel_bias), gn.reshape(1, -1))


X_SLICES = 8


def _residual_norm(o_ref, xs_ref, g_ref):
    rows, d = o_ref.shape
    w = d // X_SLICES

    step = NORM_ROWS * NORM_UNROLL

    def body(c, carry):
        base = pl.multiple_of(c * step, step)
        chunks = [pl.ds(base + u * NORM_ROWS, NORM_ROWS) for u in range(NORM_UNROLL)]
        ys = [o_ref[r, :] for r in chunks]
        yns = [(y * _rms_scale(y)) * g_ref[...] for y in ys]
        for r, yn in zip(chunks, yns):
            for p in range(X_SLICES):
                o_ref[r, p * w:(p + 1) * w] = xs_ref[p, r, :] + yn[:, p * w:(p + 1) * w]
        return carry

    lax.fori_loop(0, rows // step, body, 0)


def _x_slice_spec(tm, d):
    return pl.BlockSpec((tm, d // X_SLICES), lambda i, k: (i, jnp.minimum(k, X_SLICES - 1)))


def _accumulate_then_residual_norm(a_ref, w_ref, x_ref, g_ref, o_ref, xs_ref, ragged):
    k = pl.program_id(1)
    last = pl.num_programs(1) - 1

    @pl.when(k == 0)
    def _():
        o_ref[...] = jnp.zeros(o_ref.shape, F32)

    @pl.when(k < X_SLICES)
    def _():
        xs_ref[k] = x_ref[...]

    if ragged is None:
        o_ref[...] += jnp.dot(a_ref[...], w_ref[...], preferred_element_type=F32)
    else:
        @pl.when(k < last)
        def _():
            o_ref[...] += jnp.dot(a_ref[...], w_ref[...], preferred_element_type=F32)

        @pl.when(k == last)
        def _():
            a, w = a_ref[...], w_ref[...]
            a = jnp.where(lax.broadcasted_iota(jnp.int32, a.shape, 1) < ragged, a, jnp.zeros_like(a))
            w = jnp.where(lax.broadcasted_iota(jnp.int32, w.shape, 0) < ragged, w, jnp.zeros_like(w))
            o_ref[...] += jnp.dot(a, w, preferred_element_type=F32)

    @pl.when(k == last)
    def _():
        _residual_norm(o_ref, xs_ref, g_ref)


OUT_TK = 512


def _out_proj_kernel(mla_ref, hg_ref, ca_ref, w_ref, x_ref, gm_ref, gp_ref, o_ref, a_ref, xs_ref):
    @pl.when(pl.program_id(1) == 0)
    def _():
        rows = mla_ref.shape[0]
        n_mla, n_hg, n_ca = MLA_WIDTH // OUT_TK, HG_WIDTH // OUT_TK, CA_WIDTH // OUT_TK

        def body(c, carry):
            r = pl.ds(pl.multiple_of(c * NORM_ROWS, NORM_ROWS), NORM_ROWS)
            m = mla_ref[r, :]
            mn = ((m * _rms_scale(m)) * gm_ref[...]).astype(BF16)
            for p in range(n_mla):
                a_ref[p, r, :] = mn[:, p * OUT_TK:(p + 1) * OUT_TK]
            return carry

        lax.fori_loop(0, rows // NORM_ROWS, body, 0, unroll=NORM_UNROLL)
        for p in range(n_hg):
            a_ref[n_mla + p] = hg_ref[:, p * OUT_TK:(p + 1) * OUT_TK]
        for p in range(n_ca):
            a_ref[n_mla + n_hg + p] = ca_ref[:, p * OUT_TK:(p + 1) * OUT_TK]

    _accumulate_then_residual_norm(a_ref.at[pl.program_id(1)], w_ref, x_ref, gp_ref, o_ref, xs_ref, None)


def _out_proj(o_mla, o_hg, o_ca, x, g_mla, g_post, w, layer, tm):
    t, d = x.shape
    kdim = w.shape[1]
    nk = kdim // OUT_TK
    assert MLA_WIDTH % OUT_TK == 0 and HG_WIDTH % OUT_TK == 0 and CA_WIDTH % OUT_TK == 0
    assert kdim == MLA_WIDTH + HG_WIDTH + CA_WIDTH and nk >= X_SLICES
    return pl.pallas_call(
        _out_proj_kernel,
        grid=(t // tm, nk),
        in_specs=[pl.BlockSpec((tm, MLA_WIDTH), lambda i, k: (i, 0)),
                  pl.BlockSpec((tm, HG_WIDTH), lambda i, k: (i, 0)),
                  pl.BlockSpec((tm, CA_WIDTH), lambda i, k: (i, 0)),
                  pl.BlockSpec((None, OUT_TK, d), lambda i, k: (layer, k, 0)),
                  _x_slice_spec(tm, d),
                  pl.BlockSpec((1, MLA_WIDTH), lambda i, k: (0, 0)),
                  pl.BlockSpec((1, d), lambda i, k: (0, 0))],
        out_specs=pl.BlockSpec((tm, d), lambda i, k: (i, 0)),
        out_shape=jax.ShapeDtypeStruct((t, d), F32),
        scratch_shapes=[pltpu.VMEM((nk, tm, OUT_TK), BF16),
                        pltpu.VMEM((X_SLICES, tm, d // X_SLICES), F32)],
        compiler_params=_cparams(("parallel", "arbitrary")),
        name="out_proj",
    )(o_mla, o_hg, o_ca, w, x, g_mla.reshape(1, -1), g_post.reshape(1, -1))


def _down_proj_kernel(a_ref, w_ref, x_ref, g_ref, o_ref, xs_ref, *, ragged):
    _accumulate_then_residual_norm(a_ref, w_ref, x_ref, g_ref, o_ref, xs_ref, ragged)


def _down_proj(a, w, x, g, layer, tm, tk):
    t, d = x.shape
    kdim = a.shape[1]
    ragged = kdim % tk or None
    nk = pl.cdiv(kdim, tk)
    assert nk >= X_SLICES
    return pl.pallas_call(
        functools.partial(_down_proj_kernel, ragged=ragged),
        grid=(t // tm, nk),
        in_specs=[pl.BlockSpec((tm, tk), lambda i, k: (i, k)),
                  pl.BlockSpec((None, tk, d), lambda i, k: (layer, k, 0)),
                  _x_slice_spec(tm, d),
                  pl.BlockSpec((1, d), lambda i, k: (0, 0))],
        out_specs=pl.BlockSpec((tm, d), lambda i, k: (i, 0)),
        out_shape=jax.ShapeDtypeStruct((t, d), F32),
        scratch_shapes=[pltpu.VMEM((X_SLICES, tm, d // X_SLICES), F32)],
        compiler_params=_cparams(("parallel", "arbitrary")),
        name="down_proj",
    )(a, w, x, g.reshape(1, -1))


def _prep_w_in(w):
    depth, d, _ = w.shape
    sizes = (MLA_Q_RANK, MLA_KV_RANK, MLA_ROPE, HG_WIDTH, HG_WIDTH, HG_WIDTH, HG_WIDTH,
             CA_WIDTH, CA_WIDTH, CA_WIDTH)
    offs = [0]
    for s in sizes:
        offs.append(offs[-1] + s)
    cq, ckv, kr, hq, hf, hi, hg, aq, ak, av = (w[:, :, offs[n]:offs[n + 1]].astype(BF16)
                                               for n in range(len(sizes)))
    z = jnp.zeros((depth, d, D_IN_PAD - COL_KR - MLA_ROPE), BF16)
    out = jnp.concatenate([aq, ak, av, hq, hf, hi, hg, ckv, cq, kr, z], axis=2)
    assert out.shape[2] == D_IN_PAD
    return out


def _prep_w_uq(w):
    depth, r, _ = w.shape
    w4 = w.astype(BF16).reshape(depth, r, MLA_HEADS, MLA_NOPE + MLA_ROPE)
    pad = jnp.zeros((depth, r, MLA_HEADS, MLA_QK_PAD - MLA_NOPE - MLA_ROPE), BF16)
    return jnp.concatenate([w4, pad], axis=3).reshape(depth, r, MLA_HEADS * MLA_QK_PAD)


def _prep_w_ukv(w):
    depth, r, _ = w.shape
    w4 = w.astype(BF16).reshape(depth, r, MLA_HEADS, MLA_NOPE + MLA_V)
    wk = w4[:, :, :, :MLA_NOPE].reshape(depth, r, MLA_HEADS * MLA_NOPE)
    wv = w4[:, :, :, MLA_NOPE:].reshape(depth, r, MLA_HEADS * MLA_V)
    return wk, jnp.swapaxes(wv, 1, 2)


def kernel(x, positions, attn_pre_norm, attn_post_norm, w_in, mla_q_norm, mla_kv_norm, w_uq, w_ukv,
           mla_out_norm, hg_lower_bounds, hg_out_norm, ca_rel_bias, ca_out_norm, w_out, ffn_pre_norm,
           ffn_post_norm, w_gate, w_up, w_down):
    batch, seq, d = x.shape
    t = batch * seq
    depth = w_in.shape[0]
    xf = x.reshape(t, d)
    pos = positions.reshape(t, 1)
    w_in_b = _prep_w_in(w_in)
    w_uq_b = _prep_w_uq(w_uq)
    w_uk_b, w_uvt_b = _prep_w_ukv(w_ukv)
    w_out_b, w_gate_b, w_up_b, w_down_b = (w.astype(BF16) for w in (w_out, w_gate, w_up, w_down))
    for l in range(depth):
        h = _norm_matmul(xf, attn_pre_norm[l], [w_in_b], l, F32, tm=1024, tn=512)
        q, k, vt = _mla_proj(h, pos, mla_q_norm[l], mla_kv_norm[l], w_uq_b, w_uk_b, w_uvt_b, l, tm=MLA_TQ)
        o_mla = _mla_attn(q, k, vt, batch, seq)
        o_hg = _hgrn(h, hg_lower_bounds, hg_out_norm[l], l, batch, seq)
        o_ca = _chunk_attn(h, ca_rel_bias[l], ca_out_norm[l], batch, seq)
        xf = _out_proj(o_mla, o_hg, o_ca, xf, mla_out_norm[l], attn_post_norm[l], w_out_b, l, tm=512)
        hid = _norm_matmul(xf, ffn_pre_norm[l], [w_gate_b, w_up_b], l, BF16, tm=1024, tn=512)
        xf = _down_proj(hid, w_down_b, xf, ffn_post_norm[l], l, tm=512, tk=1024)
    return xf.reshape(batch, seq, d)
```

```python
import functools
import math

import jax
import jax.numpy as jnp
from jax import lax
from jax.experimental import pallas as pl
from jax.experimental.pallas import tpu as pltpu

F32 = jnp.float32
BF16 = jnp.bfloat16

EPS = 1e-6
MASK_VALUE = -1e30
TINY = 1e-30
CHUNK = 64

MLA_HEADS = 16
MLA_Q_RANK = 768
MLA_KV_RANK = 512
MLA_NOPE = 128
MLA_ROPE = 64
MLA_V = 128
ROPE_THETA = 10000.0
MLA_QK_PAD = 256

HG_HEADS = 8
HG_DIM = 128
HG_BLOCK = 16
HG_CHUNK = 128
HG_ROWS = 512

CA_HEADS = 8
CA_DIM = 128
CA_LEFT_CHUNKS = 8
CA_REL_CLIP = 256
CA_TQ = 256
CA_WIN = 3 * CA_TQ
CA_BIAS_LEN = 1024

MLA_WIDTH = MLA_HEADS * MLA_V
HG_WIDTH = HG_HEADS * HG_DIM
CA_WIDTH = CA_HEADS * CA_DIM

COL_AQ = 0
COL_AK = 1024
COL_AV = 2048
COL_HQ = 3072
COL_HF = 4096
COL_HI = 5120
COL_HG = 6144
COL_CKV = 7168
COL_CQ = 7680
COL_KR = 8448
D_IN_PAD = 8704

LANE = 128
VMEM_LIMIT = 56 * 1024 * 1024


def _cparams(sem, vmem=VMEM_LIMIT):
    return pltpu.CompilerParams(dimension_semantics=sem, vmem_limit_bytes=vmem)


def _rms_scale(x):
    return lax.rsqrt(jnp.mean(x * x, axis=-1, keepdims=True) + EPS)


NORM_ROWS = 16
NORM_UNROLL = 4


def _norm_rows_into(xn_ref, x_ref, g_ref):
    rows = x_ref.shape[0]

    def body(c, carry):
        r = pl.ds(pl.multiple_of(c * NORM_ROWS, NORM_ROWS), NORM_ROWS)
        x = x_ref[r, :]
        xn_ref[r, :] = ((x * _rms_scale(x)) * g_ref[...]).astype(BF16)
        return carry

    lax.fori_loop(0, rows // NORM_ROWS, body, 0, unroll=NORM_UNROLL)


def _norm_matmul_kernel(x_ref, g_ref, w_ref, o_ref, xn_ref):
    @pl.when(pl.program_id(1) == 0)
    def _():
        _norm_rows_into(xn_ref, x_ref, g_ref)

    o_ref[...] = jnp.dot(xn_ref[...], w_ref[...], preferred_element_type=F32).astype(o_ref.dtype)


def _norm_swiglu_kernel(x_ref, g_ref, wg_ref, wu_ref, o_ref, xn_ref):
    @pl.when(pl.program_id(1) == 0)
    def _():
        _norm_rows_into(xn_ref, x_ref, g_ref)

    xn = xn_ref[...]
    gate = jnp.dot(xn, wg_ref[...], preferred_element_type=F32)
    up = jnp.dot(xn, wu_ref[...], preferred_element_type=F32)
    o_ref[...] = ((gate * jax.nn.sigmoid(gate)) * up).astype(o_ref.dtype)


def _norm_matmul(x, g, ws, layer, out_dtype, tm, tn):
    t, d = x.shape
    n = ws[0].shape[2]
    kern = _norm_matmul_kernel if len(ws) == 1 else _norm_swiglu_kernel
    w_specs = [pl.BlockSpec((None, d, tn), lambda i, j: (layer, 0, j)) for _ in ws]
    return pl.pallas_call(
        kern,
        grid=(t // tm, pl.cdiv(n, tn)),
        in_specs=[pl.BlockSpec((tm, d), lambda i, j: (i, 0), pipeline_mode=pl.Buffered(1)),
                  pl.BlockSpec((1, d), lambda i, j: (0, 0))] + w_specs,
        out_specs=pl.BlockSpec((tm, tn), lambda i, j: (i, j)),
        out_shape=jax.ShapeDtypeStruct((t, n), out_dtype),
        scratch_shapes=[pltpu.VMEM((tm, d), BF16)],
        compiler_params=_cparams(("parallel", "arbitrary")),
        name="norm_matmul" if len(ws) == 1 else "norm_swiglu",
    )(x, g.reshape(1, d), *ws)


def _rope_tables(pos_ref):
    lane = lax.broadcasted_iota(jnp.int32, (1, LANE), 1)
    half = MLA_ROPE // 2
    idx = (lane % half).astype(F32)
    inv_freq = jnp.exp((-math.log(ROPE_THETA) * 2.0) * idx / MLA_ROPE)
    inv_freq = jnp.where(lane < MLA_ROPE, inv_freq, 0.0)
    ang = pos_ref[...].astype(F32) * inv_freq
    cos, sin = jnp.cos(ang), jnp.sin(ang)
    sin_hi = jnp.where((lane >= half) & (lane < MLA_ROPE), sin, 0.0)
    sin_lo = jnp.where(lane < half, -sin, 0.0)
    return cos, sin_hi, sin_lo


def _rope(x, tables):
    cos, sin_hi, sin_lo = tables
    half = MLA_ROPE // 2
    return x * cos + pltpu.roll(x, half, 1) * sin_hi + pltpu.roll(x, LANE - half, 1) * sin_lo


def _mla_proj_kernel(cq_ref, ckv_ref, kr_ref, pos_ref, gq_ref, gkv_ref, wq_ref, wk_ref, wvt_ref,
                     q_ref, k_ref, vt_ref):
    tables = _rope_tables(pos_ref)
    cq = cq_ref[...]
    cqn = ((cq * _rms_scale(cq)) * gq_ref[...]).astype(BF16)
    ckv = ckv_ref[...]
    ckvn = ((ckv * _rms_scale(ckv)) * gkv_ref[...]).astype(BF16)
    k_pe = _rope(kr_ref[...], tables).astype(BF16)
    for h in range(MLA_HEADS):
        c0 = h * MLA_QK_PAD
        qh = jnp.dot(cqn, wq_ref[:, c0:c0 + MLA_QK_PAD], preferred_element_type=F32)
        q_ref[:, c0:c0 + LANE] = qh[:, :LANE].astype(BF16)
        q_ref[:, c0 + LANE:c0 + MLA_QK_PAD] = _rope(qh[:, LANE:], tables).astype(BF16)
        kh = jnp.dot(ckvn, wk_ref[:, h * LANE:(h + 1) * LANE], preferred_element_type=F32)
        k_ref[:, c0:c0 + LANE] = kh.astype(BF16)
        k_ref[:, c0 + LANE:c0 + MLA_QK_PAD] = k_pe
    rows = 4 * LANE
    for c0 in range(0, MLA_WIDTH, rows):
        vt = lax.dot_general(wvt_ref[c0:c0 + rows, :], ckvn, (((1,), (1,)), ((), ())),
                             preferred_element_type=F32)
        vt_ref[0, c0:c0 + rows, :] = vt.astype(BF16)


def _mla_proj(h, positions, gq, gkv, wq, wk, wvt, layer, tm):
    t = h.shape[0]
    qk_w = MLA_HEADS * MLA_QK_PAD
    const = lambda i: (0, 0)
    return pl.pallas_call(
        _mla_proj_kernel,
        grid=(t // tm,),
        in_specs=[pl.BlockSpec((tm, MLA_Q_RANK), lambda i: (i, COL_CQ // MLA_Q_RANK)),
                  pl.BlockSpec((tm, MLA_KV_RANK), lambda i: (i, COL_CKV // MLA_KV_RANK)),
                  pl.BlockSpec((tm, LANE), lambda i: (i, COL_KR // LANE)),
                  pl.BlockSpec((tm, 1), lambda i: (i, 0)),
                  pl.BlockSpec((1, MLA_Q_RANK), const),
                  pl.BlockSpec((1, MLA_KV_RANK), const),
                  pl.BlockSpec((None,) + wq.shape[1:], lambda i: (layer, 0, 0)),
                  pl.BlockSpec((None,) + wk.shape[1:], lambda i: (layer, 0, 0)),
                  pl.BlockSpec((None,) + wvt.shape[1:], lambda i: (layer, 0, 0))],
        out_specs=[pl.BlockSpec((tm, qk_w), lambda i: (i, 0)),
                   pl.BlockSpec((tm, qk_w), lambda i: (i, 0)),
                   pl.BlockSpec((1, MLA_WIDTH, tm), lambda i: (i, 0, 0))],
        out_shape=[jax.ShapeDtypeStruct((t, qk_w), BF16),
                   jax.ShapeDtypeStruct((t, qk_w), BF16),
                   jax.ShapeDtypeStruct((t // tm, MLA_WIDTH, tm), BF16)],
        compiler_params=_cparams(("parallel",)),
        name="mla_proj",
    )(h, h, h, positions, gq.reshape(1, -1), gkv.reshape(1, -1), wq, wk, wvt)


MLA_TQ = 1024
MLA_TK = MLA_TQ // 2
MLA_QBLK = 256


def _mla_attn_kernel(q_ref, k_ref, vt_ref, o_ref, qt_ref, sa_ref, sb_ref, xa_ref, xb_ref, pa_ref, pb_ref,
                     m_ref, l_ref, alpha_ref, acc_ref):
    i = pl.program_id(2)
    tq, tk = MLA_TQ, MLA_TK
    qt_ref[...] = q_ref[...].astype(F32).T.astype(BF16)
    c = (MLA_NOPE + MLA_ROPE) ** -0.5 * math.log2(math.e)
    m_ref[...] = jnp.full(m_ref.shape, -jnp.inf, F32)
    l_ref[...] = jnp.zeros(l_ref.shape, F32)
    alpha_ref[...] = jnp.ones(alpha_ref.shape, F32)
    acc_ref[...] = jnp.zeros(acc_ref.shape, F32)
    pb_ref[...] = jnp.zeros(pb_ref.shape, BF16)

    def scores_into(s_ref, x_ref, t):
        r = pl.ds(pl.multiple_of(t * tk, tk), tk)
        s = jnp.dot(k_ref[r, :], qt_ref[...], preferred_element_type=F32)
        s_ref[...] = s
        x_ref[...] = jnp.max(s, axis=0, keepdims=True)

    def add_values(p_ref, vt_tile):
        acc_ref[...] = alpha_ref[...] * acc_ref[...] + jnp.dot(vt_tile, p_ref[...],
                                                               preferred_element_type=F32)

    def softmax_terms(s_ref, x_ref, p_ref, chunk_shift):
        for q0 in range(0, tq, MLA_QBLK):
            cols = slice(q0, q0 + MLA_QBLK)
            s = s_ref[:, cols]
            if chunk_shift is None:
                tile_max = x_ref[:, cols]
            else:
                kc = lax.broadcasted_iota(jnp.int32, s.shape, 0) // CHUNK + chunk_shift
                qc = (lax.broadcasted_iota(jnp.int32, s.shape, 1) + q0) // CHUNK
                s = jnp.where(kc <= qc, s, MASK_VALUE)
                tile_max = jnp.max(s, axis=0, keepdims=True)
            m_old = m_ref[:, cols]
            m_new = jnp.maximum(m_old, tile_max)
            alpha = jnp.exp2((m_old - m_new) * c)
            p = jnp.exp2((s - m_new) * c)
            l_ref[:, cols] = alpha * l_ref[:, cols] + jnp.sum(p, axis=0, keepdims=True)
            m_ref[:, cols] = m_new
            alpha_ref[:, cols] = alpha
            p_ref[:, cols] = p.astype(BF16)

    def pair(u, diagonal):
        scores_into(sb_ref, xb_ref, 2 * u + 1)
        add_values(pb_ref, vt_ref[jnp.maximum(2 * u - 1, 0)])
        softmax_terms(sa_ref, xa_ref, pa_ref, 0 if diagonal else None)
        if not diagonal:
            scores_into(sa_ref, xa_ref, 2 * u + 2)
        add_values(pa_ref, vt_ref[2 * u])
        softmax_terms(sb_ref, xb_ref, pb_ref, tk // CHUNK if diagonal else None)

    scores_into(sa_ref, xa_ref, 0)

    def body(u, carry):
        pair(u, False)
        return carry

    lax.fori_loop(0, i, body, 0)
    pair(i, True)
    add_values(pb_ref, vt_ref[2 * i + 1])
    o_ref[...] = (acc_ref[...] / l_ref[...]).T


def _mla_attn(q, k, vt, batch, seq):
    tq = MLA_TQ
    nq = seq // tq
    assert vt.shape[2] == MLA_TK
    return pl.pallas_call(
        _mla_attn_kernel,
        grid=(batch, MLA_HEADS, nq),
        in_specs=[pl.BlockSpec((tq, MLA_QK_PAD), lambda b, h, i: (b * nq + i, h)),
                  pl.BlockSpec((seq, MLA_QK_PAD), lambda b, h, i: (b, h)),
                  pl.BlockSpec((seq // MLA_TK, MLA_V, MLA_TK), lambda b, h, i: (b, h, 0))],
        out_specs=pl.BlockSpec((tq, MLA_V), lambda b, h, i: (b * nq + i, h)),
        out_shape=jax.ShapeDtypeStruct((batch * seq, MLA_WIDTH), F32),
        scratch_shapes=[pltpu.VMEM((MLA_QK_PAD, tq), BF16),
                        pltpu.VMEM((MLA_TK, tq), F32), pltpu.VMEM((MLA_TK, tq), F32),
                        pltpu.VMEM((1, tq), F32), pltpu.VMEM((1, tq), F32),
                        pltpu.VMEM((MLA_TK, tq), BF16), pltpu.VMEM((MLA_TK, tq), BF16),
                        pltpu.VMEM((1, tq), F32), pltpu.VMEM((1, tq), F32), pltpu.VMEM((1, tq), F32),
                        pltpu.VMEM((MLA_V, tq), F32)],
        compiler_params=_cparams(("parallel", "parallel", "arbitrary")),
        name="mla_attn",
    )(q, k, vt)


def _block_cumsum(x, row):
    r = row % HG_BLOCK
    s = 1
    while s < HG_BLOCK:
        x = x + jnp.where(r >= s, pltpu.roll(x, s, 0), 0.0)
        s *= 2
    return x


def _hgrn_chunk(hq, hf, hi, lb, state):
    c = HG_CHUNK
    nb = c // HG_BLOCK
    row = lax.broadcasted_iota(jnp.int32, (c, HG_DIM), 0)
    col = lax.broadcasted_iota(jnp.int32, (c, HG_DIM), 1)

    q = hq * jax.nn.sigmoid(hq)
    f = lb + (1.0 - lb) * jax.nn.sigmoid(hf)
    k = (1.0 - lb) * jax.nn.sigmoid(-hf)
    b = _block_cumsum(jnp.log(jnp.maximum(f, TINY)), row)
    b3 = b.reshape(nb, HG_BLOCK, HG_DIM)
    b_last3 = jnp.broadcast_to(b3[:, HG_BLOCK - 1:HG_BLOCK, :], b3.shape)
    b_last = b_last3.reshape(c, HG_DIM)

    q3 = q.reshape(nb, HG_BLOCK, HG_DIM)
    k3 = k.reshape(nb, HG_BLOCK, HG_DIM)
    v3 = hi.reshape(nb, HG_BLOCK, HG_DIM)
    irow = lax.broadcasted_iota(jnp.int32, b3.shape, 1)
    sub = 8
    o_parts = [jnp.zeros((nb, sub, HG_DIM), F32) for _ in range(HG_BLOCK // sub)]
    for j in range(HG_BLOCK):
        lo = (j // sub) * sub
        w = q3[:, lo:, :] * k3[:, j:j + 1, :] * jnp.exp(b3[:, lo:, :] - b3[:, j:j + 1, :])
        if j > lo:
            w = jnp.where(irow[:, lo:, :] >= j, w, 0.0)
        wv = jnp.sum(w, axis=-1, keepdims=True) * v3[:, j:j + 1, :]
        for part in range(lo // sub, HG_BLOCK // sub):
            r0 = part * sub - lo
            o_parts[part] = o_parts[part] + wv[:, r0:r0 + sub, :]
    o = jnp.concatenate(o_parts, axis=1).reshape(c, HG_DIM)

    q_dec = q * jnp.exp(b)
    k_dec_t = (k * jnp.exp(b_last - b)).T
    b_last_t = b_last.T
    v_bf = hi.astype(BF16)
    blk_of_col = col // HG_BLOCK
    k_stack = jnp.concatenate(
        [jnp.where(blk_of_col == j, k_dec_t, 0.0) for j in range(nb)], axis=0).astype(BF16)
    u_all = jnp.dot(k_stack, v_bf, preferred_element_type=F32)
    states = []
    for j in range(nb):
        states.append(state.astype(BF16))
        decay = jnp.exp(b_last_t[:, j * HG_BLOCK:j * HG_BLOCK + 1])
        state = decay * state + u_all[j * HG_DIM:(j + 1) * HG_DIM, :]
    s_stack = jnp.concatenate(states, axis=0)
    blk_of_row = row // HG_BLOCK
    q_exp = jnp.concatenate(
        [jnp.where(blk_of_row == j, q_dec, 0.0) for j in range(nb)], axis=1).astype(BF16)
    o = o + jnp.dot(q_exp, s_stack, preferred_element_type=F32)
    return o, state


def _hgrn_kernel(hq_ref, hf_ref, hi_ref, hg_ref, lbraw_ref, gn_ref, o_ref, state_ref, *, layer):
    @pl.when(pl.program_id(2) == 0)
    def _():
        state_ref[...] = jnp.zeros(state_ref.shape, F32)

    raw = lbraw_ref[...]
    e = jnp.exp(raw - jnp.max(raw, axis=0, keepdims=True))
    p = e / jnp.sum(e, axis=0, keepdims=True)
    lb = jnp.sum(p[:layer + 1, :], axis=0, keepdims=True) - p[0:1, :]

    def body(ci, carry):
        r = pl.ds(pl.multiple_of(ci * HG_CHUNK, HG_CHUNK), HG_CHUNK)
        o, state = _hgrn_chunk(hq_ref[r, :], hf_ref[r, :], hi_ref[r, :], lb, state_ref[...])
        state_ref[...] = state
        o = (o * _rms_scale(o)) * gn_ref[...]
        g = hg_ref[r, :]
        o_ref[r, :] = (o * (g * jax.nn.sigmoid(g))).astype(o_ref.dtype)
        return carry

    lax.fori_loop(0, hq_ref.shape[0] // HG_CHUNK, body, 0, unroll=2)


def _hgrn(h, lb_raw, gn, layer, batch, seq):
    rows = HG_ROWS
    nr = seq // rows
    depth = lb_raw.shape[0]

    def col_spec(col0):
        return pl.BlockSpec((rows, HG_DIM), lambda b, hh, c: (b * nr + c, col0 // HG_DIM + hh))

    return pl.pallas_call(
        functools.partial(_hgrn_kernel, layer=layer),
        grid=(batch, HG_HEADS, nr),
        in_specs=[col_spec(COL_HQ), col_spec(COL_HF), col_spec(COL_HI), col_spec(COL_HG),
                  pl.BlockSpec((depth, HG_DIM), lambda b, hh, c: (0, hh)),
                  pl.BlockSpec((1, HG_DIM), lambda b, hh, c: (0, hh))],
        out_specs=pl.BlockSpec((rows, HG_DIM), lambda b, hh, c: (b * nr + c, hh)),
        out_shape=jax.ShapeDtypeStruct((batch * seq, HG_WIDTH), BF16),
        scratch_shapes=[pltpu.VMEM((HG_DIM, HG_DIM), F32)],
        compiler_params=_cparams(("parallel", "parallel", "arbitrary")),
        name="hgrn",
    )(h, h, h, h, lb_raw, gn.reshape(1, -1))


def _ca_bias_rows(rel_bias):
    idx = jnp.arange(CA_BIAS_LEN)
    m = jnp.where(idx < CA_WIN, idx, idx - CA_BIAS_LEN)
    bucket = jnp.clip(2 * CA_TQ - m, -CA_REL_CLIP, CA_REL_CLIP) + CA_REL_CLIP
    return rel_bias[:, bucket].astype(F32)


def _ca_kernel(q_ref, k0_ref, k1_ref, k2_ref, v0_ref, v1_ref, v2_ref, brow_ref, gn_ref,
               o_ref, bias_ref):
    t = pl.program_id(1)
    tq = CA_TQ

    @pl.when((pl.program_id(0) == 0) & (t == 0))
    def _():
        qc = lax.broadcasted_iota(jnp.int32, (tq, CA_WIN), 0) // CHUNK
        kc = lax.broadcasted_iota(jnp.int32, (tq, CA_WIN), 1) // CHUNK
        band = (kc >= qc) & (kc <= qc + CA_LEFT_CHUNKS)
        for h in range(CA_HEADS):
            rows = jnp.broadcast_to(brow_ref[h:h + 1, :], (tq, CA_BIAS_LEN))
            rolled = pltpu.roll(rows, 0, 1, stride=1, stride_axis=0)
            bias_ref[h] = jnp.where(band, rolled[:, :CA_WIN], MASK_VALUE)

    first_valid = jnp.maximum(2 * tq - t * tq, 0)
    kk = lax.broadcasted_iota(jnp.int32, (tq, CA_WIN), 1)
    seq_ok = kk >= first_valid
    scale = CA_DIM ** -0.5
    outs = []
    for h in range(CA_HEADS):
        c = slice(h * CA_DIM, (h + 1) * CA_DIM)
        qh = q_ref[:, c].astype(BF16)
        kh = jnp.concatenate([k0_ref[:, c], k1_ref[:, c], k2_ref[:, c]], axis=0).astype(BF16)
        vh = jnp.concatenate([v0_ref[:, c], v1_ref[:, c], v2_ref[:, c]], axis=0).astype(BF16)
        s = lax.dot_general(qh, kh, (((1,), (1,)), ((), ())), preferred_element_type=F32) * scale
        s = jnp.where(seq_ok, s + bias_ref[h], MASK_VALUE)
        p = jnp.exp(s - jnp.max(s, axis=-1, keepdims=True))
        l = jnp.sum(p, axis=-1, keepdims=True)
        outs.append(jnp.dot(p.astype(BF16), vh, preferred_element_type=F32) / l)
    o = jnp.concatenate(outs, axis=1)
    o_ref[...] = ((o * _rms_scale(o)) * gn_ref[...]).astype(o_ref.dtype)


def _chunk_attn(h, rel_bias, gn, batch, seq):
    tq = CA_TQ
    nq = seq // tq
    cq, ck, cv = COL_AQ // CA_WIDTH, COL_AK // CA_WIDTH, COL_AV // CA_WIDTH

    def kv_spec(col, back):
        return pl.BlockSpec((tq, CA_WIDTH), lambda b, t: (b * nq + jnp.maximum(t - back, 0), col))

    return pl.pallas_call(
        _ca_kernel,
        grid=(batch, nq),
        in_specs=[pl.BlockSpec((tq, CA_WIDTH), lambda b, t: (b * nq + t, cq)),
                  kv_spec(ck, 2), kv_spec(ck, 1), kv_spec(ck, 0),
                  kv_spec(cv, 2), kv_spec(cv, 1), kv_spec(cv, 0),
                  pl.BlockSpec((CA_HEADS, CA_BIAS_LEN), lambda b, t: (0, 0)),
                  pl.BlockSpec((1, CA_WIDTH), lambda b, t: (0, 0))],
        out_specs=pl.BlockSpec((tq, CA_WIDTH), lambda b, t: (b * nq + t, 0)),
        out_shape=jax.ShapeDtypeStruct((batch * seq, CA_WIDTH), BF16),
        scratch_shapes=[pltpu.VMEM((CA_HEADS, tq, CA_WIN), F32)],
        compiler_params=_cparams(("arbitrary", "arbitrary")),
        name="chunk_attn",
    )(h, h, h, h, h, h, h, _ca_bias_rows(rel_bias), gn.reshape(1, -1))


X_SLICES = 8


def _residual_norm(o_ref, xs_ref, g_ref):
    rows, d = o_ref.shape
    w = d // X_SLICES

    step = NORM_ROWS * NORM_UNROLL

    def body(c, carry):
        base = pl.multiple_of(c * step, step)
        chunks = [pl.ds(base + u * NORM_ROWS, NORM_ROWS) for u in range(NORM_UNROLL)]
        ys = [o_ref[r, :] for r in chunks]
        yns = [(y * _rms_scale(y)) * g_ref[...] for y in ys]
        for r, yn in zip(chunks, yns):
            for p in range(X_SLICES):
                o_ref[r, p * w:(p + 1) * w] = xs_ref[p, r, :] + yn[:, p * w:(p + 1) * w]
        return carry

    lax.fori_loop(0, rows // step, body, 0)


def _x_slice_spec(tm, d):
    return pl.BlockSpec((tm, d // X_SLICES), lambda i, k: (i, jnp.minimum(k, X_SLICES - 1)))


def _accumulate_then_residual_norm(a_ref, w_ref, x_ref, g_ref, o_ref, xs_ref, ragged):
    k = pl.program_id(1)
    last = pl.num_programs(1) - 1

    @pl.when(k == 0)
    def _():
        o_ref[...] = jnp.zeros(o_ref.shape, F32)

    @pl.when(k < X_SLICES)
    def _():
        xs_ref[k] = x_ref[...]

    if ragged is None:
        o_ref[...] += jnp.dot(a_ref[...], w_ref[...], preferred_element_type=F32)
    else:
        @pl.when(k < last)
        def _():
            o_ref[...] += jnp.dot(a_ref[...], w_ref[...], preferred_element_type=F32)

        @pl.when(k == last)
        def _():
            a, w = a_ref[...], w_ref[...]
            a = jnp.where(lax.broadcasted_iota(jnp.int32, a.shape, 1) < ragged, a, jnp.zeros_like(a))
            w = jnp.where(lax.broadcasted_iota(jnp.int32, w.shape, 0) < ragged, w, jnp.zeros_like(w))
            o_ref[...] += jnp.dot(a, w, preferred_element_type=F32)

    @pl.when(k == last)
    def _():
        _residual_norm(o_ref, xs_ref, g_ref)


OUT_TK = 512


def _out_proj_kernel(mla_ref, hg_ref, ca_ref, w_ref, x_ref, gm_ref, gp_ref, o_ref, a_ref, xs_ref):
    @pl.when(pl.program_id(1) == 0)
    def _():
        rows = mla_ref.shape[0]
        n_mla, n_hg, n_ca = MLA_WIDTH // OUT_TK, HG_WIDTH // OUT_TK, CA_WIDTH // OUT_TK

        def body(c, carry):
            r = pl.ds(pl.multiple_of(c * NORM_ROWS, NORM_ROWS), NORM_ROWS)
            m = mla_ref[r, :]
            mn = ((m * _rms_scale(m)) * gm_ref[...]).astype(BF16)
            for p in range(n_mla):
                a_ref[p, r, :] = mn[:, p * OUT_TK:(p + 1) * OUT_TK]
            return carry

        lax.fori_loop(0, rows // NORM_ROWS, body, 0, unroll=NORM_UNROLL)
        for p in range(n_hg):
            a_ref[n_mla + p] = hg_ref[:, p * OUT_TK:(p + 1) * OUT_TK]
        for p in range(n_ca):
            a_ref[n_mla + n_hg + p] = ca_ref[:, p * OUT_TK:(p + 1) * OUT_TK]

    _accumulate_then_residual_norm(a_ref.at[pl.program_id(1)], w_ref, x_ref, gp_ref, o_ref, xs_ref, None)


def _out_proj(o_mla, o_hg, o_ca, x, g_mla, g_post, w, layer, tm):
    t, d = x.shape
    kdim = w.shape[1]
    nk = kdim // OUT_TK
    assert MLA_WIDTH % OUT_TK == 0 and HG_WIDTH % OUT_TK == 0 and CA_WIDTH % OUT_TK == 0
    assert kdim == MLA_WIDTH + HG_WIDTH + CA_WIDTH and nk >= X_SLICES
    return pl.pallas_call(
        _out_proj_kernel,
        grid=(t // tm, nk),
        in_specs=[pl.BlockSpec((tm, MLA_WIDTH), lambda i, k: (i, 0)),
                  pl.BlockSpec((tm, HG_WIDTH), lambda i, k: (i, 0)),
                  pl.BlockSpec((tm, CA_WIDTH), lambda i, k: (i, 0)),
                  pl.BlockSpec((None, OUT_TK, d), lambda i, k: (layer, k, 0)),
                  _x_slice_spec(tm, d),
                  pl.BlockSpec((1, MLA_WIDTH), lambda i, k: (0, 0)),
                  pl.BlockSpec((1, d), lambda i, k: (0, 0))],
        out_specs=pl.BlockSpec((tm, d), lambda i, k: (i, 0)),
        out_shape=jax.ShapeDtypeStruct((t, d), F32),
        scratch_shapes=[pltpu.VMEM((nk, tm, OUT_TK), BF16),
                        pltpu.VMEM((X_SLICES, tm, d // X_SLICES), F32)],
        compiler_params=_cparams(("parallel", "arbitrary")),
        name="out_proj",
    )(o_mla, o_hg, o_ca, w, x, g_mla.reshape(1, -1), g_post.reshape(1, -1))


def _down_proj_kernel(a_ref, w_ref, x_ref, g_ref, o_ref, xs_ref, *, ragged):
    _accumulate_then_residual_norm(a_ref, w_ref, x_ref, g_ref, o_ref, xs_ref, ragged)


def _down_proj(a, w, x, g, layer, tm, tk):
    t, d = x.shape
    kdim = a.shape[1]
    ragged = kdim % tk or None
    nk = pl.cdiv(kdim, tk)
    assert nk >= X_SLICES
    return pl.pallas_call(
        functools.partial(_down_proj_kernel, ragged=ragged),
        grid=(t // tm, nk),
        in_specs=[pl.BlockSpec((tm, tk), lambda i, k: (i, k)),
                  pl.BlockSpec((None, tk, d), lambda i, k: (layer, k, 0)),
                  _x_slice_spec(tm, d),
                  pl.BlockSpec((1, d), lambda i, k: (0, 0))],
        out_specs=pl.BlockSpec((tm, d), lambda i, k: (i, 0)),
        out_shape=jax.ShapeDtypeStruct((t, d), F32),
        scratch_shapes=[pltpu.VMEM((X_SLICES, tm, d // X_SLICES), F32)],
        compiler_params=_cparams(("parallel", "arbitrary")),
        name="down_proj",
    )(a, w, x, g.reshape(1, -1))


def _prep_w_in(w):
    depth, d, _ = w.shape
    sizes = (MLA_Q_RANK, MLA_KV_RANK, MLA_ROPE, HG_WIDTH, HG_WIDTH, HG_WIDTH, HG_WIDTH,
             CA_WIDTH, CA_WIDTH, CA_WIDTH)
    offs = [0]
    for s in sizes:
        offs.append(offs[-1] + s)
    cq, ckv, kr, hq, hf, hi, hg, aq, ak, av = (w[:, :, offs[n]:offs[n + 1]].astype(BF16)
                                               for n in range(len(sizes)))
    z = jnp.zeros((depth, d, D_IN_PAD - COL_KR - MLA_ROPE), BF16)
    out = jnp.concatenate([aq, ak, av, hq, hf, hi, hg, ckv, cq, kr, z], axis=2)
    assert out.shape[2] == D_IN_PAD
    return out


def _prep_w_uq(w):
    depth, r, _ = w.shape
    w4 = w.astype(BF16).reshape(depth, r, MLA_HEADS, MLA_NOPE + MLA_ROPE)
    pad = jnp.zeros((depth, r, MLA_HEADS, MLA_QK_PAD - MLA_NOPE - MLA_ROPE), BF16)
    return jnp.concatenate([w4, pad], axis=3).reshape(depth, r, MLA_HEADS * MLA_QK_PAD)


def _prep_w_ukv(w):
    depth, r, _ = w.shape
    w4 = w.astype(BF16).reshape(depth, r, MLA_HEADS, MLA_NOPE + MLA_V)
    wk = w4[:, :, :, :MLA_NOPE].reshape(depth, r, MLA_HEADS * MLA_NOPE)
    wv = w4[:, :, :, MLA_NOPE:].reshape(depth, r, MLA_HEADS * MLA_V)
    return wk, jnp.swapaxes(wv, 1, 2)


def kernel(x, positions, attn_pre_norm, attn_post_norm, w_in, mla_q_norm, mla_kv_norm, w_uq, w_ukv,
           mla_out_norm, hg_lower_bounds, hg_out_norm, ca_rel_bias, ca_out_norm, w_out, ffn_pre_norm,
           ffn_post_norm, w_gate, w_up, w_down):
    batch, seq, d = x.shape
    t = batch * seq
    depth = w_in.shape[0]
    xf = x.reshape(t, d)
    pos = positions.reshape(t, 1)
    w_in_b = _prep_w_in(w_in)
    w_uq_b = _prep_w_uq(w_uq)
    w_uk_b, w_uvt_b = _prep_w_ukv(w_ukv)
    w_out_b, w_gate_b, w_up_b, w_down_b = (w.astype(BF16) for w in (w_out, w_gate, w_up, w_down))
    for l in range(depth):
        h = _norm_matmul(xf, attn_pre_norm[l], [w_in_b], l, F32, tm=1024, tn=512)
        q, k, vt = _mla_proj(h, pos, mla_q_norm[l], mla_kv_norm[l], w_uq_b, w_uk_b, w_uvt_b, l, tm=MLA_TK)
        o_mla = _mla_attn(q, k, vt, batch, seq)
        o_hg = _hgrn(h, hg_lower_bounds, hg_out_norm[l], l, batch, seq)
        o_ca = _chunk_attn(h, ca_rel_bias[l], ca_out_norm[l], batch, seq)
        xf = _out_proj(o_mla, o_hg, o_ca, xf, mla_out_norm[l], attn_post_norm[l], w_out_b, l, tm=512)
        hid = _norm_matmul(xf, ffn_pre_norm[l], [w_gate_b, w_up_b], l, BF16, tm=1024, tn=512)
        xf = _down_proj(hid, w_down_b, xf, ffn_post_norm[l], l, tm=512, tk=1024)
    return xf.reshape(batch, seq, d)
```

```python
import functools
import math

import jax
import jax.numpy as jnp
from jax import lax
from jax.experimental import pallas as pl
from jax.experimental.pallas import tpu as pltpu

F32 = jnp.float32
BF16 = jnp.bfloat16

EPS = 1e-6
MASK_VALUE = -1e30
TINY = 1e-30
CHUNK = 64

MLA_HEADS = 16
MLA_Q_RANK = 768
MLA_KV_RANK = 512
MLA_NOPE = 128
MLA_ROPE = 64
MLA_V = 128
ROPE_THETA = 10000.0
MLA_QK_PAD = 256
MLA_ONES = 16
MLA_VT_ROWS = MLA_V + MLA_ONES
MLA_EXP_SCALE = (MLA_NOPE + MLA_ROPE) ** -0.5 * math.log2(math.e)

HG_HEADS = 8
HG_DIM = 128
HG_BLOCK = 16
HG_CHUNK = 128
HG_ROWS = 512

CA_HEADS = 8
CA_DIM = 128
CA_LEFT_CHUNKS = 8
CA_REL_CLIP = 256
CA_TQ = 256
CA_WIN = 3 * CA_TQ
CA_BIAS_LEN = 1024

MLA_WIDTH = MLA_HEADS * MLA_V
HG_WIDTH = HG_HEADS * HG_DIM
CA_WIDTH = CA_HEADS * CA_DIM

COL_AQ = 0
COL_AK = 1024
COL_AV = 2048
COL_HQ = 3072
COL_HF = 4096
COL_HI = 5120
COL_HG = 6144
COL_CKV = 7168
COL_CQ = 7680
COL_KR = 8448
D_IN_PAD = 8704

LANE = 128
VMEM_LIMIT = 56 * 1024 * 1024


def _cparams(sem, vmem=VMEM_LIMIT):
    return pltpu.CompilerParams(dimension_semantics=sem, vmem_limit_bytes=vmem)


def _rms_scale(x):
    return lax.rsqrt(jnp.mean(x * x, axis=-1, keepdims=True) + EPS)


NORM_ROWS = 16
NORM_UNROLL = 4


def _norm_rows_into(xn_ref, x_ref, g_ref):
    rows = x_ref.shape[0]

    def body(c, carry):
        r = pl.ds(pl.multiple_of(c * NORM_ROWS, NORM_ROWS), NORM_ROWS)
        x = x_ref[r, :]
        xn_ref[r, :] = ((x * _rms_scale(x)) * g_ref[...]).astype(BF16)
        return carry

    lax.fori_loop(0, rows // NORM_ROWS, body, 0, unroll=NORM_UNROLL)


def _norm_matmul_kernel(x_ref, g_ref, w_ref, o_ref, xn_ref):
    @pl.when(pl.program_id(1) == 0)
    def _():
        _norm_rows_into(xn_ref, x_ref, g_ref)

    o_ref[...] = jnp.dot(xn_ref[...], w_ref[...], preferred_element_type=F32).astype(o_ref.dtype)


def _norm_swiglu_kernel(x_ref, g_ref, wg_ref, wu_ref, o_ref, xn_ref):
    @pl.when(pl.program_id(1) == 0)
    def _():
        _norm_rows_into(xn_ref, x_ref, g_ref)

    xn = xn_ref[...]
    gate = jnp.dot(xn, wg_ref[...], preferred_element_type=F32)
    up = jnp.dot(xn, wu_ref[...], preferred_element_type=F32)
    o_ref[...] = ((gate * jax.nn.sigmoid(gate)) * up).astype(o_ref.dtype)


def _norm_matmul(x, g, ws, layer, out_dtype, tm, tn):
    t, d = x.shape
    n = ws[0].shape[2]
    kern = _norm_matmul_kernel if len(ws) == 1 else _norm_swiglu_kernel
    w_specs = [pl.BlockSpec((None, d, tn), lambda i, j: (layer, 0, j)) for _ in ws]
    return pl.pallas_call(
        kern,
        grid=(t // tm, pl.cdiv(n, tn)),
        in_specs=[pl.BlockSpec((tm, d), lambda i, j: (i, 0), pipeline_mode=pl.Buffered(1)),
                  pl.BlockSpec((1, d), lambda i, j: (0, 0))] + w_specs,
        out_specs=pl.BlockSpec((tm, tn), lambda i, j: (i, j)),
        out_shape=jax.ShapeDtypeStruct((t, n), out_dtype),
        scratch_shapes=[pltpu.VMEM((tm, d), BF16)],
        compiler_params=_cparams(("parallel", "arbitrary")),
        name="norm_matmul" if len(ws) == 1 else "norm_swiglu",
    )(x, g.reshape(1, d), *ws)


def _rope_tables(pos_ref):
    lane = lax.broadcasted_iota(jnp.int32, (1, LANE), 1)
    half = MLA_ROPE // 2
    idx = (lane % half).astype(F32)
    inv_freq = jnp.exp((-math.log(ROPE_THETA) * 2.0) * idx / MLA_ROPE)
    inv_freq = jnp.where(lane < MLA_ROPE, inv_freq, 0.0)
    ang = pos_ref[...].astype(F32) * inv_freq
    cos, sin = jnp.cos(ang), jnp.sin(ang)
    sin_hi = jnp.where((lane >= half) & (lane < MLA_ROPE), sin, 0.0)
    sin_lo = jnp.where(lane < half, -sin, 0.0)
    return cos, sin_hi, sin_lo


def _rope(x, tables):
    cos, sin_hi, sin_lo = tables
    half = MLA_ROPE // 2
    return x * cos + pltpu.roll(x, half, 1) * sin_hi + pltpu.roll(x, LANE - half, 1) * sin_lo


def _mla_proj_kernel(cq_ref, ckv_ref, kr_ref, pos_ref, gq_ref, gkv_ref, wq_ref, wk_ref, wvt_ref,
                     q_ref, k_ref, vt_ref):
    tables = _rope_tables(pos_ref)
    cq = cq_ref[...]
    cqn = ((cq * _rms_scale(cq)) * gq_ref[...]).astype(BF16)
    ckv = ckv_ref[...]
    ckvn = ((ckv * _rms_scale(ckv)) * gkv_ref[...]).astype(BF16)
    k_pe = _rope(kr_ref[...], tables).astype(BF16)
    for h in range(MLA_HEADS):
        c0 = h * MLA_QK_PAD
        qh = jnp.dot(cqn, wq_ref[:, c0:c0 + MLA_QK_PAD], preferred_element_type=F32) * MLA_EXP_SCALE
        q_ref[:, c0:c0 + LANE] = qh[:, :LANE].astype(BF16)
        q_ref[:, c0 + LANE:c0 + MLA_QK_PAD] = _rope(qh[:, LANE:], tables).astype(BF16)
        kh = jnp.dot(ckvn, wk_ref[:, h * LANE:(h + 1) * LANE], preferred_element_type=F32)
        k_ref[:, c0:c0 + LANE] = kh.astype(BF16)
        k_ref[:, c0 + LANE:c0 + MLA_QK_PAD] = k_pe
    group = 4
    ones = jnp.ones((MLA_ONES, vt_ref.shape[2]), BF16)
    for h0 in range(0, MLA_HEADS, group):
        vt = lax.dot_general(wvt_ref[h0 * MLA_V:(h0 + group) * MLA_V, :], ckvn, (((1,), (1,)), ((), ())),
                             preferred_element_type=F32).astype(BF16)
        for h in range(h0, h0 + group):
            r0 = h * MLA_VT_ROWS
            vt_ref[0, r0:r0 + MLA_V, :] = vt[(h - h0) * MLA_V:(h - h0 + 1) * MLA_V, :]
            vt_ref[0, r0 + MLA_V:r0 + MLA_VT_ROWS, :] = ones


def _mla_proj(h, positions, gq, gkv, wq, wk, wvt, layer, tm):
    t = h.shape[0]
    qk_w = MLA_HEADS * MLA_QK_PAD
    const = lambda i: (0, 0)
    return pl.pallas_call(
        _mla_proj_kernel,
        grid=(t // tm,),
        in_specs=[pl.BlockSpec((tm, MLA_Q_RANK), lambda i: (i, COL_CQ // MLA_Q_RANK)),
                  pl.BlockSpec((tm, MLA_KV_RANK), lambda i: (i, COL_CKV // MLA_KV_RANK)),
                  pl.BlockSpec((tm, LANE), lambda i: (i, COL_KR // LANE)),
                  pl.BlockSpec((tm, 1), lambda i: (i, 0)),
                  pl.BlockSpec((1, MLA_Q_RANK), const),
                  pl.BlockSpec((1, MLA_KV_RANK), const),
                  pl.BlockSpec((None,) + wq.shape[1:], lambda i: (layer, 0, 0)),
                  pl.BlockSpec((None,) + wk.shape[1:], lambda i: (layer, 0, 0)),
                  pl.BlockSpec((None,) + wvt.shape[1:], lambda i: (layer, 0, 0))],
        out_specs=[pl.BlockSpec((tm, qk_w), lambda i: (i, 0)),
                   pl.BlockSpec((tm, qk_w), lambda i: (i, 0)),
                   pl.BlockSpec((1, MLA_HEADS * MLA_VT_ROWS, tm), lambda i: (i, 0, 0))],
        out_shape=[jax.ShapeDtypeStruct((t, qk_w), BF16),
                   jax.ShapeDtypeStruct((t, qk_w), BF16),
                   jax.ShapeDtypeStruct((t // tm, MLA_HEADS * MLA_VT_ROWS, tm), BF16)],
        compiler_params=_cparams(("parallel",)),
        name="mla_proj",
    )(h, h, h, positions, gq.reshape(1, -1), gkv.reshape(1, -1), wq, wk, wvt)


MLA_TQ = 1024
MLA_TK = MLA_TQ // 2
MLA_QBLK = 256
MLA_KBLK = 128


def _mla_attn_kernel(q_ref, k_ref, vt_ref, o_ref, qt_ref, sa_ref, sb_ref, xa_ref, xb_ref, pa_ref, pb_ref,
                     m_ref, alpha_ref, acc_ref):
    i = pl.program_id(2)
    tq, tk = MLA_TQ, MLA_TK
    qt_ref[...] = q_ref[...].astype(F32).T.astype(BF16)
    m_ref[...] = jnp.full(m_ref.shape, -jnp.inf, F32)
    alpha_ref[...] = jnp.ones(alpha_ref.shape, F32)
    acc_ref[...] = jnp.zeros(acc_ref.shape, F32)
    pb_ref[...] = jnp.zeros(pb_ref.shape, BF16)

    def scores_into(s_ref, x_ref, t):
        r = pl.ds(pl.multiple_of(t * tk, tk), tk)
        s = jnp.dot(k_ref[r, :], qt_ref[...], preferred_element_type=F32)
        s_ref[...] = s
        x_ref[...] = jnp.max(s, axis=0, keepdims=True)

    def add_values(p_ref, vt_tile):
        acc_ref[...] = alpha_ref[...] * acc_ref[...] + jnp.dot(vt_tile, p_ref[...],
                                                               preferred_element_type=F32)

    def softmax_terms(s_ref, x_ref, p_ref, chunk_shift):
        for q0 in range(0, tq, MLA_QBLK):
            cols = slice(q0, q0 + MLA_QBLK)

            def masked(s, k0):
                kc = (lax.broadcasted_iota(jnp.int32, s.shape, 0) + k0) // CHUNK + chunk_shift
                qc = (lax.broadcasted_iota(jnp.int32, s.shape, 1) + q0) // CHUNK
                return jnp.where(kc <= qc, s, MASK_VALUE)

            if chunk_shift is None:
                tile_max = x_ref[:, cols]
            else:
                tile_max = jnp.max(masked(s_ref[:, cols], 0), axis=0, keepdims=True)
            m_old = m_ref[:, cols]
            m_new = jnp.maximum(m_old, tile_max)
            m_ref[:, cols] = m_new
            alpha_ref[:, cols] = jnp.exp2(m_old - m_new)
            for k0 in range(0, tk, MLA_KBLK):
                rows = slice(k0, k0 + MLA_KBLK)
                s = s_ref[rows, cols]
                if chunk_shift is not None:
                    s = masked(s, k0)
                p_ref[rows, cols] = jnp.exp2(s - m_new).astype(BF16)

    def pair(u, diagonal):
        scores_into(sb_ref, xb_ref, 2 * u + 1)
        add_values(pb_ref, vt_ref[jnp.maximum(2 * u - 1, 0)])
        softmax_terms(sa_ref, xa_ref, pa_ref, 0 if diagonal else None)
        if not diagonal:
            scores_into(sa_ref, xa_ref, 2 * u + 2)
        add_values(pa_ref, vt_ref[2 * u])
        softmax_terms(sb_ref, xb_ref, pb_ref, tk // CHUNK if diagonal else None)

    scores_into(sa_ref, xa_ref, 0)

    def body(u, carry):
        pair(u, False)
        return carry

    lax.fori_loop(0, i, body, 0)
    pair(i, True)
    add_values(pb_ref, vt_ref[2 * i + 1])
    o_ref[...] = (acc_ref[:MLA_V, :] / acc_ref[MLA_V:MLA_V + 1, :]).T


def _mla_attn(q, k, vt, batch, seq):
    tq = MLA_TQ
    nq = seq // tq
    assert vt.shape[2] == MLA_TK
    return pl.pallas_call(
        _mla_attn_kernel,
        grid=(batch, MLA_HEADS, nq),
        in_specs=[pl.BlockSpec((tq, MLA_QK_PAD), lambda b, h, i: (b * nq + i, h)),
                  pl.BlockSpec((seq, MLA_QK_PAD), lambda b, h, i: (b, h)),
                  pl.BlockSpec((seq // MLA_TK, MLA_VT_ROWS, MLA_TK), lambda b, h, i: (b, h, 0))],
        out_specs=pl.BlockSpec((tq, MLA_V), lambda b, h, i: (b * nq + i, h)),
        out_shape=jax.ShapeDtypeStruct((batch * seq, MLA_WIDTH), F32),
        scratch_shapes=[pltpu.VMEM((MLA_QK_PAD, tq), BF16),
                        pltpu.VMEM((MLA_TK, tq), F32), pltpu.VMEM((MLA_TK, tq), F32),
                        pltpu.VMEM((1, tq), F32), pltpu.VMEM((1, tq), F32),
                        pltpu.VMEM((MLA_TK, tq), BF16), pltpu.VMEM((MLA_TK, tq), BF16),
                        pltpu.VMEM((1, tq), F32), pltpu.VMEM((1, tq), F32),
                        pltpu.VMEM((MLA_VT_ROWS, tq), F32)],
        compiler_params=_cparams(("parallel", "parallel", "arbitrary")),
        name="mla_attn",
    )(q, k, vt)


def _block_cumsum(x, row):
    r = row % HG_BLOCK
    s = 1
    while s < HG_BLOCK:
        x = x + jnp.where(r >= s, pltpu.roll(x, s, 0), 0.0)
        s *= 2
    return x


def _hgrn_chunk(hq, hf, hi, lb, state):
    c = HG_CHUNK
    nb = c // HG_BLOCK
    row = lax.broadcasted_iota(jnp.int32, (c, HG_DIM), 0)
    col = lax.broadcasted_iota(jnp.int32, (c, HG_DIM), 1)

    q = hq * jax.nn.sigmoid(hq)
    f = lb + (1.0 - lb) * jax.nn.sigmoid(hf)
    k = (1.0 - lb) * jax.nn.sigmoid(-hf)
    b = _block_cumsum(jnp.log(jnp.maximum(f, TINY)), row)
    b3 = b.reshape(nb, HG_BLOCK, HG_DIM)
    b_last3 = jnp.broadcast_to(b3[:, HG_BLOCK - 1:HG_BLOCK, :], b3.shape)
    b_last = b_last3.reshape(c, HG_DIM)

    q3 = q.reshape(nb, HG_BLOCK, HG_DIM)
    k3 = k.reshape(nb, HG_BLOCK, HG_DIM)
    v3 = hi.reshape(nb, HG_BLOCK, HG_DIM)
    irow = lax.broadcasted_iota(jnp.int32, b3.shape, 1)
    sub = 8
    o_parts = [jnp.zeros((nb, sub, HG_DIM), F32) for _ in range(HG_BLOCK // sub)]
    for j in range(HG_BLOCK):
        lo = (j // sub) * sub
        w = q3[:, lo:, :] * k3[:, j:j + 1, :] * jnp.exp(b3[:, lo:, :] - b3[:, j:j + 1, :])
        if j > lo:
            w = jnp.where(irow[:, lo:, :] >= j, w, 0.0)
        wv = jnp.sum(w, axis=-1, keepdims=True) * v3[:, j:j + 1, :]
        for part in range(lo // sub, HG_BLOCK // sub):
            r0 = part * sub - lo
            o_parts[part] = o_parts[part] + wv[:, r0:r0 + sub, :]
    o = jnp.concatenate(o_parts, axis=1).reshape(c, HG_DIM)

    q_dec = q * jnp.exp(b)
    k_dec_t = (k * jnp.exp(b_last - b)).T
    b_last_t = b_last.T
    v_bf = hi.astype(BF16)
    blk_of_col = col // HG_BLOCK
    k_stack = jnp.concatenate(
        [jnp.where(blk_of_col == j, k_dec_t, 0.0) for j in range(nb)], axis=0).astype(BF16)
    u_all = jnp.dot(k_stack, v_bf, preferred_element_type=F32)
    states = []
    for j in range(nb):
        states.append(state.astype(BF16))
        decay = jnp.exp(b_last_t[:, j * HG_BLOCK:j * HG_BLOCK + 1])
        state = decay * state + u_all[j * HG_DIM:(j + 1) * HG_DIM, :]
    s_stack = jnp.concatenate(states, axis=0)
    blk_of_row = row // HG_BLOCK
    q_exp = jnp.concatenate(
        [jnp.where(blk_of_row == j, q_dec, 0.0) for j in range(nb)], axis=1).astype(BF16)
    o = o + jnp.dot(q_exp, s_stack, preferred_element_type=F32)
    return o, state


def _hgrn_kernel(hq_ref, hf_ref, hi_ref, hg_ref, lbraw_ref, gn_ref, o_ref, state_ref, *, layer):
    @pl.when(pl.program_id(2) == 0)
    def _():
        state_ref[...] = jnp.zeros(state_ref.shape, F32)

    raw = lbraw_ref[...]
    e = jnp.exp(raw - jnp.max(raw, axis=0, keepdims=True))
    p = e / jnp.sum(e, axis=0, keepdims=True)
    lb = jnp.sum(p[:layer + 1, :], axis=0, keepdims=True) - p[0:1, :]

    def body(ci, carry):
        r = pl.ds(pl.multiple_of(ci * HG_CHUNK, HG_CHUNK), HG_CHUNK)
        o, state = _hgrn_chunk(hq_ref[r, :], hf_ref[r, :], hi_ref[r, :], lb, state_ref[...])
        state_ref[...] = state
        o = (o * _rms_scale(o)) * gn_ref[...]
        g = hg_ref[r, :]
        o_ref[r, :] = (o * (g * jax.nn.sigmoid(g))).astype(o_ref.dtype)
        return carry

    lax.fori_loop(0, hq_ref.shape[0] // HG_CHUNK, body, 0, unroll=2)


def _hgrn(h, lb_raw, gn, layer, batch, seq):
    rows = HG_ROWS
    nr = seq // rows
    depth = lb_raw.shape[0]

    def col_spec(col0):
        return pl.BlockSpec((rows, HG_DIM), lambda b, hh, c: (b * nr + c, col0 // HG_DIM + hh))

    return pl.pallas_call(
        functools.partial(_hgrn_kernel, layer=layer),
        grid=(batch, HG_HEADS, nr),
        in_specs=[col_spec(COL_HQ), col_spec(COL_HF), col_spec(COL_HI), col_spec(COL_HG),
                  pl.BlockSpec((depth, HG_DIM), lambda b, hh, c: (0, hh)),
                  pl.BlockSpec((1, HG_DIM), lambda b, hh, c: (0, hh))],
        out_specs=pl.BlockSpec((rows, HG_DIM), lambda b, hh, c: (b * nr + c, hh)),
        out_shape=jax.ShapeDtypeStruct((batch * seq, HG_WIDTH), BF16),
        scratch_shapes=[pltpu.VMEM((HG_DIM, HG_DIM), F32)],
        compiler_params=_cparams(("parallel", "parallel", "arbitrary")),
        name="hgrn",
    )(h, h, h, h, lb_raw, gn.reshape(1, -1))


def _ca_bias_rows(rel_bias):
    idx = jnp.arange(CA_BIAS_LEN)
    m = jnp.where(idx < CA_WIN, idx, idx - CA_BIAS_LEN)
    bucket = jnp.clip(2 * CA_TQ - m, -CA_REL_CLIP, CA_REL_CLIP) + CA_REL_CLIP
    return rel_bias[:, bucket].astype(F32)


def _ca_kernel(q_ref, k0_ref, k1_ref, k2_ref, v0_ref, v1_ref, v2_ref, brow_ref, gn_ref,
               o_ref, bias_ref):
    t = pl.program_id(1)
    tq = CA_TQ

    @pl.when((pl.program_id(0) == 0) & (t == 0))
    def _():
        qc = lax.broadcasted_iota(jnp.int32, (tq, CA_WIN), 0) // CHUNK
        kc = lax.broadcasted_iota(jnp.int32, (tq, CA_WIN), 1) // CHUNK
        band = (kc >= qc) & (kc <= qc + CA_LEFT_CHUNKS)
        for h in range(CA_HEADS):
            rows = jnp.broadcast_to(brow_ref[h:h + 1, :], (tq, CA_BIAS_LEN))
            rolled = pltpu.roll(rows, 0, 1, stride=1, stride_axis=0)
            bias_ref[h] = jnp.where(band, rolled[:, :CA_WIN], MASK_VALUE)

    first_valid = jnp.maximum(2 * tq - t * tq, 0)
    kk = lax.broadcasted_iota(jnp.int32, (tq, CA_WIN), 1)
    seq_ok = kk >= first_valid
    scale = CA_DIM ** -0.5
    outs = []
    for h in range(CA_HEADS):
        c = slice(h * CA_DIM, (h + 1) * CA_DIM)
        qh = q_ref[:, c].astype(BF16)
        kh = jnp.concatenate([k0_ref[:, c], k1_ref[:, c], k2_ref[:, c]], axis=0).astype(BF16)
        vh = jnp.concatenate([v0_ref[:, c], v1_ref[:, c], v2_ref[:, c]], axis=0).astype(BF16)
        s = lax.dot_general(qh, kh, (((1,), (1,)), ((), ())), preferred_element_type=F32) * scale
        s = jnp.where(seq_ok, s + bias_ref[h], MASK_VALUE)
        p = jnp.exp(s - jnp.max(s, axis=-1, keepdims=True))
        l = jnp.sum(p, axis=-1, keepdims=True)
        outs.append(jnp.dot(p.astype(BF16), vh, preferred_element_type=F32) / l)
    o = jnp.concatenate(outs, axis=1)
    o_ref[...] = ((o * _rms_scale(o)) * gn_ref[...]).astype(o_ref.dtype)


def _chunk_attn(h, rel_bias, gn, batch, seq):
    tq = CA_TQ
    nq = seq // tq
    cq, ck, cv = COL_AQ // CA_WIDTH, COL_AK // CA_WIDTH, COL_AV // CA_WIDTH

    def kv_spec(col, back):
        return pl.BlockSpec((tq, CA_WIDTH), lambda b, t: (b * nq + jnp.maximum(t - back, 0), col))

    return pl.pallas_call(
        _ca_kernel,
        grid=(batch, nq),
        in_specs=[pl.BlockSpec((tq, CA_WIDTH), lambda b, t: (b * nq + t, cq)),
                  kv_spec(ck, 2), kv_spec(ck, 1), kv_spec(ck, 0),
                  kv_spec(cv, 2), kv_spec(cv, 1), kv_spec(cv, 0),
                  pl.BlockSpec((CA_HEADS, CA_BIAS_LEN), lambda b, t: (0, 0)),
                  pl.BlockSpec((1, CA_WIDTH), lambda b, t: (0, 0))],
        out_specs=pl.BlockSpec((tq, CA_WIDTH), lambda b, t: (b * nq + t, 0)),
        out_shape=jax.ShapeDtypeStruct((batch * seq, CA_WIDTH), BF16),
        scratch_shapes=[pltpu.VMEM((CA_HEADS, tq, CA_WIN), F32)],
        compiler_params=_cparams(("arbitrary", "arbitrary")),
        name="chunk_attn",
    )(h, h, h, h, h, h, h, _ca_bias_rows(rel_bias), gn.reshape(1, -1))


X_SLICES = 8


def _residual_norm(o_ref, xs_ref, g_ref):
    rows, d = o_ref.shape
    w = d // X_SLICES

    step = NORM_ROWS * NORM_UNROLL

    def body(c, carry):
        base = pl.multiple_of(c * step, step)
        chunks = [pl.ds(base + u * NORM_ROWS, NORM_ROWS) for u in range(NORM_UNROLL)]
        ys = [o_ref[r, :] for r in chunks]
        yns = [(y * _rms_scale(y)) * g_ref[...] for y in ys]
        for r, yn in zip(chunks, yns):
            for p in range(X_SLICES):
                o_ref[r, p * w:(p + 1) * w] = xs_ref[p, r, :] + yn[:, p * w:(p + 1) * w]
        return carry

    lax.fori_loop(0, rows // step, body, 0)


def _x_slice_spec(tm, d):
    return pl.BlockSpec((tm, d // X_SLICES), lambda i, k: (i, jnp.minimum(k, X_SLICES - 1)))


def _accumulate_then_residual_norm(a_ref, w_ref, x_ref, g_ref, o_ref, xs_ref, ragged):
    k = pl.program_id(1)
    last = pl.num_programs(1) - 1

    @pl.when(k == 0)
    def _():
        o_ref[...] = jnp.zeros(o_ref.shape, F32)

    @pl.when(k < X_SLICES)
    def _():
        xs_ref[k] = x_ref[...]

    if ragged is None:
        o_ref[...] += jnp.dot(a_ref[...], w_ref[...], preferred_element_type=F32)
    else:
        @pl.when(k < last)
        def _():
            o_ref[...] += jnp.dot(a_ref[...], w_ref[...], preferred_element_type=F32)

        @pl.when(k == last)
        def _():
            a, w = a_ref[...], w_ref[...]
            a = jnp.where(lax.broadcasted_iota(jnp.int32, a.shape, 1) < ragged, a, jnp.zeros_like(a))
            w = jnp.where(lax.broadcasted_iota(jnp.int32, w.shape, 0) < ragged, w, jnp.zeros_like(w))
            o_ref[...] += jnp.dot(a, w, preferred_element_type=F32)

    @pl.when(k == last)
    def _():
        _residual_norm(o_ref, xs_ref, g_ref)


OUT_TK = 512


def _out_proj_kernel(mla_ref, hg_ref, ca_ref, w_ref, x_ref, gm_ref, gp_ref, o_ref, a_ref, xs_ref):
    @pl.when(pl.program_id(1) == 0)
    def _():
        rows = mla_ref.shape[0]
        n_mla, n_hg, n_ca = MLA_WIDTH // OUT_TK, HG_WIDTH // OUT_TK, CA_WIDTH // OUT_TK

        def body(c, carry):
            r = pl.ds(pl.multiple_of(c * NORM_ROWS, NORM_ROWS), NORM_ROWS)
            m = mla_ref[r, :]
            mn = ((m * _rms_scale(m)) * gm_ref[...]).astype(BF16)
            for p in range(n_mla):
                a_ref[p, r, :] = mn[:, p * OUT_TK:(p + 1) * OUT_TK]
            return carry

        lax.fori_loop(0, rows // NORM_ROWS, body, 0, unroll=NORM_UNROLL)
        for p in range(n_hg):
            a_ref[n_mla + p] = hg_ref[:, p * OUT_TK:(p + 1) * OUT_TK]
        for p in range(n_ca):
            a_ref[n_mla + n_hg + p] = ca_ref[:, p * OUT_TK:(p + 1) * OUT_TK]

    _accumulate_then_residual_norm(a_ref.at[pl.program_id(1)], w_ref, x_ref, gp_ref, o_ref, xs_ref, None)


def _out_proj(o_mla, o_hg, o_ca, x, g_mla, g_post, w, layer, tm):
    t, d = x.shape
    kdim = w.shape[1]
    nk = kdim // OUT_TK
    assert MLA_WIDTH % OUT_TK == 0 and HG_WIDTH % OUT_TK == 0 and CA_WIDTH % OUT_TK == 0
    assert kdim == MLA_WIDTH + HG_WIDTH + CA_WIDTH and nk >= X_SLICES
    return pl.pallas_call(
        _out_proj_kernel,
        grid=(t // tm, nk),
        in_specs=[pl.BlockSpec((tm, MLA_WIDTH), lambda i, k: (i, 0)),
                  pl.BlockSpec((tm, HG_WIDTH), lambda i, k: (i, 0)),
                  pl.BlockSpec((tm, CA_WIDTH), lambda i, k: (i, 0)),
                  pl.BlockSpec((None, OUT_TK, d), lambda i, k: (layer, k, 0)),
                  _x_slice_spec(tm, d),
                  pl.BlockSpec((1, MLA_WIDTH), lambda i, k: (0, 0)),
                  pl.BlockSpec((1, d), lambda i, k: (0, 0))],
        out_specs=pl.BlockSpec((tm, d), lambda i, k: (i, 0)),
        out_shape=jax.ShapeDtypeStruct((t, d), F32),
        scratch_shapes=[pltpu.VMEM((nk, tm, OUT_TK), BF16),
                        pltpu.VMEM((X_SLICES, tm, d // X_SLICES), F32)],
        compiler_params=_cparams(("parallel", "arbitrary")),
        name="out_proj",
    )(o_mla, o_hg, o_ca, w, x, g_mla.reshape(1, -1), g_post.reshape(1, -1))


def _down_proj_kernel(a_ref, w_ref, x_ref, g_ref, o_ref, xs_ref, *, ragged):
    _accumulate_then_residual_norm(a_ref, w_ref, x_ref, g_ref, o_ref, xs_ref, ragged)


def _down_proj(a, w, x, g, layer, tm, tk):
    t, d = x.shape
    kdim = a.shape[1]
    ragged = kdim % tk or None
    nk = pl.cdiv(kdim, tk)
    assert nk >= X_SLICES
    return pl.pallas_call(
        functools.partial(_down_proj_kernel, ragged=ragged),
        grid=(t // tm, nk),
        in_specs=[pl.BlockSpec((tm, tk), lambda i, k: (i, k)),
                  pl.BlockSpec((None, tk, d), lambda i, k: (layer, k, 0)),
                  _x_slice_spec(tm, d),
                  pl.BlockSpec((1, d), lambda i, k: (0, 0))],
        out_specs=pl.BlockSpec((tm, d), lambda i, k: (i, 0)),
        out_shape=jax.ShapeDtypeStruct((t, d), F32),
        scratch_shapes=[pltpu.VMEM((X_SLICES, tm, d // X_SLICES), F32)],
        compiler_params=_cparams(("parallel", "arbitrary")),
        name="down_proj",
    )(a, w, x, g.reshape(1, -1))


def _prep_w_in(w):
    depth, d, _ = w.shape
    o_ckv = MLA_Q_RANK
    o_kr = o_ckv + MLA_KV_RANK
    o_hg = o_kr + MLA_ROPE
    o_ca = o_hg + 4 * HG_WIDTH
    assert (COL_AQ, COL_AK, COL_AV) == (0, CA_WIDTH, 2 * CA_WIDTH) and COL_HQ == 3 * CA_WIDTH
    assert (COL_HF, COL_HI, COL_HG) == (COL_HQ + HG_WIDTH, COL_HQ + 2 * HG_WIDTH, COL_HQ + 3 * HG_WIDTH)
    assert COL_CKV == COL_HG + HG_WIDTH and COL_CQ == COL_CKV + MLA_KV_RANK and COL_KR == COL_CQ + MLA_Q_RANK
    wb = w.astype(BF16)
    z = jnp.zeros((depth, d, D_IN_PAD - COL_KR - MLA_ROPE), BF16)
    out = jnp.concatenate([wb[:, :, o_ca:], wb[:, :, o_hg:o_ca], wb[:, :, o_ckv:o_kr], wb[:, :, :o_ckv],
                           wb[:, :, o_kr:o_hg], z], axis=2)
    assert out.shape[2] == D_IN_PAD
    return out


def _prep_w_uq(w):
    depth, r, _ = w.shape
    w4 = w.astype(BF16).reshape(depth, r, MLA_HEADS, MLA_NOPE + MLA_ROPE)
    pad = jnp.zeros((depth, r, MLA_HEADS, MLA_QK_PAD - MLA_NOPE - MLA_ROPE), BF16)
    return jnp.concatenate([w4, pad], axis=3).reshape(depth, r, MLA_HEADS * MLA_QK_PAD)


def _prep_w_ukv(w):
    depth, r, _ = w.shape
    w4 = w.astype(BF16).reshape(depth, r, MLA_HEADS, MLA_NOPE + MLA_V)
    wk = w4[:, :, :, :MLA_NOPE].reshape(depth, r, MLA_HEADS * MLA_NOPE)
    wv = w4[:, :, :, MLA_NOPE:].reshape(depth, r, MLA_HEADS * MLA_V)
    return wk, jnp.swapaxes(wv, 1, 2)


def kernel(x, positions, attn_pre_norm, attn_post_norm, w_in, mla_q_norm, mla_kv_norm, w_uq, w_ukv,
           mla_out_norm, hg_lower_bounds, hg_out_norm, ca_rel_bias, ca_out_norm, w_out, ffn_pre_norm,
           ffn_post_norm, w_gate, w_up, w_down):
    batch, seq, d = x.shape
    t = batch * seq
    depth = w_in.shape[0]
    xf = x.reshape(t, d)
    pos = positions.reshape(t, 1)
    w_in_b = _prep_w_in(w_in)
    w_uq_b = _prep_w_uq(w_uq)
    w_uk_b, w_uvt_b = _prep_w_ukv(w_ukv)
    w_out_b, w_gate_b, w_up_b, w_down_b = (w.astype(BF16) for w in (w_out, w_gate, w_up, w_down))
    for l in range(depth):
        h = _norm_matmul(xf, attn_pre_norm[l], [w_in_b], l, F32, tm=1024, tn=512)
        q, k, vt = _mla_proj(h, pos, mla_q_norm[l], mla_kv_norm[l], w_uq_b, w_uk_b, w_uvt_b, l, tm=MLA_TK)
        o_mla = _mla_attn(q, k, vt, batch, seq)
        o_hg = _hgrn(h, hg_lower_bounds, hg_out_norm[l], l, batch, seq)
        o_ca = _chunk_attn(h, ca_rel_bias[l], ca_out_norm[l], batch, seq)
        xf = _out_proj(o_mla, o_hg, o_ca, xf, mla_out_norm[l], attn_post_norm[l], w_out_b, l, tm=512)
        hid = _norm_matmul(xf, ffn_pre_norm[l], [w_gate_b, w_up_b], l, BF16, tm=1024, tn=512)
        xf = _down_proj(hid, w_down_b, xf, ffn_post_norm[l], l, tm=512, tk=1024)
    return xf.reshape(batch, seq, d)
```

```python
import functools
import math

import jax
import jax.numpy as jnp
from jax import lax
from jax.experimental import pallas as pl
from jax.experimental.pallas import tpu as pltpu

F32 = jnp.float32
BF16 = jnp.bfloat16

EPS = 1e-6
MASK_VALUE = -1e30
TINY = 1e-30
CHUNK = 64

MLA_HEADS = 16
MLA_Q_RANK = 768
MLA_KV_RANK = 512
MLA_NOPE = 128
MLA_ROPE = 64
MLA_V = 128
ROPE_THETA = 10000.0
MLA_QK_PAD = 256
MLA_ONES = 16
MLA_VT_ROWS = MLA_V + MLA_ONES
MLA_EXP_SCALE = (MLA_NOPE + MLA_ROPE) ** -0.5 * math.log2(math.e)

HG_HEADS = 8
HG_DIM = 128
HG_BLOCK = 16
HG_CHUNK = 128
HG_ROWS = 512

CA_HEADS = 8
CA_DIM = 128
CA_LEFT_CHUNKS = 8
CA_REL_CLIP = 256
CA_TQ = 256
CA_WIN = 3 * CA_TQ
CA_BIAS_LEN = 1024

MLA_WIDTH = MLA_HEADS * MLA_V
HG_WIDTH = HG_HEADS * HG_DIM
CA_WIDTH = CA_HEADS * CA_DIM

COL_AQ = 0
COL_AK = 1024
COL_AV = 2048
COL_HQ = 3072
COL_HF = 4096
COL_HI = 5120
COL_HG = 6144
COL_CKV = 7168
COL_CQ = 7680
COL_KR = 8448
D_IN_PAD = 8704

LANE = 128
VMEM_LIMIT = 56 * 1024 * 1024


def _cparams(sem, vmem=VMEM_LIMIT):
    return pltpu.CompilerParams(dimension_semantics=sem, vmem_limit_bytes=vmem)


def _rms_scale(x):
    return lax.rsqrt(jnp.mean(x * x, axis=-1, keepdims=True) + EPS)


NORM_ROWS = 16
NORM_UNROLL = 4


def _norm_rows_into(xn_ref, x_ref, g_ref):
    rows = x_ref.shape[0]

    def body(c, carry):
        r = pl.ds(pl.multiple_of(c * NORM_ROWS, NORM_ROWS), NORM_ROWS)
        x = x_ref[r, :]
        xn_ref[r, :] = ((x * _rms_scale(x)) * g_ref[...]).astype(BF16)
        return carry

    lax.fori_loop(0, rows // NORM_ROWS, body, 0, unroll=NORM_UNROLL)


def _norm_matmul_kernel(x_ref, g_ref, w_ref, o_ref, xn_ref):
    @pl.when(pl.program_id(1) == 0)
    def _():
        _norm_rows_into(xn_ref, x_ref, g_ref)

    o_ref[...] = jnp.dot(xn_ref[...], w_ref[...], preferred_element_type=F32).astype(o_ref.dtype)


def _norm_swiglu_kernel(x_ref, g_ref, wg_ref, wu_ref, o_ref, xn_ref):
    @pl.when(pl.program_id(1) == 0)
    def _():
        _norm_rows_into(xn_ref, x_ref, g_ref)

    xn = xn_ref[...]
    gate = jnp.dot(xn, wg_ref[...], preferred_element_type=F32)
    up = jnp.dot(xn, wu_ref[...], preferred_element_type=F32)
    o_ref[...] = ((gate * jax.nn.sigmoid(gate)) * up).astype(o_ref.dtype)


def _norm_matmul(x, g, ws, layer, out_dtype, tm, tn):
    t, d = x.shape
    n = ws[0].shape[2]
    kern = _norm_matmul_kernel if len(ws) == 1 else _norm_swiglu_kernel
    w_specs = [pl.BlockSpec((None, d, tn), lambda i, j: (layer, 0, j)) for _ in ws]
    return pl.pallas_call(
        kern,
        grid=(t // tm, pl.cdiv(n, tn)),
        in_specs=[pl.BlockSpec((tm, d), lambda i, j: (i, 0), pipeline_mode=pl.Buffered(1)),
                  pl.BlockSpec((1, d), lambda i, j: (0, 0))] + w_specs,
        out_specs=pl.BlockSpec((tm, tn), lambda i, j: (i, j)),
        out_shape=jax.ShapeDtypeStruct((t, n), out_dtype),
        scratch_shapes=[pltpu.VMEM((tm, d), BF16)],
        compiler_params=_cparams(("parallel", "arbitrary")),
        name="norm_matmul" if len(ws) == 1 else "norm_swiglu",
    )(x, g.reshape(1, d), *ws)


def _rope_tables(pos_ref):
    lane = lax.broadcasted_iota(jnp.int32, (1, LANE), 1)
    half = MLA_ROPE // 2
    idx = (lane % half).astype(F32)
    inv_freq = jnp.exp((-math.log(ROPE_THETA) * 2.0) * idx / MLA_ROPE)
    inv_freq = jnp.where(lane < MLA_ROPE, inv_freq, 0.0)
    ang = pos_ref[...].astype(F32) * inv_freq
    cos, sin = jnp.cos(ang), jnp.sin(ang)
    sin_hi = jnp.where((lane >= half) & (lane < MLA_ROPE), sin, 0.0)
    sin_lo = jnp.where(lane < half, -sin, 0.0)
    return cos, sin_hi, sin_lo


def _rope(x, tables):
    cos, sin_hi, sin_lo = tables
    half = MLA_ROPE // 2
    return x * cos + pltpu.roll(x, half, 1) * sin_hi + pltpu.roll(x, LANE - half, 1) * sin_lo


def _mla_proj_kernel(cq_ref, ckv_ref, kr_ref, pos_ref, gq_ref, gkv_ref, wq_ref, wk_ref, wvt_ref,
                     q_ref, k_ref, vt_ref):
    tables = _rope_tables(pos_ref)
    cq = cq_ref[...]
    cqn = ((cq * _rms_scale(cq)) * gq_ref[...]).astype(BF16)
    ckv = ckv_ref[...]
    ckvn = ((ckv * _rms_scale(ckv)) * gkv_ref[...]).astype(BF16)
    k_pe = _rope(kr_ref[...], tables).astype(BF16)
    for h in range(MLA_HEADS):
        c0 = h * MLA_QK_PAD
        qh = jnp.dot(cqn, wq_ref[:, c0:c0 + MLA_QK_PAD], preferred_element_type=F32) * MLA_EXP_SCALE
        q_ref[:, c0:c0 + LANE] = qh[:, :LANE].astype(BF16)
        q_ref[:, c0 + LANE:c0 + MLA_QK_PAD] = _rope(qh[:, LANE:], tables).astype(BF16)
        kh = jnp.dot(ckvn, wk_ref[:, h * LANE:(h + 1) * LANE], preferred_element_type=F32)
        k_ref[:, c0:c0 + LANE] = kh.astype(BF16)
        k_ref[:, c0 + LANE:c0 + MLA_QK_PAD] = k_pe
    group = 4
    ones = jnp.ones((MLA_ONES, vt_ref.shape[2]), BF16)
    for h0 in range(0, MLA_HEADS, group):
        vt = lax.dot_general(wvt_ref[h0 * MLA_V:(h0 + group) * MLA_V, :], ckvn, (((1,), (1,)), ((), ())),
                             preferred_element_type=F32).astype(BF16)
        for h in range(h0, h0 + group):
            r0 = h * MLA_VT_ROWS
            vt_ref[0, r0:r0 + MLA_V, :] = vt[(h - h0) * MLA_V:(h - h0 + 1) * MLA_V, :]
            vt_ref[0, r0 + MLA_V:r0 + MLA_VT_ROWS, :] = ones


def _mla_proj(h, positions, gq, gkv, wq, wk, wvt, layer, tm):
    t = h.shape[0]
    qk_w = MLA_HEADS * MLA_QK_PAD
    const = lambda i: (0, 0)
    return pl.pallas_call(
        _mla_proj_kernel,
        grid=(t // tm,),
        in_specs=[pl.BlockSpec((tm, MLA_Q_RANK), lambda i: (i, COL_CQ // MLA_Q_RANK)),
                  pl.BlockSpec((tm, MLA_KV_RANK), lambda i: (i, COL_CKV // MLA_KV_RANK)),
                  pl.BlockSpec((tm, LANE), lambda i: (i, COL_KR // LANE)),
                  pl.BlockSpec((tm, 1), lambda i: (i, 0)),
                  pl.BlockSpec((1, MLA_Q_RANK), const),
                  pl.BlockSpec((1, MLA_KV_RANK), const),
                  pl.BlockSpec((None,) + wq.shape[1:], lambda i: (layer, 0, 0)),
                  pl.BlockSpec((None,) + wk.shape[1:], lambda i: (layer, 0, 0)),
                  pl.BlockSpec((None,) + wvt.shape[1:], lambda i: (layer, 0, 0))],
        out_specs=[pl.BlockSpec((tm, qk_w), lambda i: (i, 0)),
                   pl.BlockSpec((tm, qk_w), lambda i: (i, 0)),
                   pl.BlockSpec((1, MLA_HEADS * MLA_VT_ROWS, tm), lambda i: (i, 0, 0))],
        out_shape=[jax.ShapeDtypeStruct((t, qk_w), BF16),
                   jax.ShapeDtypeStruct((t, qk_w), BF16),
                   jax.ShapeDtypeStruct((t // tm, MLA_HEADS * MLA_VT_ROWS, tm), BF16)],
        compiler_params=_cparams(("parallel",)),
        name="mla_proj",
    )(h, h, h, positions, gq.reshape(1, -1), gkv.reshape(1, -1), wq, wk, wvt)


MLA_TQ = 1024
MLA_TK = MLA_TQ // 2
MLA_QBLK = 256
MLA_KBLK = 128


def _mla_attn_kernel(q_ref, k_ref, vt_ref, o_ref, qt_ref, sa_ref, sb_ref, xa_ref, xb_ref, pa_ref, pb_ref,
                     m_ref, alpha_ref, acc_ref):
    i = pl.program_id(2)
    tq, tk = MLA_TQ, MLA_TK
    qt_ref[...] = q_ref[...].astype(F32).T.astype(BF16)
    m_ref[...] = jnp.full(m_ref.shape, -jnp.inf, F32)
    alpha_ref[...] = jnp.ones(alpha_ref.shape, F32)
    acc_ref[...] = jnp.zeros(acc_ref.shape, F32)
    pb_ref[...] = jnp.zeros(pb_ref.shape, BF16)

    def scores_into(s_ref, x_ref, t, q_lo=0):
        r = pl.ds(pl.multiple_of(t * tk, tk), tk)
        s = jnp.dot(k_ref[r, :], qt_ref[:, q_lo:], preferred_element_type=F32)
        s_ref[:, q_lo:] = s
        x_ref[:, q_lo:] = jnp.max(s, axis=0, keepdims=True)

    def add_values(p_ref, vt_tile, q_lo=0):
        acc_ref[:, q_lo:] = alpha_ref[:, q_lo:] * acc_ref[:, q_lo:] + jnp.dot(
            vt_tile, p_ref[:, q_lo:], preferred_element_type=F32)

    def softmax_terms(s_ref, x_ref, p_ref, chunk_shift, q_lo=0):
        for q0 in range(q_lo, tq, MLA_QBLK):
            cols = slice(q0, q0 + MLA_QBLK)

            def masked(s, k0):
                kc = (lax.broadcasted_iota(jnp.int32, s.shape, 0) + k0) // CHUNK + chunk_shift
                qc = (lax.broadcasted_iota(jnp.int32, s.shape, 1) + q0) // CHUNK
                return jnp.where(kc <= qc, s, MASK_VALUE)

            if chunk_shift is None:
                tile_max = x_ref[:, cols]
            else:
                tile_max = jnp.max(masked(s_ref[:, cols], 0), axis=0, keepdims=True)
            m_old = m_ref[:, cols]
            m_new = jnp.maximum(m_old, tile_max)
            m_ref[:, cols] = m_new
            alpha_ref[:, cols] = jnp.exp2(m_old - m_new)
            for k0 in range(0, tk, MLA_KBLK):
                rows = slice(k0, k0 + MLA_KBLK)
                s = s_ref[rows, cols]
                if chunk_shift is not None:
                    s = masked(s, k0)
                p_ref[rows, cols] = jnp.exp2(s - m_new).astype(BF16)

    def pair(u, diagonal):
        q_lo = tk if diagonal else 0
        scores_into(sb_ref, xb_ref, 2 * u + 1, q_lo)
        add_values(pb_ref, vt_ref[jnp.maximum(2 * u - 1, 0)])
        softmax_terms(sa_ref, xa_ref, pa_ref, 0 if diagonal else None)
        if not diagonal:
            scores_into(sa_ref, xa_ref, 2 * u + 2)
        add_values(pa_ref, vt_ref[2 * u])
        softmax_terms(sb_ref, xb_ref, pb_ref, tk // CHUNK if diagonal else None, q_lo)

    scores_into(sa_ref, xa_ref, 0)

    def body(u, carry):
        pair(u, False)
        return carry

    lax.fori_loop(0, i, body, 0)
    pair(i, True)
    add_values(pb_ref, vt_ref[2 * i + 1], tk)
    o_ref[...] = (acc_ref[:MLA_V, :] / acc_ref[MLA_V:MLA_V + 1, :]).T


def _mla_attn(q, k, vt, batch, seq):
    tq = MLA_TQ
    nq = seq // tq
    assert vt.shape[2] == MLA_TK
    return pl.pallas_call(
        _mla_attn_kernel,
        grid=(batch, MLA_HEADS, nq),
        in_specs=[pl.BlockSpec((tq, MLA_QK_PAD), lambda b, h, i: (b * nq + i, h)),
                  pl.BlockSpec((seq, MLA_QK_PAD), lambda b, h, i: (b, h)),
                  pl.BlockSpec((seq // MLA_TK, MLA_VT_ROWS, MLA_TK), lambda b, h, i: (b, h, 0))],
        out_specs=pl.BlockSpec((tq, MLA_V), lambda b, h, i: (b * nq + i, h)),
        out_shape=jax.ShapeDtypeStruct((batch * seq, MLA_WIDTH), F32),
        scratch_shapes=[pltpu.VMEM((MLA_QK_PAD, tq), BF16),
                        pltpu.VMEM((MLA_TK, tq), F32), pltpu.VMEM((MLA_TK, tq), F32),
                        pltpu.VMEM((1, tq), F32), pltpu.VMEM((1, tq), F32),
                        pltpu.VMEM((MLA_TK, tq), BF16), pltpu.VMEM((MLA_TK, tq), BF16),
                        pltpu.VMEM((1, tq), F32), pltpu.VMEM((1, tq), F32),
                        pltpu.VMEM((MLA_VT_ROWS, tq), F32)],
        compiler_params=_cparams(("parallel", "parallel", "arbitrary")),
        name="mla_attn",
    )(q, k, vt)


def _block_cumsum(x, row):
    r = row % HG_BLOCK
    s = 1
    while s < HG_BLOCK:
        x = x + jnp.where(r >= s, pltpu.roll(x, s, 0), 0.0)
        s *= 2
    return x


def _hgrn_chunk(hq, hf, hi, lb, state):
    c = HG_CHUNK
    nb = c // HG_BLOCK
    row = lax.broadcasted_iota(jnp.int32, (c, HG_DIM), 0)
    col = lax.broadcasted_iota(jnp.int32, (c, HG_DIM), 1)

    q = hq * jax.nn.sigmoid(hq)
    f = lb + (1.0 - lb) * jax.nn.sigmoid(hf)
    k = (1.0 - lb) * jax.nn.sigmoid(-hf)
    b = _block_cumsum(jnp.log(jnp.maximum(f, TINY)), row)
    b3 = b.reshape(nb, HG_BLOCK, HG_DIM)
    b_last3 = jnp.broadcast_to(b3[:, HG_BLOCK - 1:HG_BLOCK, :], b3.shape)
    b_last = b_last3.reshape(c, HG_DIM)

    q3 = q.reshape(nb, HG_BLOCK, HG_DIM)
    k3 = k.reshape(nb, HG_BLOCK, HG_DIM)
    v3 = hi.reshape(nb, HG_BLOCK, HG_DIM)
    irow = lax.broadcasted_iota(jnp.int32, b3.shape, 1)
    sub = 8
    ws = []
    for j in range(HG_BLOCK):
        lo = (j // sub) * sub
        w = q3[:, lo:, :] * k3[:, j:j + 1, :] * jnp.exp(b3[:, lo:, :] - b3[:, j:j + 1, :])
        if j > lo:
            w = jnp.where(irow[:, lo:, :] >= j, w, 0.0)
        ws.append(w.reshape(nb * (HG_BLOCK - lo), HG_DIM))
    sums = jnp.dot(jnp.concatenate(ws, axis=0).astype(BF16), jnp.ones((HG_DIM, HG_DIM), BF16),
                   preferred_element_type=F32)
    o_parts = [jnp.zeros((nb, sub, HG_DIM), F32) for _ in range(HG_BLOCK // sub)]
    r_at = 0
    for j in range(HG_BLOCK):
        lo = (j // sub) * sub
        n_rows = nb * (HG_BLOCK - lo)
        a_j = sums[r_at:r_at + n_rows, :].reshape(nb, HG_BLOCK - lo, HG_DIM)
        r_at += n_rows
        wv = a_j * v3[:, j:j + 1, :]
        for part in range(lo // sub, HG_BLOCK // sub):
            r0 = part * sub - lo
            o_parts[part] = o_parts[part] + wv[:, r0:r0 + sub, :]
    o = jnp.concatenate(o_parts, axis=1).reshape(c, HG_DIM)

    q_dec = q * jnp.exp(b)
    k_dec_t = (k * jnp.exp(b_last - b)).T
    b_last_t = b_last.T
    v_bf = hi.astype(BF16)
    blk_of_col = col // HG_BLOCK
    k_stack = jnp.concatenate(
        [jnp.where(blk_of_col == j, k_dec_t, 0.0) for j in range(nb)], axis=0).astype(BF16)
    u_all = jnp.dot(k_stack, v_bf, preferred_element_type=F32)
    states = []
    for j in range(nb):
        states.append(state.astype(BF16))
        decay = jnp.exp(b_last_t[:, j * HG_BLOCK:j * HG_BLOCK + 1])
        state = decay * state + u_all[j * HG_DIM:(j + 1) * HG_DIM, :]
    s_stack = jnp.concatenate(states, axis=0)
    blk_of_row = row // HG_BLOCK
    q_exp = jnp.concatenate(
        [jnp.where(blk_of_row == j, q_dec, 0.0) for j in range(nb)], axis=1).astype(BF16)
    o = o + jnp.dot(q_exp, s_stack, preferred_element_type=F32)
    return o, state


def _hgrn_kernel(hq_ref, hf_ref, hi_ref, hg_ref, lbraw_ref, gn_ref, o_ref, state_ref, *, layer):
    @pl.when(pl.program_id(2) == 0)
    def _():
        state_ref[...] = jnp.zeros(state_ref.shape, F32)

    raw = lbraw_ref[...]
    e = jnp.exp(raw - jnp.max(raw, axis=0, keepdims=True))
    p = e / jnp.sum(e, axis=0, keepdims=True)
    lb = jnp.sum(p[:layer + 1, :], axis=0, keepdims=True) - p[0:1, :]

    def body(ci, carry):
        r = pl.ds(pl.multiple_of(ci * HG_CHUNK, HG_CHUNK), HG_CHUNK)
        o, state = _hgrn_chunk(hq_ref[r, :], hf_ref[r, :], hi_ref[r, :], lb, state_ref[...])
        state_ref[...] = state
        o = (o * _rms_scale(o)) * gn_ref[...]
        g = hg_ref[r, :]
        o_ref[r, :] = (o * (g * jax.nn.sigmoid(g))).astype(o_ref.dtype)
        return carry

    lax.fori_loop(0, hq_ref.shape[0] // HG_CHUNK, body, 0, unroll=2)


def _hgrn(h, lb_raw, gn, layer, batch, seq):
    rows = HG_ROWS
    nr = seq // rows
    depth = lb_raw.shape[0]

    def col_spec(col0):
        return pl.BlockSpec((rows, HG_DIM), lambda b, hh, c: (b * nr + c, col0 // HG_DIM + hh))

    return pl.pallas_call(
        functools.partial(_hgrn_kernel, layer=layer),
        grid=(batch, HG_HEADS, nr),
        in_specs=[col_spec(COL_HQ), col_spec(COL_HF), col_spec(COL_HI), col_spec(COL_HG),
                  pl.BlockSpec((depth, HG_DIM), lambda b, hh, c: (0, hh)),
                  pl.BlockSpec((1, HG_DIM), lambda b, hh, c: (0, hh))],
        out_specs=pl.BlockSpec((rows, HG_DIM), lambda b, hh, c: (b * nr + c, hh)),
        out_shape=jax.ShapeDtypeStruct((batch * seq, HG_WIDTH), BF16),
        scratch_shapes=[pltpu.VMEM((HG_DIM, HG_DIM), F32)],
        compiler_params=_cparams(("parallel", "parallel", "arbitrary")),
        name="hgrn",
    )(h, h, h, h, lb_raw, gn.reshape(1, -1))


def _ca_bias_rows(rel_bias):
    idx = jnp.arange(CA_BIAS_LEN)
    m = jnp.where(idx < CA_WIN, idx, idx - CA_BIAS_LEN)
    bucket = jnp.clip(2 * CA_TQ - m, -CA_REL_CLIP, CA_REL_CLIP) + CA_REL_CLIP
    return rel_bias[:, bucket].astype(F32)


def _ca_kernel(q_ref, k0_ref, k1_ref, k2_ref, v0_ref, v1_ref, v2_ref, brow_ref, gn_ref,
               o_ref, bias_ref):
    t = pl.program_id(1)
    tq = CA_TQ

    @pl.when((pl.program_id(0) == 0) & (t == 0))
    def _():
        qc = lax.broadcasted_iota(jnp.int32, (tq, CA_WIN), 0) // CHUNK
        kc = lax.broadcasted_iota(jnp.int32, (tq, CA_WIN), 1) // CHUNK
        band = (kc >= qc) & (kc <= qc + CA_LEFT_CHUNKS)
        for h in range(CA_HEADS):
            rows = jnp.broadcast_to(brow_ref[h:h + 1, :], (tq, CA_BIAS_LEN))
            rolled = pltpu.roll(rows, 0, 1, stride=1, stride_axis=0)
            bias_ref[h] = jnp.where(band, rolled[:, :CA_WIN], MASK_VALUE)

    first_valid = jnp.maximum(2 * tq - t * tq, 0)
    kk = lax.broadcasted_iota(jnp.int32, (tq, CA_WIN), 1)
    seq_ok = kk >= first_valid
    scale = CA_DIM ** -0.5
    outs = []
    for h in range(CA_HEADS):
        c = slice(h * CA_DIM, (h + 1) * CA_DIM)
        qh = q_ref[:, c].astype(BF16)
        kh = jnp.concatenate([k0_ref[:, c], k1_ref[:, c], k2_ref[:, c]], axis=0).astype(BF16)
        vh = jnp.concatenate([v0_ref[:, c], v1_ref[:, c], v2_ref[:, c]], axis=0).astype(BF16)
        s = lax.dot_general(qh, kh, (((1,), (1,)), ((), ())), preferred_element_type=F32) * scale
        s = jnp.where(seq_ok, s + bias_ref[h], MASK_VALUE)
        p = jnp.exp(s - jnp.max(s, axis=-1, keepdims=True))
        l = jnp.sum(p, axis=-1, keepdims=True)
        outs.append(jnp.dot(p.astype(BF16), vh, preferred_element_type=F32) / l)
    o = jnp.concatenate(outs, axis=1)
    o_ref[...] = ((o * _rms_scale(o)) * gn_ref[...]).astype(o_ref.dtype)


def _chunk_attn(h, rel_bias, gn, batch, seq):
    tq = CA_TQ
    nq = seq // tq
    cq, ck, cv = COL_AQ // CA_WIDTH, COL_AK // CA_WIDTH, COL_AV // CA_WIDTH

    def kv_spec(col, back):
        return pl.BlockSpec((tq, CA_WIDTH), lambda b, t: (b * nq + jnp.maximum(t - back, 0), col))

    return pl.pallas_call(
        _ca_kernel,
        grid=(batch, nq),
        in_specs=[pl.BlockSpec((tq, CA_WIDTH), lambda b, t: (b * nq + t, cq)),
                  kv_spec(ck, 2), kv_spec(ck, 1), kv_spec(ck, 0),
                  kv_spec(cv, 2), kv_spec(cv, 1), kv_spec(cv, 0),
                  pl.BlockSpec((CA_HEADS, CA_BIAS_LEN), lambda b, t: (0, 0)),
                  pl.BlockSpec((1, CA_WIDTH), lambda b, t: (0, 0))],
        out_specs=pl.BlockSpec((tq, CA_WIDTH), lambda b, t: (b * nq + t, 0)),
        out_shape=jax.ShapeDtypeStruct((batch * seq, CA_WIDTH), BF16),
        scratch_shapes=[pltpu.VMEM((CA_HEADS, tq, CA_WIN), F32)],
        compiler_params=_cparams(("arbitrary", "arbitrary")),
        name="chunk_attn",
    )(h, h, h, h, h, h, h, _ca_bias_rows(rel_bias), gn.reshape(1, -1))


X_SLICES = 8


def _residual_norm(o_ref, xs_ref, g_ref):
    rows, d = o_ref.shape
    w = d // X_SLICES

    step = NORM_ROWS * NORM_UNROLL

    def body(c, carry):
        base = pl.multiple_of(c * step, step)
        chunks = [pl.ds(base + u * NORM_ROWS, NORM_ROWS) for u in range(NORM_UNROLL)]
        ys = [o_ref[r, :] for r in chunks]
        yns = [(y * _rms_scale(y)) * g_ref[...] for y in ys]
        for r, yn in zip(chunks, yns):
            for p in range(X_SLICES):
                o_ref[r, p * w:(p + 1) * w] = xs_ref[p, r, :] + yn[:, p * w:(p + 1) * w]
        return carry

    lax.fori_loop(0, rows // step, body, 0)


def _x_slice_spec(tm, d):
    return pl.BlockSpec((tm, d // X_SLICES), lambda i, k: (i, jnp.minimum(k, X_SLICES - 1)))


def _accumulate_then_residual_norm(a_ref, w_ref, x_ref, g_ref, o_ref, xs_ref, ragged):
    k = pl.program_id(1)
    last = pl.num_programs(1) - 1

    @pl.when(k == 0)
    def _():
        o_ref[...] = jnp.zeros(o_ref.shape, F32)

    @pl.when(k < X_SLICES)
    def _():
        xs_ref[k] = x_ref[...]

    if ragged is None:
        o_ref[...] += jnp.dot(a_ref[...], w_ref[...], preferred_element_type=F32)
    else:
        @pl.when(k < last)
        def _():
            o_ref[...] += jnp.dot(a_ref[...], w_ref[...], preferred_element_type=F32)

        @pl.when(k == last)
        def _():
            a, w = a_ref[...], w_ref[...]
            a = jnp.where(lax.broadcasted_iota(jnp.int32, a.shape, 1) < ragged, a, jnp.zeros_like(a))
            w = jnp.where(lax.broadcasted_iota(jnp.int32, w.shape, 0) < ragged, w, jnp.zeros_like(w))
            o_ref[...] += jnp.dot(a, w, preferred_element_type=F32)

    @pl.when(k == last)
    def _():
        _residual_norm(o_ref, xs_ref, g_ref)


OUT_TK = 512


def _out_proj_kernel(mla_ref, hg_ref, ca_ref, w_ref, x_ref, gm_ref, gp_ref, o_ref, a_ref, xs_ref):
    @pl.when(pl.program_id(1) == 0)
    def _():
        rows = mla_ref.shape[0]
        n_mla, n_hg, n_ca = MLA_WIDTH // OUT_TK, HG_WIDTH // OUT_TK, CA_WIDTH // OUT_TK

        def body(c, carry):
            r = pl.ds(pl.multiple_of(c * NORM_ROWS, NORM_ROWS), NORM_ROWS)
            m = mla_ref[r, :]
            mn = ((m * _rms_scale(m)) * gm_ref[...]).astype(BF16)
            for p in range(n_mla):
                a_ref[p, r, :] = mn[:, p * OUT_TK:(p + 1) * OUT_TK]
            return carry

        lax.fori_loop(0, rows // NORM_ROWS, body, 0, unroll=NORM_UNROLL)
        for p in range(n_hg):
            a_ref[n_mla + p] = hg_ref[:, p * OUT_TK:(p + 1) * OUT_TK]
        for p in range(n_ca):
            a_ref[n_mla + n_hg + p] = ca_ref[:, p * OUT_TK:(p + 1) * OUT_TK]

    _accumulate_then_residual_norm(a_ref.at[pl.program_id(1)], w_ref, x_ref, gp_ref, o_ref, xs_ref, None)


def _out_proj(o_mla, o_hg, o_ca, x, g_mla, g_post, w, layer, tm):
    t, d = x.shape
    kdim = w.shape[1]
    nk = kdim // OUT_TK
    assert MLA_WIDTH % OUT_TK == 0 and HG_WIDTH % OUT_TK == 0 and CA_WIDTH % OUT_TK == 0
    assert kdim == MLA_WIDTH + HG_WIDTH + CA_WIDTH and nk >= X_SLICES
    return pl.pallas_call(
        _out_proj_kernel,
        grid=(t // tm, nk),
        in_specs=[pl.BlockSpec((tm, MLA_WIDTH), lambda i, k: (i, 0)),
                  pl.BlockSpec((tm, HG_WIDTH), lambda i, k: (i, 0)),
                  pl.BlockSpec((tm, CA_WIDTH), lambda i, k: (i, 0)),
                  pl.BlockSpec((None, OUT_TK, d), lambda i, k: (layer, k, 0)),
                  _x_slice_spec(tm, d),
                  pl.BlockSpec((1, MLA_WIDTH), lambda i, k: (0, 0)),
                  pl.BlockSpec((1, d), lambda i, k: (0, 0))],
        out_specs=pl.BlockSpec((tm, d), lambda i, k: (i, 0)),
        out_shape=jax.ShapeDtypeStruct((t, d), F32),
        scratch_shapes=[pltpu.VMEM((nk, tm, OUT_TK), BF16),
                        pltpu.VMEM((X_SLICES, tm, d // X_SLICES), F32)],
        compiler_params=_cparams(("parallel", "arbitrary")),
        name="out_proj",
    )(o_mla, o_hg, o_ca, w, x, g_mla.reshape(1, -1), g_post.reshape(1, -1))


def _down_proj_kernel(a_ref, w_ref, x_ref, g_ref, o_ref, xs_ref, *, ragged):
    _accumulate_then_residual_norm(a_ref, w_ref, x_ref, g_ref, o_ref, xs_ref, ragged)


def _down_proj(a, w, x, g, layer, tm, tk):
    t, d = x.shape
    kdim = a.shape[1]
    ragged = kdim % tk or None
    nk = pl.cdiv(kdim, tk)
    assert nk >= X_SLICES
    return pl.pallas_call(
        functools.partial(_down_proj_kernel, ragged=ragged),
        grid=(t // tm, nk),
        in_specs=[pl.BlockSpec((tm, tk), lambda i, k: (i, k)),
                  pl.BlockSpec((None, tk, d), lambda i, k: (layer, k, 0)),
                  _x_slice_spec(tm, d),
                  pl.BlockSpec((1, d), lambda i, k: (0, 0))],
        out_specs=pl.BlockSpec((tm, d), lambda i, k: (i, 0)),
        out_shape=jax.ShapeDtypeStruct((t, d), F32),
        scratch_shapes=[pltpu.VMEM((X_SLICES, tm, d // X_SLICES), F32)],
        compiler_params=_cparams(("parallel", "arbitrary")),
        name="down_proj",
    )(a, w, x, g.reshape(1, -1))


def _prep_w_in(w):
    depth, d, _ = w.shape
    o_ckv = MLA_Q_RANK
    o_kr = o_ckv + MLA_KV_RANK
    o_hg = o_kr + MLA_ROPE
    o_ca = o_hg + 4 * HG_WIDTH
    assert (COL_AQ, COL_AK, COL_AV) == (0, CA_WIDTH, 2 * CA_WIDTH) and COL_HQ == 3 * CA_WIDTH
    assert (COL_HF, COL_HI, COL_HG) == (COL_HQ + HG_WIDTH, COL_HQ + 2 * HG_WIDTH, COL_HQ + 3 * HG_WIDTH)
    assert COL_CKV == COL_HG + HG_WIDTH and COL_CQ == COL_CKV + MLA_KV_RANK and COL_KR == COL_CQ + MLA_Q_RANK
    wb = w.astype(BF16)
    z = jnp.zeros((depth, d, D_IN_PAD - COL_KR - MLA_ROPE), BF16)
    out = jnp.concatenate([wb[:, :, o_ca:], wb[:, :, o_hg:o_ca], wb[:, :, o_ckv:o_kr], wb[:, :, :o_ckv],
                           wb[:, :, o_kr:o_hg], z], axis=2)
    assert out.shape[2] == D_IN_PAD
    return out


def _prep_w_uq(w):
    depth, r, _ = w.shape
    w4 = w.astype(BF16).reshape(depth, r, MLA_HEADS, MLA_NOPE + MLA_ROPE)
    pad = jnp.zeros((depth, r, MLA_HEADS, MLA_QK_PAD - MLA_NOPE - MLA_ROPE), BF16)
    return jnp.concatenate([w4, pad], axis=3).reshape(depth, r, MLA_HEADS * MLA_QK_PAD)


def _prep_w_ukv(w):
    depth, r, _ = w.shape
    w4 = w.astype(BF16).reshape(depth, r, MLA_HEADS, MLA_NOPE + MLA_V)
    wk = w4[:, :, :, :MLA_NOPE].reshape(depth, r, MLA_HEADS * MLA_NOPE)
    wv = w4[:, :, :, MLA_NOPE:].reshape(depth, r, MLA_HEADS * MLA_V)
    return wk, jnp.swapaxes(wv, 1, 2)


def kernel(x, positions, attn_pre_norm, attn_post_norm, w_in, mla_q_norm, mla_kv_norm, w_uq, w_ukv,
           mla_out_norm, hg_lower_bounds, hg_out_norm, ca_rel_bias, ca_out_norm, w_out, ffn_pre_norm,
           ffn_post_norm, w_gate, w_up, w_down):
    batch, seq, d = x.shape
    t = batch * seq
    depth = w_in.shape[0]
    xf = x.reshape(t, d)
    pos = positions.reshape(t, 1)
    w_in_b = _prep_w_in(w_in)
    w_uq_b = _prep_w_uq(w_uq)
    w_uk_b, w_uvt_b = _prep_w_ukv(w_ukv)
    w_out_b, w_gate_b, w_up_b, w_down_b = (w.astype(BF16) for w in (w_out, w_gate, w_up, w_down))
    for l in range(depth):
        h = _norm_matmul(xf, attn_pre_norm[l], [w_in_b], l, F32, tm=1024, tn=512)
        q, k, vt = _mla_proj(h, pos, mla_q_norm[l], mla_kv_norm[l], w_uq_b, w_uk_b, w_uvt_b, l, tm=MLA_TK)
        o_mla = _mla_attn(q, k, vt, batch, seq)
        o_hg = _hgrn(h, hg_lower_bounds, hg_out_norm[l], l, batch, seq)
        o_ca = _chunk_attn(h, ca_rel_bias[l], ca_out_norm[l], batch, seq)
        xf = _out_proj(o_mla, o_hg, o_ca, xf, mla_out_norm[l], attn_post_norm[l], w_out_b, l, tm=512)
        hid = _norm_matmul(xf, ffn_pre_norm[l], [w_gate_b, w_up_b], l, BF16, tm=1024, tn=512)
        xf = _down_proj(hid, w_down_b, xf, ffn_post_norm[l], l, tm=512, tk=1024)
    return xf.reshape(batch, seq, d)
```

```python
import functools
import math

import jax
import jax.numpy as jnp
from jax import lax
from jax.experimental import pallas as pl
from jax.experimental.pallas import tpu as pltpu

F32 = jnp.float32
BF16 = jnp.bfloat16

EPS = 1e-6
MASK_VALUE = -1e30
TINY = 1e-30
CHUNK = 64

MLA_HEADS = 16
MLA_Q_RANK = 768
MLA_KV_RANK = 512
MLA_NOPE = 128
MLA_ROPE = 64
MLA_V = 128
ROPE_THETA = 10000.0
MLA_QK_PAD = 256
MLA_ONES = 16
MLA_VT_ROWS = MLA_V + MLA_ONES
MLA_EXP_SCALE = (MLA_NOPE + MLA_ROPE) ** -0.5 * math.log2(math.e)

HG_HEADS = 8
HG_DIM = 128
HG_BLOCK = 16
HG_CHUNK = 128
HG_ROWS = 512

CA_HEADS = 8
CA_DIM = 128
CA_LEFT_CHUNKS = 8
CA_REL_CLIP = 256
CA_TQ = 256
CA_WIN = 3 * CA_TQ
CA_BIAS_LEN = 1024

MLA_WIDTH = MLA_HEADS * MLA_V
HG_WIDTH = HG_HEADS * HG_DIM
CA_WIDTH = CA_HEADS * CA_DIM

COL_AQ = 0
COL_AK = 1024
COL_AV = 2048
COL_HQ = 3072
COL_HF = 4096
COL_HI = 5120
COL_HG = 6144
COL_CKV = 7168
COL_CQ = 7680
COL_KR = 8448
D_IN_PAD = 8704

LANE = 128
VMEM_LIMIT = 56 * 1024 * 1024


def _cparams(sem, vmem=VMEM_LIMIT):
    return pltpu.CompilerParams(dimension_semantics=sem, vmem_limit_bytes=vmem)


def _rms_scale(x):
    return lax.rsqrt(jnp.mean(x * x, axis=-1, keepdims=True) + EPS)


NORM_ROWS = 16
NORM_UNROLL = 4


def _norm_rows_into(xn_ref, x_ref, g_ref):
    rows = x_ref.shape[0]

    def body(c, carry):
        r = pl.ds(pl.multiple_of(c * NORM_ROWS, NORM_ROWS), NORM_ROWS)
        x = x_ref[r, :]
        xn_ref[r, :] = ((x * _rms_scale(x)) * g_ref[...]).astype(BF16)
        return carry

    lax.fori_loop(0, rows // NORM_ROWS, body, 0, unroll=NORM_UNROLL)


def _norm_matmul_kernel(x_ref, g_ref, w_ref, o_ref, xn_ref):
    @pl.when(pl.program_id(1) == 0)
    def _():
        _norm_rows_into(xn_ref, x_ref, g_ref)

    o_ref[...] = jnp.dot(xn_ref[...], w_ref[...], preferred_element_type=F32).astype(o_ref.dtype)


def _norm_swiglu_kernel(x_ref, g_ref, wg_ref, wu_ref, o_ref, xn_ref):
    @pl.when(pl.program_id(1) == 0)
    def _():
        _norm_rows_into(xn_ref, x_ref, g_ref)

    xn = xn_ref[...]
    gate = jnp.dot(xn, wg_ref[...], preferred_element_type=F32)
    up = jnp.dot(xn, wu_ref[...], preferred_element_type=F32)
    o_ref[...] = ((gate * jax.nn.sigmoid(gate)) * up).astype(o_ref.dtype)


def _norm_matmul(x, g, ws, layer, out_dtype, tm, tn):
    t, d = x.shape
    n = ws[0].shape[2]
    kern = _norm_matmul_kernel if len(ws) == 1 else _norm_swiglu_kernel
    w_specs = [pl.BlockSpec((None, d, tn), lambda i, j: (layer, 0, j)) for _ in ws]
    return pl.pallas_call(
        kern,
        grid=(t // tm, pl.cdiv(n, tn)),
        in_specs=[pl.BlockSpec((tm, d), lambda i, j: (i, 0), pipeline_mode=pl.Buffered(1)),
                  pl.BlockSpec((1, d), lambda i, j: (0, 0))] + w_specs,
        out_specs=pl.BlockSpec((tm, tn), lambda i, j: (i, j)),
        out_shape=jax.ShapeDtypeStruct((t, n), out_dtype),
        scratch_shapes=[pltpu.VMEM((tm, d), BF16)],
        compiler_params=_cparams(("parallel", "arbitrary")),
        name="norm_matmul" if len(ws) == 1 else "norm_swiglu",
    )(x, g.reshape(1, d), *ws)


def _rope_tables(pos_ref):
    lane = lax.broadcasted_iota(jnp.int32, (1, LANE), 1)
    half = MLA_ROPE // 2
    idx = (lane % half).astype(F32)
    inv_freq = jnp.exp((-math.log(ROPE_THETA) * 2.0) * idx / MLA_ROPE)
    inv_freq = jnp.where(lane < MLA_ROPE, inv_freq, 0.0)
    ang = pos_ref[...].astype(F32) * inv_freq
    cos, sin = jnp.cos(ang), jnp.sin(ang)
    sin_hi = jnp.where((lane >= half) & (lane < MLA_ROPE), sin, 0.0)
    sin_lo = jnp.where(lane < half, -sin, 0.0)
    return cos, sin_hi, sin_lo


def _rope(x, tables):
    cos, sin_hi, sin_lo = tables
    half = MLA_ROPE // 2
    return x * cos + pltpu.roll(x, half, 1) * sin_hi + pltpu.roll(x, LANE - half, 1) * sin_lo


def _mla_proj_kernel(cq_ref, ckv_ref, kr_ref, pos_ref, gq_ref, gkv_ref, wq_ref, wk_ref, wvt_ref,
                     q_ref, k_ref, vt_ref):
    tables = _rope_tables(pos_ref)
    cq = cq_ref[...]
    cqn = ((cq * _rms_scale(cq)) * gq_ref[...]).astype(BF16)
    ckv = ckv_ref[...]
    ckvn = ((ckv * _rms_scale(ckv)) * gkv_ref[...]).astype(BF16)
    k_pe = _rope(kr_ref[...], tables).astype(BF16)
    for h in range(MLA_HEADS):
        c0 = h * MLA_QK_PAD
        qh = jnp.dot(cqn, wq_ref[:, c0:c0 + MLA_QK_PAD], preferred_element_type=F32) * MLA_EXP_SCALE
        q_ref[:, c0:c0 + LANE] = qh[:, :LANE].astype(BF16)
        q_ref[:, c0 + LANE:c0 + MLA_QK_PAD] = _rope(qh[:, LANE:], tables).astype(BF16)
        kh = jnp.dot(ckvn, wk_ref[:, h * LANE:(h + 1) * LANE], preferred_element_type=F32)
        k_ref[:, c0:c0 + LANE] = kh.astype(BF16)
        k_ref[:, c0 + LANE:c0 + MLA_QK_PAD] = k_pe
    group = 4
    ones = jnp.ones((MLA_ONES, vt_ref.shape[2]), BF16)
    for h0 in range(0, MLA_HEADS, group):
        vt = lax.dot_general(wvt_ref[h0 * MLA_V:(h0 + group) * MLA_V, :], ckvn, (((1,), (1,)), ((), ())),
                             preferred_element_type=F32).astype(BF16)
        for h in range(h0, h0 + group):
            r0 = h * MLA_VT_ROWS
            vt_ref[0, r0:r0 + MLA_V, :] = vt[(h - h0) * MLA_V:(h - h0 + 1) * MLA_V, :]
            vt_ref[0, r0 + MLA_V:r0 + MLA_VT_ROWS, :] = ones


def _mla_proj(h, positions, gq, gkv, wq, wk, wvt, layer, tm):
    t = h.shape[0]
    qk_w = MLA_HEADS * MLA_QK_PAD
    const = lambda i: (0, 0)
    return pl.pallas_call(
        _mla_proj_kernel,
        grid=(t // tm,),
        in_specs=[pl.BlockSpec((tm, MLA_Q_RANK), lambda i: (i, COL_CQ // MLA_Q_RANK)),
                  pl.BlockSpec((tm, MLA_KV_RANK), lambda i: (i, COL_CKV // MLA_KV_RANK)),
                  pl.BlockSpec((tm, LANE), lambda i: (i, COL_KR // LANE)),
                  pl.BlockSpec((tm, 1), lambda i: (i, 0)),
                  pl.BlockSpec((1, MLA_Q_RANK), const),
                  pl.BlockSpec((1, MLA_KV_RANK), const),
                  pl.BlockSpec((None,) + wq.shape[1:], lambda i: (layer, 0, 0)),
                  pl.BlockSpec((None,) + wk.shape[1:], lambda i: (layer, 0, 0)),
                  pl.BlockSpec((None,) + wvt.shape[1:], lambda i: (layer, 0, 0))],
        out_specs=[pl.BlockSpec((tm, qk_w), lambda i: (i, 0)),
                   pl.BlockSpec((tm, qk_w), lambda i: (i, 0)),
                   pl.BlockSpec((1, MLA_HEADS * MLA_VT_ROWS, tm), lambda i: (i, 0, 0))],
        out_shape=[jax.ShapeDtypeStruct((t, qk_w), BF16),
                   jax.ShapeDtypeStruct((t, qk_w), BF16),
                   jax.ShapeDtypeStruct((t // tm, MLA_HEADS * MLA_VT_ROWS, tm), BF16)],
        compiler_params=_cparams(("parallel",)),
        name="mla_proj",
    )(h, h, h, positions, gq.reshape(1, -1), gkv.reshape(1, -1), wq, wk, wvt)


MLA_TQ = 2048
MLA_TK = 512
MLA_QBLK = 256
MLA_KBLK = 128


def _mla_attn_kernel(q_ref, k_ref, vt_ref, o_ref, qt_ref, sa_ref, sb_ref, xa_ref, xb_ref, pa_ref, pb_ref,
                     m_ref, alpha_ref, acc_ref):
    i = pl.program_id(2)
    tq, tk = MLA_TQ, MLA_TK
    qt_ref[...] = q_ref[...].astype(F32).T.astype(BF16)
    m_ref[...] = jnp.full(m_ref.shape, -jnp.inf, F32)
    alpha_ref[...] = jnp.ones(alpha_ref.shape, F32)
    acc_ref[...] = jnp.zeros(acc_ref.shape, F32)
    pb_ref[...] = jnp.zeros(pb_ref.shape, BF16)

    def scores_into(s_ref, x_ref, t, q_lo=0):
        r = pl.ds(pl.multiple_of(t * tk, tk), tk)
        s = jnp.dot(k_ref[r, :], qt_ref[:, q_lo:], preferred_element_type=F32)
        s_ref[:, q_lo:] = s
        x_ref[:, q_lo:] = jnp.max(s, axis=0, keepdims=True)

    def add_values(p_ref, vt_tile, q_lo=0):
        acc_ref[:, q_lo:] = alpha_ref[:, q_lo:] * acc_ref[:, q_lo:] + jnp.dot(
            vt_tile, p_ref[:, q_lo:], preferred_element_type=F32)

    def softmax_terms(s_ref, x_ref, p_ref, chunk_shift, q_lo=0):
        for q0 in range(q_lo, tq, MLA_QBLK):
            cols = slice(q0, q0 + MLA_QBLK)

            def masked(s, k0):
                kc = (lax.broadcasted_iota(jnp.int32, s.shape, 0) + k0) // CHUNK + chunk_shift
                qc = (lax.broadcasted_iota(jnp.int32, s.shape, 1) + q0) // CHUNK
                return jnp.where(kc <= qc, s, MASK_VALUE)

            if chunk_shift is None:
                tile_max = x_ref[:, cols]
            else:
                tile_max = jnp.max(masked(s_ref[:, cols], 0), axis=0, keepdims=True)
            m_old = m_ref[:, cols]
            m_new = jnp.maximum(m_old, tile_max)
            m_ref[:, cols] = m_new
            alpha_ref[:, cols] = jnp.exp2(m_old - m_new)
            for k0 in range(0, tk, MLA_KBLK):
                rows = slice(k0, k0 + MLA_KBLK)
                s = s_ref[rows, cols]
                if chunk_shift is not None:
                    s = masked(s, k0)
                p_ref[rows, cols] = jnp.exp2(s - m_new).astype(BF16)

    nt = tq // tk
    first_diag = nt * i

    def pair(u, d):
        def lo(dd):
            return 0 if dd is None or dd < 0 else dd * tk

        def shift(dd):
            return None if dd is None else dd * (tk // CHUNK)

        d_odd = None if d is None else d + 1
        d_prev = None if d is None else d - 1
        scores_into(sb_ref, xb_ref, 2 * u + 1, lo(d_odd))
        add_values(pb_ref, vt_ref[jnp.maximum(2 * u - 1, 0)], lo(d_prev))
        softmax_terms(sa_ref, xa_ref, pa_ref, shift(d), lo(d))
        if d is None:
            scores_into(sa_ref, xa_ref, 2 * u + 2)
        elif d + 2 < nt:
            scores_into(sa_ref, xa_ref, 2 * u + 2, lo(d + 2))
        add_values(pa_ref, vt_ref[2 * u], lo(d))
        softmax_terms(sb_ref, xb_ref, pb_ref, shift(d_odd), lo(d_odd))

    scores_into(sa_ref, xa_ref, 0)

    def body(u, carry):
        pair(u, None)
        return carry

    lax.fori_loop(0, first_diag // 2, body, 0)
    for d in range(0, nt, 2):
        pair((first_diag + d) // 2, d)
    add_values(pb_ref, vt_ref[first_diag + nt - 1], (nt - 1) * tk)
    o_ref[...] = (acc_ref[:MLA_V, :] / acc_ref[MLA_V:MLA_V + 1, :]).T


def _mla_attn(q, k, vt, batch, seq):
    tq = MLA_TQ
    nq = seq // tq
    assert vt.shape[2] == MLA_TK
    return pl.pallas_call(
        _mla_attn_kernel,
        grid=(batch, MLA_HEADS, nq),
        in_specs=[pl.BlockSpec((tq, MLA_QK_PAD), lambda b, h, i: (b * nq + i, h)),
                  pl.BlockSpec((seq, MLA_QK_PAD), lambda b, h, i: (b, h)),
                  pl.BlockSpec((seq // MLA_TK, MLA_VT_ROWS, MLA_TK), lambda b, h, i: (b, h, 0))],
        out_specs=pl.BlockSpec((tq, MLA_V), lambda b, h, i: (b * nq + i, h)),
        out_shape=jax.ShapeDtypeStruct((batch * seq, MLA_WIDTH), F32),
        scratch_shapes=[pltpu.VMEM((MLA_QK_PAD, tq), BF16),
                        pltpu.VMEM((MLA_TK, tq), F32), pltpu.VMEM((MLA_TK, tq), F32),
                        pltpu.VMEM((1, tq), F32), pltpu.VMEM((1, tq), F32),
                        pltpu.VMEM((MLA_TK, tq), BF16), pltpu.VMEM((MLA_TK, tq), BF16),
                        pltpu.VMEM((1, tq), F32), pltpu.VMEM((1, tq), F32),
                        pltpu.VMEM((MLA_VT_ROWS, tq), F32)],
        compiler_params=_cparams(("parallel", "parallel", "arbitrary")),
        name="mla_attn",
    )(q, k, vt)


def _block_cumsum(x, row):
    r = row % HG_BLOCK
    s = 1
    while s < HG_BLOCK:
        x = x + jnp.where(r >= s, pltpu.roll(x, s, 0), 0.0)
        s *= 2
    return x


def _hgrn_chunk(hq, hf, hi, lb, state):
    c = HG_CHUNK
    nb = c // HG_BLOCK
    row = lax.broadcasted_iota(jnp.int32, (c, HG_DIM), 0)
    col = lax.broadcasted_iota(jnp.int32, (c, HG_DIM), 1)

    q = hq * jax.nn.sigmoid(hq)
    f = lb + (1.0 - lb) * jax.nn.sigmoid(hf)
    k = (1.0 - lb) * jax.nn.sigmoid(-hf)
    b = _block_cumsum(jnp.log(jnp.maximum(f, TINY)), row)
    b3 = b.reshape(nb, HG_BLOCK, HG_DIM)
    b_last3 = jnp.broadcast_to(b3[:, HG_BLOCK - 1:HG_BLOCK, :], b3.shape)
    b_last = b_last3.reshape(c, HG_DIM)

    q3 = q.reshape(nb, HG_BLOCK, HG_DIM)
    k3 = k.reshape(nb, HG_BLOCK, HG_DIM)
    v3 = hi.reshape(nb, HG_BLOCK, HG_DIM)
    irow = lax.broadcasted_iota(jnp.int32, b3.shape, 1)
    sub = 8
    ws = []
    for j in range(HG_BLOCK):
        lo = (j // sub) * sub
        w = q3[:, lo:, :] * k3[:, j:j + 1, :] * jnp.exp(b3[:, lo:, :] - b3[:, j:j + 1, :])
        if j > lo:
            w = jnp.where(irow[:, lo:, :] >= j, w, 0.0)
        ws.append(w.reshape(nb * (HG_BLOCK - lo), HG_DIM))
    sums = jnp.dot(jnp.concatenate(ws, axis=0).astype(BF16), jnp.ones((HG_DIM, HG_DIM), BF16),
                   preferred_element_type=F32)
    o_parts = [jnp.zeros((nb, sub, HG_DIM), F32) for _ in range(HG_BLOCK // sub)]
    r_at = 0
    for j in range(HG_BLOCK):
        lo = (j // sub) * sub
        n_rows = nb * (HG_BLOCK - lo)
        a_j = sums[r_at:r_at + n_rows, :].reshape(nb, HG_BLOCK - lo, HG_DIM)
        r_at += n_rows
        wv = a_j * v3[:, j:j + 1, :]
        for part in range(lo // sub, HG_BLOCK // sub):
            r0 = part * sub - lo
            o_parts[part] = o_parts[part] + wv[:, r0:r0 + sub, :]
    o = jnp.concatenate(o_parts, axis=1).reshape(c, HG_DIM)

    q_dec = q * jnp.exp(b)
    k_dec_t = (k * jnp.exp(b_last - b)).T
    b_last_t = b_last.T
    v_bf = hi.astype(BF16)
    blk_of_col = col // HG_BLOCK
    k_stack = jnp.concatenate(
        [jnp.where(blk_of_col == j, k_dec_t, 0.0) for j in range(nb)], axis=0).astype(BF16)
    u_all = jnp.dot(k_stack, v_bf, preferred_element_type=F32)
    states = []
    for j in range(nb):
        states.append(state.astype(BF16))
        decay = jnp.exp(b_last_t[:, j * HG_BLOCK:j * HG_BLOCK + 1])
        state = decay * state + u_all[j * HG_DIM:(j + 1) * HG_DIM, :]
    s_stack = jnp.concatenate(states, axis=0)
    blk_of_row = row // HG_BLOCK
    q_exp = jnp.concatenate(
        [jnp.where(blk_of_row == j, q_dec, 0.0) for j in range(nb)], axis=1).astype(BF16)
    o = o + jnp.dot(q_exp, s_stack, preferred_element_type=F32)
    return o, state


def _hgrn_kernel(hq_ref, hf_ref, hi_ref, hg_ref, lbraw_ref, gn_ref, o_ref, state_ref, *, layer):
    @pl.when(pl.program_id(2) == 0)
    def _():
        state_ref[...] = jnp.zeros(state_ref.shape, F32)

    raw = lbraw_ref[...]
    e = jnp.exp(raw - jnp.max(raw, axis=0, keepdims=True))
    p = e / jnp.sum(e, axis=0, keepdims=True)
    lb = jnp.sum(p[:layer + 1, :], axis=0, keepdims=True) - p[0:1, :]

    def body(ci, carry):
        r = pl.ds(pl.multiple_of(ci * HG_CHUNK, HG_CHUNK), HG_CHUNK)
        o, state = _hgrn_chunk(hq_ref[r, :], hf_ref[r, :], hi_ref[r, :], lb, state_ref[...])
        state_ref[...] = state
        o = (o * _rms_scale(o)) * gn_ref[...]
        g = hg_ref[r, :]
        o_ref[r, :] = (o * (g * jax.nn.sigmoid(g))).astype(o_ref.dtype)
        return carry

    lax.fori_loop(0, hq_ref.shape[0] // HG_CHUNK, body, 0, unroll=2)


def _hgrn(h, lb_raw, gn, layer, batch, seq):
    rows = HG_ROWS
    nr = seq // rows
    depth = lb_raw.shape[0]

    def col_spec(col0):
        return pl.BlockSpec((rows, HG_DIM), lambda b, hh, c: (b * nr + c, col0 // HG_DIM + hh))

    return pl.pallas_call(
        functools.partial(_hgrn_kernel, layer=layer),
        grid=(batch, HG_HEADS, nr),
        in_specs=[col_spec(COL_HQ), col_spec(COL_HF), col_spec(COL_HI), col_spec(COL_HG),
                  pl.BlockSpec((depth, HG_DIM), lambda b, hh, c: (0, hh)),
                  pl.BlockSpec((1, HG_DIM), lambda b, hh, c: (0, hh))],
        out_specs=pl.BlockSpec((rows, HG_DIM), lambda b, hh, c: (b * nr + c, hh)),
        out_shape=jax.ShapeDtypeStruct((batch * seq, HG_WIDTH), BF16),
        scratch_shapes=[pltpu.VMEM((HG_DIM, HG_DIM), F32)],
        compiler_params=_cparams(("parallel", "parallel", "arbitrary")),
        name="hgrn",
    )(h, h, h, h, lb_raw, gn.reshape(1, -1))


def _ca_bias_rows(rel_bias):
    idx = jnp.arange(CA_BIAS_LEN)
    m = jnp.where(idx < CA_WIN, idx, idx - CA_BIAS_LEN)
    bucket = jnp.clip(2 * CA_TQ - m, -CA_REL_CLIP, CA_REL_CLIP) + CA_REL_CLIP
    return rel_bias[:, bucket].astype(F32)


def _ca_kernel(q_ref, k0_ref, k1_ref, k2_ref, v0_ref, v1_ref, v2_ref, brow_ref, gn_ref,
               o_ref, bias_ref):
    t = pl.program_id(1)
    tq = CA_TQ

    @pl.when((pl.program_id(0) == 0) & (t == 0))
    def _():
        qc = lax.broadcasted_iota(jnp.int32, (tq, CA_WIN), 0) // CHUNK
        kc = lax.broadcasted_iota(jnp.int32, (tq, CA_WIN), 1) // CHUNK
        band = (kc >= qc) & (kc <= qc + CA_LEFT_CHUNKS)
        for h in range(CA_HEADS):
            rows = jnp.broadcast_to(brow_ref[h:h + 1, :], (tq, CA_BIAS_LEN))
            rolled = pltpu.roll(rows, 0, 1, stride=1, stride_axis=0)
            bias_ref[h] = jnp.where(band, rolled[:, :CA_WIN], MASK_VALUE)

    first_valid = jnp.maximum(2 * tq - t * tq, 0)
    kk = lax.broadcasted_iota(jnp.int32, (tq, CA_WIN), 1)
    seq_ok = kk >= first_valid
    scale = CA_DIM ** -0.5
    outs = []
    for h in range(CA_HEADS):
        c = slice(h * CA_DIM, (h + 1) * CA_DIM)
        qh = q_ref[:, c].astype(BF16)
        kh = jnp.concatenate([k0_ref[:, c], k1_ref[:, c], k2_ref[:, c]], axis=0).astype(BF16)
        vh = jnp.concatenate([v0_ref[:, c], v1_ref[:, c], v2_ref[:, c]], axis=0).astype(BF16)
        s = lax.dot_general(qh, kh, (((1,), (1,)), ((), ())), preferred_element_type=F32) * scale
        s = jnp.where(seq_ok, s + bias_ref[h], MASK_VALUE)
        p = jnp.exp(s - jnp.max(s, axis=-1, keepdims=True))
        l = jnp.sum(p, axis=-1, keepdims=True)
        outs.append(jnp.dot(p.astype(BF16), vh, preferred_element_type=F32) / l)
    o = jnp.concatenate(outs, axis=1)
    o_ref[...] = ((o * _rms_scale(o)) * gn_ref[...]).astype(o_ref.dtype)


def _chunk_attn(h, rel_bias, gn, batch, seq):
    tq = CA_TQ
    nq = seq // tq
    cq, ck, cv = COL_AQ // CA_WIDTH, COL_AK // CA_WIDTH, COL_AV // CA_WIDTH

    def kv_spec(col, back):
        return pl.BlockSpec((tq, CA_WIDTH), lambda b, t: (b * nq + jnp.maximum(t - back, 0), col))

    return pl.pallas_call(
        _ca_kernel,
        grid=(batch, nq),
        in_specs=[pl.BlockSpec((tq, CA_WIDTH), lambda b, t: (b * nq + t, cq)),
                  kv_spec(ck, 2), kv_spec(ck, 1), kv_spec(ck, 0),
                  kv_spec(cv, 2), kv_spec(cv, 1), kv_spec(cv, 0),
                  pl.BlockSpec((CA_HEADS, CA_BIAS_LEN), lambda b, t: (0, 0)),
                  pl.BlockSpec((1, CA_WIDTH), lambda b, t: (0, 0))],
        out_specs=pl.BlockSpec((tq, CA_WIDTH), lambda b, t: (b * nq + t, 0)),
        out_shape=jax.ShapeDtypeStruct((batch * seq, CA_WIDTH), BF16),
        scratch_shapes=[pltpu.VMEM((CA_HEADS, tq, CA_WIN), F32)],
        compiler_params=_cparams(("arbitrary", "arbitrary")),
        name="chunk_attn",
    )(h, h, h, h, h, h, h, _ca_bias_rows(rel_bias), gn.reshape(1, -1))


X_SLICES = 8


def _residual_norm(o_ref, xs_ref, g_ref):
    rows, d = o_ref.shape
    w = d // X_SLICES

    step = NORM_ROWS * NORM_UNROLL

    def body(c, carry):
        base = pl.multiple_of(c * step, step)
        chunks = [pl.ds(base + u * NORM_ROWS, NORM_ROWS) for u in range(NORM_UNROLL)]
        ys = [o_ref[r, :] for r in chunks]
        yns = [(y * _rms_scale(y)) * g_ref[...] for y in ys]
        for r, yn in zip(chunks, yns):
            for p in range(X_SLICES):
                o_ref[r, p * w:(p + 1) * w] = xs_ref[p, r, :] + yn[:, p * w:(p + 1) * w]
        return carry

    lax.fori_loop(0, rows // step, body, 0)


def _x_slice_spec(tm, d):
    return pl.BlockSpec((tm, d // X_SLICES), lambda i, k: (i, jnp.minimum(k, X_SLICES - 1)))


def _accumulate_then_residual_norm(a_ref, w_ref, x_ref, g_ref, o_ref, xs_ref, ragged):
    k = pl.program_id(1)
    last = pl.num_programs(1) - 1

    @pl.when(k < X_SLICES)
    def _():
        xs_ref[k] = x_ref[...]

    @pl.when(k == 0)
    def _():
        o_ref[...] = jnp.dot(a_ref[...], w_ref[...], preferred_element_type=F32)

    if ragged is None:
        @pl.when(k > 0)
        def _():
            o_ref[...] += jnp.dot(a_ref[...], w_ref[...], preferred_element_type=F32)
    else:
        @pl.when((k > 0) & (k < last))
        def _():
            o_ref[...] += jnp.dot(a_ref[...], w_ref[...], preferred_element_type=F32)

        @pl.when(k == last)
        def _():
            a, w = a_ref[...], w_ref[...]
            a = jnp.where(lax.broadcasted_iota(jnp.int32, a.shape, 1) < ragged, a, jnp.zeros_like(a))
            w = jnp.where(lax.broadcasted_iota(jnp.int32, w.shape, 0) < ragged, w, jnp.zeros_like(w))
            o_ref[...] += jnp.dot(a, w, preferred_element_type=F32)

    @pl.when(k == last)
    def _():
        _residual_norm(o_ref, xs_ref, g_ref)


OUT_TK = 512


def _out_proj_kernel(mla_ref, hg_ref, ca_ref, w_ref, x_ref, gm_ref, gp_ref, o_ref, a_ref, xs_ref):
    @pl.when(pl.program_id(1) == 0)
    def _():
        rows = mla_ref.shape[0]
        n_mla, n_hg, n_ca = MLA_WIDTH // OUT_TK, HG_WIDTH // OUT_TK, CA_WIDTH // OUT_TK

        def body(c, carry):
            r = pl.ds(pl.multiple_of(c * NORM_ROWS, NORM_ROWS), NORM_ROWS)
            m = mla_ref[r, :]
            mn = ((m * _rms_scale(m)) * gm_ref[...]).astype(BF16)
            for p in range(n_mla):
                a_ref[p, r, :] = mn[:, p * OUT_TK:(p + 1) * OUT_TK]
            return carry

        lax.fori_loop(0, rows // NORM_ROWS, body, 0, unroll=NORM_UNROLL)
        for p in range(n_hg):
            a_ref[n_mla + p] = hg_ref[:, p * OUT_TK:(p + 1) * OUT_TK]
        for p in range(n_ca):
            a_ref[n_mla + n_hg + p] = ca_ref[:, p * OUT_TK:(p + 1) * OUT_TK]

    _accumulate_then_residual_norm(a_ref.at[pl.program_id(1)], w_ref, x_ref, gp_ref, o_ref, xs_ref, None)


def _out_proj(o_mla, o_hg, o_ca, x, g_mla, g_post, w, layer, tm):
    t, d = x.shape
    kdim = w.shape[1]
    nk = kdim // OUT_TK
    assert MLA_WIDTH % OUT_TK == 0 and HG_WIDTH % OUT_TK == 0 and CA_WIDTH % OUT_TK == 0
    assert kdim == MLA_WIDTH + HG_WIDTH + CA_WIDTH and nk >= X_SLICES
    return pl.pallas_call(
        _out_proj_kernel,
        grid=(t // tm, nk),
        in_specs=[pl.BlockSpec((tm, MLA_WIDTH), lambda i, k: (i, 0)),
                  pl.BlockSpec((tm, HG_WIDTH), lambda i, k: (i, 0)),
                  pl.BlockSpec((tm, CA_WIDTH), lambda i, k: (i, 0)),
                  pl.BlockSpec((None, OUT_TK, d), lambda i, k: (layer, k, 0)),
                  _x_slice_spec(tm, d),
                  pl.BlockSpec((1, MLA_WIDTH), lambda i, k: (0, 0)),
                  pl.BlockSpec((1, d), lambda i, k: (0, 0))],
        out_specs=pl.BlockSpec((tm, d), lambda i, k: (i, 0)),
        out_shape=jax.ShapeDtypeStruct((t, d), F32),
        scratch_shapes=[pltpu.VMEM((nk, tm, OUT_TK), BF16),
                        pltpu.VMEM((X_SLICES, tm, d // X_SLICES), F32)],
        compiler_params=_cparams(("parallel", "arbitrary")),
        name="out_proj",
    )(o_mla, o_hg, o_ca, w, x, g_mla.reshape(1, -1), g_post.reshape(1, -1))


def _down_proj_kernel(a_ref, w_ref, x_ref, g_ref, o_ref, xs_ref, *, ragged):
    _accumulate_then_residual_norm(a_ref, w_ref, x_ref, g_ref, o_ref, xs_ref, ragged)


def _down_proj(a, w, x, g, layer, tm, tk):
    t, d = x.shape
    kdim = a.shape[1]
    ragged = kdim % tk or None
    nk = pl.cdiv(kdim, tk)
    assert nk >= X_SLICES
    return pl.pallas_call(
        functools.partial(_down_proj_kernel, ragged=ragged),
        grid=(t // tm, nk),
        in_specs=[pl.BlockSpec((tm, tk), lambda i, k: (i, k)),
                  pl.BlockSpec((None, tk, d), lambda i, k: (layer, k, 0)),
                  _x_slice_spec(tm, d),
                  pl.BlockSpec((1, d), lambda i, k: (0, 0))],
        out_specs=pl.BlockSpec((tm, d), lambda i, k: (i, 0)),
        out_shape=jax.ShapeDtypeStruct((t, d), F32),
        scratch_shapes=[pltpu.VMEM((X_SLICES, tm, d // X_SLICES), F32)],
        compiler_params=_cparams(("parallel", "arbitrary")),
        name="down_proj",
    )(a, w, x, g.reshape(1, -1))


def _prep_w_in(w):
    depth, d, _ = w.shape
    o_ckv = MLA_Q_RANK
    o_kr = o_ckv + MLA_KV_RANK
    o_hg = o_kr + MLA_ROPE
    o_ca = o_hg + 4 * HG_WIDTH
    assert (COL_AQ, COL_AK, COL_AV) == (0, CA_WIDTH, 2 * CA_WIDTH) and COL_HQ == 3 * CA_WIDTH
    assert (COL_HF, COL_HI, COL_HG) == (COL_HQ + HG_WIDTH, COL_HQ + 2 * HG_WIDTH, COL_HQ + 3 * HG_WIDTH)
    assert COL_CKV == COL_HG + HG_WIDTH and COL_CQ == COL_CKV + MLA_KV_RANK and COL_KR == COL_CQ + MLA_Q_RANK
    wb = w.astype(BF16)
    z = jnp.zeros((depth, d, D_IN_PAD - COL_KR - MLA_ROPE), BF16)
    out = jnp.concatenate([wb[:, :, o_ca:], wb[:, :, o_hg:o_ca], wb[:, :, o_ckv:o_kr], wb[:, :, :o_ckv],
                           wb[:, :, o_kr:o_hg], z], axis=2)
    assert out.shape[2] == D_IN_PAD
    return out


def _prep_w_uq(w):
    depth, r, _ = w.shape
    w4 = w.astype(BF16).reshape(depth, r, MLA_HEADS, MLA_NOPE + MLA_ROPE)
    pad = jnp.zeros((depth, r, MLA_HEADS, MLA_QK_PAD - MLA_NOPE - MLA_ROPE), BF16)
    return jnp.concatenate([w4, pad], axis=3).reshape(depth, r, MLA_HEADS * MLA_QK_PAD)


def _prep_w_ukv(w):
    depth, r, _ = w.shape
    w4 = w.astype(BF16).reshape(depth, r, MLA_HEADS, MLA_NOPE + MLA_V)
    wk = w4[:, :, :, :MLA_NOPE].reshape(depth, r, MLA_HEADS * MLA_NOPE)
    wv = w4[:, :, :, MLA_NOPE:].reshape(depth, r, MLA_HEADS * MLA_V)
    return wk, jnp.swapaxes(wv, 1, 2)


def kernel(x, positions, attn_pre_norm, attn_post_norm, w_in, mla_q_norm, mla_kv_norm, w_uq, w_ukv,
           mla_out_norm, hg_lower_bounds, hg_out_norm, ca_rel_bias, ca_out_norm, w_out, ffn_pre_norm,
           ffn_post_norm, w_gate, w_up, w_down):
    batch, seq, d = x.shape
    t = batch * seq
    depth = w_in.shape[0]
    xf = x.reshape(t, d)
    pos = positions.reshape(t, 1)
    w_in_b = _prep_w_in(w_in)
    w_uq_b = _prep_w_uq(w_uq)
    w_uk_b, w_uvt_b = _prep_w_ukv(w_ukv)
    w_out_b, w_gate_b, w_up_b, w_down_b = (w.astype(BF16) for w in (w_out, w_gate, w_up, w_down))
    for l in range(depth):
        h = _norm_matmul(xf, attn_pre_norm[l], [w_in_b], l, F32, tm=1024, tn=512)
        q, k, vt = _mla_proj(h, pos, mla_q_norm[l], mla_kv_norm[l], w_uq_b, w_uk_b, w_uvt_b, l, tm=MLA_TK)
        o_mla = _mla_attn(q, k, vt, batch, seq)
        o_hg = _hgrn(h, hg_lower_bounds, hg_out_norm[l], l, batch, seq)
        o_ca = _chunk_attn(h, ca_rel_bias[l], ca_out_norm[l], batch, seq)
        xf = _out_proj(o_mla, o_hg, o_ca, xf, mla_out_norm[l], attn_post_norm[l], w_out_b, l, tm=512)
        hid = _norm_matmul(xf, ffn_pre_norm[l], [w_gate_b, w_up_b], l, BF16, tm=1024, tn=512)
        xf = _down_proj(hid, w_down_b, xf, ffn_post_norm[l], l, tm=512, tk=1024)
    return xf.reshape(batch, seq, d)
```

```python
import functools
import math

import jax
import jax.numpy as jnp
from jax import lax
from jax.experimental import pallas as pl
from jax.experimental.pallas import tpu as pltpu

F32 = jnp.float32
BF16 = jnp.bfloat16

EPS = 1e-6
MASK_VALUE = -1e30
TINY = 1e-30
CHUNK = 64

MLA_HEADS = 16
MLA_Q_RANK = 768
MLA_KV_RANK = 512
MLA_NOPE = 128
MLA_ROPE = 64
MLA_V = 128
ROPE_THETA = 10000.0
MLA_QK_PAD = 256
MLA_ONES = 16
MLA_VT_ROWS = MLA_V + MLA_ONES
MLA_EXP_SCALE = (MLA_NOPE + MLA_ROPE) ** -0.5 * math.log2(math.e)

HG_HEADS = 8
HG_DIM = 128
HG_BLOCK = 16
HG_CHUNK = 128
HG_ROWS = 512

CA_HEADS = 8
CA_DIM = 128
CA_LEFT_CHUNKS = 8
CA_REL_CLIP = 256
CA_TQ = 256
CA_WIN = 3 * CA_TQ
CA_BIAS_LEN = 1024
CA_SEQ_VARIANTS = 3

MLA_WIDTH = MLA_HEADS * MLA_V
HG_WIDTH = HG_HEADS * HG_DIM
CA_WIDTH = CA_HEADS * CA_DIM

COL_AQ = 0
COL_AK = 1024
COL_AV = 2048
COL_HQ = 3072
COL_HF = 4096
COL_HI = 5120
COL_HG = 6144
COL_CKV = 7168
COL_CQ = 7680
COL_KR = 8448
D_IN_PAD = 8704

LANE = 128
VMEM_LIMIT = 56 * 1024 * 1024


def _cparams(sem, vmem=VMEM_LIMIT):
    return pltpu.CompilerParams(dimension_semantics=sem, vmem_limit_bytes=vmem)


def _rms_scale(x):
    return lax.rsqrt(jnp.mean(x * x, axis=-1, keepdims=True) + EPS)


NORM_ROWS = 16
NORM_UNROLL = 4


def _norm_rows_into(xn_ref, x_ref, g_ref):
    rows = x_ref.shape[0]

    def body(c, carry):
        r = pl.ds(pl.multiple_of(c * NORM_ROWS, NORM_ROWS), NORM_ROWS)
        x = x_ref[r, :]
        xn_ref[r, :] = ((x * _rms_scale(x)) * g_ref[...]).astype(BF16)
        return carry

    lax.fori_loop(0, rows // NORM_ROWS, body, 0, unroll=NORM_UNROLL)


def _norm_matmul_kernel(x_ref, g_ref, w_ref, o_ref, xn_ref):
    @pl.when(pl.program_id(1) == 0)
    def _():
        _norm_rows_into(xn_ref, x_ref, g_ref)

    o_ref[...] = jnp.dot(xn_ref[...], w_ref[...], preferred_element_type=F32).astype(o_ref.dtype)


def _norm_swiglu_kernel(x_ref, g_ref, wg_ref, wu_ref, o_ref, xn_ref):
    @pl.when(pl.program_id(1) == 0)
    def _():
        _norm_rows_into(xn_ref, x_ref, g_ref)

    xn = xn_ref[...]
    gate = jnp.dot(xn, wg_ref[...], preferred_element_type=F32)
    up = jnp.dot(xn, wu_ref[...], preferred_element_type=F32)
    o_ref[...] = ((gate * jax.nn.sigmoid(gate)) * up).astype(o_ref.dtype)


def _norm_matmul(x, g, ws, layer, out_dtype, tm, tn):
    t, d = x.shape
    n = ws[0].shape[2]
    kern = _norm_matmul_kernel if len(ws) == 1 else _norm_swiglu_kernel
    w_specs = [pl.BlockSpec((None, d, tn), lambda i, j: (layer, 0, j)) for _ in ws]
    return pl.pallas_call(
        kern,
        grid=(t // tm, pl.cdiv(n, tn)),
        in_specs=[pl.BlockSpec((tm, d), lambda i, j: (i, 0), pipeline_mode=pl.Buffered(1)),
                  pl.BlockSpec((1, d), lambda i, j: (0, 0))] + w_specs,
        out_specs=pl.BlockSpec((tm, tn), lambda i, j: (i, j)),
        out_shape=jax.ShapeDtypeStruct((t, n), out_dtype),
        scratch_shapes=[pltpu.VMEM((tm, d), BF16)],
        compiler_params=_cparams(("parallel", "arbitrary")),
        name="norm_matmul" if len(ws) == 1 else "norm_swiglu",
    )(x, g.reshape(1, d), *ws)


def _rope_tables(pos_ref):
    lane = lax.broadcasted_iota(jnp.int32, (1, LANE), 1)
    half = MLA_ROPE // 2
    idx = (lane % half).astype(F32)
    inv_freq = jnp.exp((-math.log(ROPE_THETA) * 2.0) * idx / MLA_ROPE)
    inv_freq = jnp.where(lane < MLA_ROPE, inv_freq, 0.0)
    ang = pos_ref[...].astype(F32) * inv_freq
    cos, sin = jnp.cos(ang), jnp.sin(ang)
    sin_hi = jnp.where((lane >= half) & (lane < MLA_ROPE), sin, 0.0)
    sin_lo = jnp.where(lane < half, -sin, 0.0)
    return cos, sin_hi, sin_lo


def _rope(x, tables):
    cos, sin_hi, sin_lo = tables
    half = MLA_ROPE // 2
    return x * cos + pltpu.roll(x, half, 1) * sin_hi + pltpu.roll(x, LANE - half, 1) * sin_lo


def _rope_tables_t(pos_row_ref):
    half = MLA_ROPE // 2
    idx = lax.broadcasted_iota(jnp.int32, (half, 1), 0).astype(F32)
    inv_freq = jnp.exp((-math.log(ROPE_THETA) * 2.0) * idx / MLA_ROPE)
    ang = inv_freq * pos_row_ref[...].astype(F32)
    return jnp.cos(ang), jnp.sin(ang)


def _mla_proj_kernel(cq_ref, ckv_ref, kr_ref, pos_ref, pos_row_ref, gq_ref, gkv_ref, wqt_ref, wk_ref, wvt_ref,
                     qt_ref, k_ref, vt_ref):
    tables = _rope_tables(pos_ref)
    cos_t, sin_t = _rope_tables_t(pos_row_ref)
    cq = cq_ref[...]
    cqn = ((cq * _rms_scale(cq)) * gq_ref[...]).astype(BF16)
    ckv = ckv_ref[...]
    ckvn = ((ckv * _rms_scale(ckv)) * gkv_ref[...]).astype(BF16)
    k_pe = _rope(kr_ref[...], tables).astype(BF16)
    for h in range(MLA_HEADS):
        c0 = h * MLA_QK_PAD
        kh = jnp.dot(ckvn, wk_ref[:, h * LANE:(h + 1) * LANE], preferred_element_type=F32)
        k_ref[:, c0:c0 + LANE] = kh.astype(BF16)
        k_ref[:, c0 + LANE:c0 + MLA_QK_PAD] = k_pe
    group = 4
    half = MLA_ROPE // 2
    nt_dims = (((1,), (1,)), ((), ()))
    for h0 in range(0, MLA_HEADS, group):
        qt = lax.dot_general(wqt_ref[h0 * MLA_QK_PAD:(h0 + group) * MLA_QK_PAD, :], cqn, nt_dims,
                             preferred_element_type=F32) * MLA_EXP_SCALE
        for h in range(h0, h0 + group):
            r0, g0 = h * MLA_QK_PAD, (h - h0) * MLA_QK_PAD
            t1 = qt[g0 + MLA_NOPE:g0 + MLA_NOPE + half, :]
            t2 = qt[g0 + MLA_NOPE + half:g0 + MLA_NOPE + MLA_ROPE, :]
            qt_ref[r0:r0 + MLA_NOPE, :] = qt[g0:g0 + MLA_NOPE, :].astype(BF16)
            qt_ref[r0 + MLA_NOPE:r0 + MLA_NOPE + half, :] = (t1 * cos_t - t2 * sin_t).astype(BF16)
            qt_ref[r0 + MLA_NOPE + half:r0 + MLA_NOPE + MLA_ROPE, :] = (t1 * sin_t + t2 * cos_t).astype(BF16)
            qt_ref[r0 + MLA_NOPE + MLA_ROPE:r0 + MLA_QK_PAD, :] = (
                qt[g0 + MLA_NOPE + MLA_ROPE:g0 + MLA_QK_PAD, :].astype(BF16))
    ones = jnp.ones((MLA_ONES, vt_ref.shape[2]), BF16)
    for h0 in range(0, MLA_HEADS, group):
        vt = lax.dot_general(wvt_ref[h0 * MLA_V:(h0 + group) * MLA_V, :], ckvn, nt_dims,
                             preferred_element_type=F32).astype(BF16)
        for h in range(h0, h0 + group):
            r0 = h * MLA_VT_ROWS
            vt_ref[0, r0:r0 + MLA_V, :] = vt[(h - h0) * MLA_V:(h - h0 + 1) * MLA_V, :]
            vt_ref[0, r0 + MLA_V:r0 + MLA_VT_ROWS, :] = ones


def _mla_proj(h, positions, gq, gkv, wqt, wk, wvt, layer, tm):
    t = h.shape[0]
    qk_w = MLA_HEADS * MLA_QK_PAD
    const = lambda i: (0, 0)
    return pl.pallas_call(
        _mla_proj_kernel,
        grid=(t // tm,),
        in_specs=[pl.BlockSpec((tm, MLA_Q_RANK), lambda i: (i, COL_CQ // MLA_Q_RANK)),
                  pl.BlockSpec((tm, MLA_KV_RANK), lambda i: (i, COL_CKV // MLA_KV_RANK)),
                  pl.BlockSpec((tm, LANE), lambda i: (i, COL_KR // LANE)),
                  pl.BlockSpec((tm, 1), lambda i: (i, 0)),
                  pl.BlockSpec((1, tm), lambda i: (0, i)),
                  pl.BlockSpec((1, MLA_Q_RANK), const),
                  pl.BlockSpec((1, MLA_KV_RANK), const),
                  pl.BlockSpec((None,) + wqt.shape[1:], lambda i: (layer, 0, 0)),
                  pl.BlockSpec((None,) + wk.shape[1:], lambda i: (layer, 0, 0)),
                  pl.BlockSpec((None,) + wvt.shape[1:], lambda i: (layer, 0, 0))],
        out_specs=[pl.BlockSpec((qk_w, tm), lambda i: (0, i)),
                   pl.BlockSpec((tm, qk_w), lambda i: (i, 0)),
                   pl.BlockSpec((1, MLA_HEADS * MLA_VT_ROWS, tm), lambda i: (i, 0, 0))],
        out_shape=[jax.ShapeDtypeStruct((qk_w, t), BF16),
                   jax.ShapeDtypeStruct((t, qk_w), BF16),
                   jax.ShapeDtypeStruct((t // tm, MLA_HEADS * MLA_VT_ROWS, tm), BF16)],
        compiler_params=_cparams(("parallel",)),
        name="mla_proj",
    )(h, h, h, positions, positions.reshape(1, t), gq.reshape(1, -1), gkv.reshape(1, -1), wqt, wk, wvt)


MLA_TQ = 2048
MLA_TK = 512
MLA_QBLK = 256
MLA_KBLK = 128


def _mla_attn_kernel(qt_ref, k_ref, vt_ref, o_ref, sa_ref, sb_ref, xa_ref, xb_ref, pa_ref, pb_ref,
                     m_ref, alpha_ref, acc_ref):
    i = pl.program_id(2)
    tq, tk = MLA_TQ, MLA_TK
    m_ref[...] = jnp.full(m_ref.shape, -jnp.inf, F32)
    alpha_ref[...] = jnp.ones(alpha_ref.shape, F32)
    acc_ref[...] = jnp.zeros(acc_ref.shape, F32)
    pb_ref[...] = jnp.zeros(pb_ref.shape, BF16)

    def scores_into(s_ref, x_ref, t, q_lo=0):
        r = pl.ds(pl.multiple_of(t * tk, tk), tk)
        s = jnp.dot(k_ref[r, :], qt_ref[:, q_lo:], preferred_element_type=F32)
        s_ref[:, q_lo:] = s
        x_ref[:, q_lo:] = jnp.max(s, axis=0, keepdims=True)

    def add_values(p_ref, vt_tile, q_lo=0):
        acc_ref[:, q_lo:] = alpha_ref[:, q_lo:] * acc_ref[:, q_lo:] + jnp.dot(
            vt_tile, p_ref[:, q_lo:], preferred_element_type=F32)

    def softmax_terms(s_ref, x_ref, p_ref, chunk_shift, q_lo=0):
        for q0 in range(q_lo, tq, MLA_QBLK):
            cols = slice(q0, q0 + MLA_QBLK)

            def masked(s, k0):
                kc = (lax.broadcasted_iota(jnp.int32, s.shape, 0) + k0) // CHUNK + chunk_shift
                qc = (lax.broadcasted_iota(jnp.int32, s.shape, 1) + q0) // CHUNK
                return jnp.where(kc <= qc, s, MASK_VALUE)

            if chunk_shift is None:
                tile_max = x_ref[:, cols]
            else:
                tile_max = jnp.max(masked(s_ref[:, cols], 0), axis=0, keepdims=True)
            m_old = m_ref[:, cols]
            m_new = jnp.maximum(m_old, tile_max)
            m_ref[:, cols] = m_new
            alpha_ref[:, cols] = jnp.exp2(m_old - m_new)
            for k0 in range(0, tk, MLA_KBLK):
                rows = slice(k0, k0 + MLA_KBLK)
                s = s_ref[rows, cols]
                if chunk_shift is not None:
                    s = masked(s, k0)
                p_ref[rows, cols] = jnp.exp2(s - m_new).astype(BF16)

    nt = tq // tk
    first_diag = nt * i

    def pair(u, d):
        def lo(dd):
            return 0 if dd is None or dd < 0 else dd * tk

        def shift(dd):
            return None if dd is None else dd * (tk // CHUNK)

        d_odd = None if d is None else d + 1
        d_prev = None if d is None else d - 1
        scores_into(sb_ref, xb_ref, 2 * u + 1, lo(d_odd))
        add_values(pb_ref, vt_ref[jnp.maximum(2 * u - 1, 0)], lo(d_prev))
        softmax_terms(sa_ref, xa_ref, pa_ref, shift(d), lo(d))
        if d is None:
            scores_into(sa_ref, xa_ref, 2 * u + 2)
        elif d + 2 < nt:
            scores_into(sa_ref, xa_ref, 2 * u + 2, lo(d + 2))
        add_values(pa_ref, vt_ref[2 * u], lo(d))
        softmax_terms(sb_ref, xb_ref, pb_ref, shift(d_odd), lo(d_odd))

    scores_into(sa_ref, xa_ref, 0)

    def body(u, carry):
        pair(u, None)
        return carry

    lax.fori_loop(0, first_diag // 2, body, 0)
    for d in range(0, nt, 2):
        pair((first_diag + d) // 2, d)
    add_values(pb_ref, vt_ref[first_diag + nt - 1], (nt - 1) * tk)
    o_ref[...] = (acc_ref[:MLA_V, :] / acc_ref[MLA_V:MLA_V + 1, :]).T


def _mla_attn(qt, k, vt, batch, seq):
    tq = MLA_TQ
    nq = seq // tq
    assert vt.shape[2] == MLA_TK
    return pl.pallas_call(
        _mla_attn_kernel,
        grid=(batch, MLA_HEADS, nq),
        in_specs=[pl.BlockSpec((MLA_QK_PAD, tq), lambda b, h, i: (h, b * nq + i)),
                  pl.BlockSpec((seq, MLA_QK_PAD), lambda b, h, i: (b, h)),
                  pl.BlockSpec((seq // MLA_TK, MLA_VT_ROWS, MLA_TK), lambda b, h, i: (b, h, 0))],
        out_specs=pl.BlockSpec((tq, MLA_V), lambda b, h, i: (b * nq + i, h)),
        out_shape=jax.ShapeDtypeStruct((batch * seq, MLA_WIDTH), F32),
        scratch_shapes=[pltpu.VMEM((MLA_TK, tq), F32), pltpu.VMEM((MLA_TK, tq), F32),
                        pltpu.VMEM((1, tq), F32), pltpu.VMEM((1, tq), F32),
                        pltpu.VMEM((MLA_TK, tq), BF16), pltpu.VMEM((MLA_TK, tq), BF16),
                        pltpu.VMEM((1, tq), F32), pltpu.VMEM((1, tq), F32),
                        pltpu.VMEM((MLA_VT_ROWS, tq), F32)],
        compiler_params=_cparams(("parallel", "parallel", "arbitrary")),
        name="mla_attn",
    )(qt, k, vt)


def _block_cumsum(x, row):
    r = row % HG_BLOCK
    s = 1
    while s < HG_BLOCK:
        x = x + jnp.where(r >= s, pltpu.roll(x, s, 0), 0.0)
        s *= 2
    return x


def _hgrn_chunk(hq, hf, hi, lb, state):
    c = HG_CHUNK
    nb = c // HG_BLOCK
    row = lax.broadcasted_iota(jnp.int32, (c, HG_DIM), 0)
    col = lax.broadcasted_iota(jnp.int32, (c, HG_DIM), 1)

    q = hq * jax.nn.sigmoid(hq)
    f = lb + (1.0 - lb) * jax.nn.sigmoid(hf)
    k = (1.0 - lb) * jax.nn.sigmoid(-hf)
    b = _block_cumsum(jnp.log(jnp.maximum(f, TINY)), row)
    b3 = b.reshape(nb, HG_BLOCK, HG_DIM)
    b_last3 = jnp.broadcast_to(b3[:, HG_BLOCK - 1:HG_BLOCK, :], b3.shape)
    b_last = b_last3.reshape(c, HG_DIM)

    q3 = q.reshape(nb, HG_BLOCK, HG_DIM)
    k3 = k.reshape(nb, HG_BLOCK, HG_DIM)
    v3 = hi.reshape(nb, HG_BLOCK, HG_DIM)
    irow = lax.broadcasted_iota(jnp.int32, b3.shape, 1)
    b3_log2 = b3 * math.log2(math.e)
    sub = 8
    ws = []
    for j in range(HG_BLOCK):
        lo = (j // sub) * sub
        w = q3[:, lo:, :] * k3[:, j:j + 1, :] * jnp.exp2(b3_log2[:, lo:, :] - b3_log2[:, j:j + 1, :])
        if j > lo:
            w = jnp.where(irow[:, lo:, :] >= j, w, 0.0)
        ws.append(w.reshape(nb * (HG_BLOCK - lo), HG_DIM))
    sums = jnp.dot(jnp.concatenate(ws, axis=0).astype(BF16), jnp.ones((HG_DIM, HG_DIM), BF16),
                   preferred_element_type=F32)
    o_parts = [jnp.zeros((nb, sub, HG_DIM), F32) for _ in range(HG_BLOCK // sub)]
    r_at = 0
    for j in range(HG_BLOCK):
        lo = (j // sub) * sub
        n_rows = nb * (HG_BLOCK - lo)
        a_j = sums[r_at:r_at + n_rows, :].reshape(nb, HG_BLOCK - lo, HG_DIM)
        r_at += n_rows
        wv = a_j * v3[:, j:j + 1, :]
        for part in range(lo // sub, HG_BLOCK // sub):
            r0 = part * sub - lo
            o_parts[part] = o_parts[part] + wv[:, r0:r0 + sub, :]
    o = jnp.concatenate(o_parts, axis=1).reshape(c, HG_DIM)

    q_dec = q * jnp.exp(b)
    k_dec_t = (k * jnp.exp(b_last - b)).T
    b_last_t = b_last.T
    v_bf = hi.astype(BF16)
    blk_of_col = col // HG_BLOCK
    k_stack = jnp.concatenate(
        [jnp.where(blk_of_col == j, k_dec_t, 0.0) for j in range(nb)], axis=0).astype(BF16)
    u_all = jnp.dot(k_stack, v_bf, preferred_element_type=F32)
    states = []
    for j in range(nb):
        states.append(state.astype(BF16))
        decay = jnp.exp(b_last_t[:, j * HG_BLOCK:j * HG_BLOCK + 1])
        state = decay * state + u_all[j * HG_DIM:(j + 1) * HG_DIM, :]
    s_stack = jnp.concatenate(states, axis=0)
    blk_of_row = row // HG_BLOCK
    q_exp = jnp.concatenate(
        [jnp.where(blk_of_row == j, q_dec, 0.0) for j in range(nb)], axis=1).astype(BF16)
    o = o + jnp.dot(q_exp, s_stack, preferred_element_type=F32)
    return o, state


def _hgrn_kernel(hq_ref, hf_ref, hi_ref, hg_ref, lbraw_ref, gn_ref, o_ref, state_ref, *, layer):
    @pl.when(pl.program_id(2) == 0)
    def _():
        state_ref[...] = jnp.zeros(state_ref.shape, F32)

    raw = lbraw_ref[...]
    e = jnp.exp(raw - jnp.max(raw, axis=0, keepdims=True))
    p = e / jnp.sum(e, axis=0, keepdims=True)
    lb = jnp.sum(p[:layer + 1, :], axis=0, keepdims=True) - p[0:1, :]

    def body(ci, carry):
        r = pl.ds(pl.multiple_of(ci * HG_CHUNK, HG_CHUNK), HG_CHUNK)
        o, state = _hgrn_chunk(hq_ref[r, :], hf_ref[r, :], hi_ref[r, :], lb, state_ref[...])
        state_ref[...] = state
        o = (o * _rms_scale(o)) * gn_ref[...]
        g = hg_ref[r, :]
        o_ref[r, :] = (o * (g * jax.nn.sigmoid(g))).astype(o_ref.dtype)
        return carry

    lax.fori_loop(0, hq_ref.shape[0] // HG_CHUNK, body, 0, unroll=2)


def _hgrn(h, lb_raw, gn, layer, batch, seq):
    rows = HG_ROWS
    nr = seq // rows
    depth = lb_raw.shape[0]

    def col_spec(col0):
        return pl.BlockSpec((rows, HG_DIM), lambda b, hh, c: (b * nr + c, col0 // HG_DIM + hh))

    return pl.pallas_call(
        functools.partial(_hgrn_kernel, layer=layer),
        grid=(batch, HG_HEADS, nr),
        in_specs=[col_spec(COL_HQ), col_spec(COL_HF), col_spec(COL_HI), col_spec(COL_HG),
                  pl.BlockSpec((depth, HG_DIM), lambda b, hh, c: (0, hh)),
                  pl.BlockSpec((1, HG_DIM), lambda b, hh, c: (0, hh))],
        out_specs=pl.BlockSpec((rows, HG_DIM), lambda b, hh, c: (b * nr + c, hh)),
        out_shape=jax.ShapeDtypeStruct((batch * seq, HG_WIDTH), BF16),
        scratch_shapes=[pltpu.VMEM((HG_DIM, HG_DIM), F32)],
        compiler_params=_cparams(("parallel", "parallel", "arbitrary")),
        name="hgrn",
    )(h, h, h, h, lb_raw, gn.reshape(1, -1))


def _ca_bias_rows(rel_bias):
    idx = jnp.arange(CA_BIAS_LEN)
    m = jnp.where(idx < CA_WIN, idx, idx - CA_BIAS_LEN)
    bucket = jnp.clip(2 * CA_TQ - m, -CA_REL_CLIP, CA_REL_CLIP) + CA_REL_CLIP
    return rel_bias[:, bucket].astype(F32)


def _ca_kernel(q_ref, k0_ref, k1_ref, k2_ref, v0_ref, v1_ref, v2_ref, brow_ref, gn_ref,
               o_ref, bias_ref):
    t = pl.program_id(1)
    tq = CA_TQ

    log2e = math.log2(math.e)

    @pl.when((pl.program_id(0) == 0) & (t == 0))
    def _():
        qc = lax.broadcasted_iota(jnp.int32, (tq, CA_WIN), 0) // CHUNK
        kk = lax.broadcasted_iota(jnp.int32, (tq, CA_WIN), 1)
        kc = kk // CHUNK
        band = (kc >= qc) & (kc <= qc + CA_LEFT_CHUNKS)
        for h in range(CA_HEADS):
            rows = jnp.broadcast_to(brow_ref[h:h + 1, :], (tq, CA_BIAS_LEN))
            rolled = pltpu.roll(rows, 0, 1, stride=1, stride_axis=0)
            base = jnp.where(band, rolled[:, :CA_WIN] * log2e, MASK_VALUE)
            for v in range(CA_SEQ_VARIANTS):
                first_valid = max(2 * tq - v * tq, 0)
                bias_ref[v, h] = jnp.where(kk >= first_valid, base, MASK_VALUE) if first_valid else base

    variant = jnp.minimum(t, CA_SEQ_VARIANTS - 1)
    c_exp = CA_DIM ** -0.5 * log2e
    ones = jnp.ones((CA_WIN, CA_DIM), BF16)
    outs = []
    for h in range(CA_HEADS):
        c = slice(h * CA_DIM, (h + 1) * CA_DIM)
        qh = (q_ref[:, c] * c_exp).astype(BF16)
        kh = jnp.concatenate([k0_ref[:, c], k1_ref[:, c], k2_ref[:, c]], axis=0).astype(BF16)
        vh = jnp.concatenate([v0_ref[:, c], v1_ref[:, c], v2_ref[:, c]], axis=0).astype(BF16)
        s = lax.dot_general(qh, kh, (((1,), (1,)), ((), ())), preferred_element_type=F32)
        s = s + bias_ref[variant, h]
        p = jnp.exp2(s - jnp.max(s, axis=-1, keepdims=True)).astype(BF16)
        ov = jnp.dot(p, jnp.concatenate([vh, ones], axis=1), preferred_element_type=F32)
        outs.append(ov[:, :CA_DIM] / ov[:, CA_DIM:CA_DIM + 1])
    o = jnp.concatenate(outs, axis=1)
    o_ref[...] = ((o * _rms_scale(o)) * gn_ref[...]).astype(o_ref.dtype)


def _chunk_attn(h, rel_bias, gn, batch, seq):
    tq = CA_TQ
    nq = seq // tq
    cq, ck, cv = COL_AQ // CA_WIDTH, COL_AK // CA_WIDTH, COL_AV // CA_WIDTH

    def kv_spec(col, back):
        return pl.BlockSpec((tq, CA_WIDTH), lambda b, t: (b * nq + jnp.maximum(t - back, 0), col))

    return pl.pallas_call(
        _ca_kernel,
        grid=(batch, nq),
        in_specs=[pl.BlockSpec((tq, CA_WIDTH), lambda b, t: (b * nq + t, cq)),
                  kv_spec(ck, 2), kv_spec(ck, 1), kv_spec(ck, 0),
                  kv_spec(cv, 2), kv_spec(cv, 1), kv_spec(cv, 0),
                  pl.BlockSpec((CA_HEADS, CA_BIAS_LEN), lambda b, t: (0, 0)),
                  pl.BlockSpec((1, CA_WIDTH), lambda b, t: (0, 0))],
        out_specs=pl.BlockSpec((tq, CA_WIDTH), lambda b, t: (b * nq + t, 0)),
        out_shape=jax.ShapeDtypeStruct((batch * seq, CA_WIDTH), BF16),
        scratch_shapes=[pltpu.VMEM((CA_SEQ_VARIANTS, CA_HEADS, tq, CA_WIN), F32)],
        compiler_params=_cparams(("arbitrary", "arbitrary")),
        name="chunk_attn",
    )(h, h, h, h, h, h, h, _ca_bias_rows(rel_bias), gn.reshape(1, -1))


X_SLICES = 8


def _residual_norm(o_ref, xs_ref, g_ref):
    rows, d = o_ref.shape
    w = d // X_SLICES

    step = NORM_ROWS * NORM_UNROLL

    def body(c, carry):
        base = pl.multiple_of(c * step, step)
        chunks = [pl.ds(base + u * NORM_ROWS, NORM_ROWS) for u in range(NORM_UNROLL)]
        ys = [o_ref[r, :] for r in chunks]
        yns = [(y * _rms_scale(y)) * g_ref[...] for y in ys]
        for r, yn in zip(chunks, yns):
            for p in range(X_SLICES):
                o_ref[r, p * w:(p + 1) * w] = xs_ref[p, r, :] + yn[:, p * w:(p + 1) * w]
        return carry

    lax.fori_loop(0, rows // step, body, 0)


def _x_slice_spec(tm, d):
    return pl.BlockSpec((tm, d // X_SLICES), lambda i, k: (i, jnp.minimum(k, X_SLICES - 1)))


def _accumulate_then_residual_norm(a_ref, w_ref, x_ref, g_ref, o_ref, xs_ref, ragged):
    k = pl.program_id(1)
    last = pl.num_programs(1) - 1

    @pl.when(k < X_SLICES)
    def _():
        xs_ref[k] = x_ref[...]

    @pl.when(k == 0)
    def _():
        o_ref[...] = jnp.dot(a_ref[...], w_ref[...], preferred_element_type=F32)

    if ragged is None:
        @pl.when(k > 0)
        def _():
            o_ref[...] += jnp.dot(a_ref[...], w_ref[...], preferred_element_type=F32)
    else:
        @pl.when((k > 0) & (k < last))
        def _():
            o_ref[...] += jnp.dot(a_ref[...], w_ref[...], preferred_element_type=F32)

        @pl.when(k == last)
        def _():
            a, w = a_ref[...], w_ref[...]
            a = jnp.where(lax.broadcasted_iota(jnp.int32, a.shape, 1) < ragged, a, jnp.zeros_like(a))
            w = jnp.where(lax.broadcasted_iota(jnp.int32, w.shape, 0) < ragged, w, jnp.zeros_like(w))
            o_ref[...] += jnp.dot(a, w, preferred_element_type=F32)

    @pl.when(k == last)
    def _():
        _residual_norm(o_ref, xs_ref, g_ref)


OUT_TK = 512


def _out_proj_kernel(mla_ref, hg_ref, ca_ref, w_ref, x_ref, gm_ref, gp_ref, o_ref, a_ref, xs_ref):
    @pl.when(pl.program_id(1) == 0)
    def _():
        rows = mla_ref.shape[0]
        n_mla, n_hg, n_ca = MLA_WIDTH // OUT_TK, HG_WIDTH // OUT_TK, CA_WIDTH // OUT_TK

        def body(c, carry):
            r = pl.ds(pl.multiple_of(c * NORM_ROWS, NORM_ROWS), NORM_ROWS)
            m = mla_ref[r, :]
            mn = ((m * _rms_scale(m)) * gm_ref[...]).astype(BF16)
            for p in range(n_mla):
                a_ref[p, r, :] = mn[:, p * OUT_TK:(p + 1) * OUT_TK]
            return carry

        lax.fori_loop(0, rows // NORM_ROWS, body, 0, unroll=NORM_UNROLL)
        for p in range(n_hg):
            a_ref[n_mla + p] = hg_ref[:, p * OUT_TK:(p + 1) * OUT_TK]
        for p in range(n_ca):
            a_ref[n_mla + n_hg + p] = ca_ref[:, p * OUT_TK:(p + 1) * OUT_TK]

    _accumulate_then_residual_norm(a_ref.at[pl.program_id(1)], w_ref, x_ref, gp_ref, o_ref, xs_ref, None)


def _out_proj(o_mla, o_hg, o_ca, x, g_mla, g_post, w, layer, tm):
    t, d = x.shape
    kdim = w.shape[1]
    nk = kdim // OUT_TK
    assert MLA_WIDTH % OUT_TK == 0 and HG_WIDTH % OUT_TK == 0 and CA_WIDTH % OUT_TK == 0
    assert kdim == MLA_WIDTH + HG_WIDTH + CA_WIDTH and nk >= X_SLICES
    return pl.pallas_call(
        _out_proj_kernel,
        grid=(t // tm, nk),
        in_specs=[pl.BlockSpec((tm, MLA_WIDTH), lambda i, k: (i, 0)),
                  pl.BlockSpec((tm, HG_WIDTH), lambda i, k: (i, 0)),
                  pl.BlockSpec((tm, CA_WIDTH), lambda i, k: (i, 0)),
                  pl.BlockSpec((None, OUT_TK, d), lambda i, k: (layer, k, 0)),
                  _x_slice_spec(tm, d),
                  pl.BlockSpec((1, MLA_WIDTH), lambda i, k: (0, 0)),
                  pl.BlockSpec((1, d), lambda i, k: (0, 0))],
        out_specs=pl.BlockSpec((tm, d), lambda i, k: (i, 0)),
        out_shape=jax.ShapeDtypeStruct((t, d), F32),
        scratch_shapes=[pltpu.VMEM((nk, tm, OUT_TK), BF16),
                        pltpu.VMEM((X_SLICES, tm, d // X_SLICES), F32)],
        compiler_params=_cparams(("parallel", "arbitrary")),
        name="out_proj",
    )(o_mla, o_hg, o_ca, w, x, g_mla.reshape(1, -1), g_post.reshape(1, -1))


def _down_proj_kernel(a_ref, w_ref, x_ref, g_ref, o_ref, xs_ref, *, ragged):
    _accumulate_then_residual_norm(a_ref, w_ref, x_ref, g_ref, o_ref, xs_ref, ragged)


def _down_proj(a, w, x, g, layer, tm, tk):
    t, d = x.shape
    kdim = a.shape[1]
    ragged = kdim % tk or None
    nk = pl.cdiv(kdim, tk)
    assert nk >= X_SLICES
    return pl.pallas_call(
        functools.partial(_down_proj_kernel, ragged=ragged),
        grid=(t // tm, nk),
        in_specs=[pl.BlockSpec((tm, tk), lambda i, k: (i, k)),
                  pl.BlockSpec((None, tk, d), lambda i, k: (layer, k, 0)),
                  _x_slice_spec(tm, d),
                  pl.BlockSpec((1, d), lambda i, k: (0, 0))],
        out_specs=pl.BlockSpec((tm, d), lambda i, k: (i, 0)),
        out_shape=jax.ShapeDtypeStruct((t, d), F32),
        scratch_shapes=[pltpu.VMEM((X_SLICES, tm, d // X_SLICES), F32)],
        compiler_params=_cparams(("parallel", "arbitrary")),
        name="down_proj",
    )(a, w, x, g.reshape(1, -1))


def _prep_w_in(w):
    depth, d, _ = w.shape
    o_ckv = MLA_Q_RANK
    o_kr = o_ckv + MLA_KV_RANK
    o_hg = o_kr + MLA_ROPE
    o_ca = o_hg + 4 * HG_WIDTH
    assert (COL_AQ, COL_AK, COL_AV) == (0, CA_WIDTH, 2 * CA_WIDTH) and COL_HQ == 3 * CA_WIDTH
    assert (COL_HF, COL_HI, COL_HG) == (COL_HQ + HG_WIDTH, COL_HQ + 2 * HG_WIDTH, COL_HQ + 3 * HG_WIDTH)
    assert COL_CKV == COL_HG + HG_WIDTH and COL_CQ == COL_CKV + MLA_KV_RANK and COL_KR == COL_CQ + MLA_Q_RANK
    wb = w.astype(BF16)
    z = jnp.zeros((depth, d, D_IN_PAD - COL_KR - MLA_ROPE), BF16)
    out = jnp.concatenate([wb[:, :, o_ca:], wb[:, :, o_hg:o_ca], wb[:, :, o_ckv:o_kr], wb[:, :, :o_ckv],
                           wb[:, :, o_kr:o_hg], z], axis=2)
    assert out.shape[2] == D_IN_PAD
    return out


def _prep_w_uq(w):
    depth, r, _ = w.shape
    w4 = w.astype(BF16).reshape(depth, r, MLA_HEADS, MLA_NOPE + MLA_ROPE)
    pad = jnp.zeros((depth, r, MLA_HEADS, MLA_QK_PAD - MLA_NOPE - MLA_ROPE), BF16)
    wq = jnp.concatenate([w4, pad], axis=3).reshape(depth, r, MLA_HEADS * MLA_QK_PAD)
    return jnp.swapaxes(wq, 1, 2)


def _prep_w_ukv(w):
    depth, r, _ = w.shape
    w4 = w.astype(BF16).reshape(depth, r, MLA_HEADS, MLA_NOPE + MLA_V)
    wk = w4[:, :, :, :MLA_NOPE].reshape(depth, r, MLA_HEADS * MLA_NOPE)
    wv = w4[:, :, :, MLA_NOPE:].reshape(depth, r, MLA_HEADS * MLA_V)
    return wk, jnp.swapaxes(wv, 1, 2)


def kernel(x, positions, attn_pre_norm, attn_post_norm, w_in, mla_q_norm, mla_kv_norm, w_uq, w_ukv,
           mla_out_norm, hg_lower_bounds, hg_out_norm, ca_rel_bias, ca_out_norm, w_out, ffn_pre_norm,
           ffn_post_norm, w_gate, w_up, w_down):
    batch, seq, d = x.shape
    t = batch * seq
    depth = w_in.shape[0]
    xf = x.reshape(t, d)
    pos = positions.reshape(t, 1)
    w_in_b = _prep_w_in(w_in)
    w_uq_b = _prep_w_uq(w_uq)
    w_uk_b, w_uvt_b = _prep_w_ukv(w_ukv)
    w_out_b, w_gate_b, w_up_b, w_down_b = (w.astype(BF16) for w in (w_out, w_gate, w_up, w_down))
    for l in range(depth):
        h = _norm_matmul(xf, attn_pre_norm[l], [w_in_b], l, F32, tm=1024, tn=512)
        q, k, vt = _mla_proj(h, pos, mla_q_norm[l], mla_kv_norm[l], w_uq_b, w_uk_b, w_uvt_b, l, tm=MLA_TK)
        o_mla = _mla_attn(q, k, vt, batch, seq)
        o_hg = _hgrn(h, hg_lower_bounds, hg_out_norm[l], l, batch, seq)
        o_ca = _chunk_attn(h, ca_rel_bias[l], ca_out_norm[l], batch, seq)
        xf = _out_proj(o_mla, o_hg, o_ca, xf, mla_out_norm[l], attn_post_norm[l], w_out_b, l, tm=512)
        hid = _norm_matmul(xf, ffn_pre_norm[l], [w_gate_b, w_up_b], l, BF16, tm=1024, tn=512)
        xf = _down_proj(hid, w_down_b, xf, ffn_post_norm[l], l, tm=512, tk=1024)
    return xf.reshape(batch, seq, d)
```

```python
import functools
import math

import jax
import jax.numpy as jnp
from jax import lax
from jax.experimental import pallas as pl
from jax.experimental.pallas import tpu as pltpu

F32 = jnp.float32
BF16 = jnp.bfloat16

EPS = 1e-6
MASK_VALUE = -1e30
TINY = 1e-30
CHUNK = 64

MLA_HEADS = 16
MLA_Q_RANK = 768
MLA_KV_RANK = 512
MLA_NOPE = 128
MLA_ROPE = 64
MLA_V = 128
ROPE_THETA = 10000.0
MLA_QK_PAD = 256
MLA_ONES = 16
MLA_VT_ROWS = MLA_V + MLA_ONES
MLA_EXP_SCALE = (MLA_NOPE + MLA_ROPE) ** -0.5 * math.log2(math.e)

HG_HEADS = 8
HG_DIM = 128
HG_BLOCK = 16
HG_CHUNK = 128
HG_ROWS = 512

CA_HEADS = 8
CA_DIM = 128
CA_LEFT_CHUNKS = 8
CA_REL_CLIP = 256
CA_TQ = 256
CA_WIN = 3 * CA_TQ
CA_BIAS_LEN = 1024
CA_SEQ_VARIANTS = 3

MLA_WIDTH = MLA_HEADS * MLA_V
HG_WIDTH = HG_HEADS * HG_DIM
CA_WIDTH = CA_HEADS * CA_DIM

COL_AQ = 0
COL_AK = 1024
COL_AV = 2048
COL_HQ = 3072
COL_HF = 4096
COL_HI = 5120
COL_HG = 6144
COL_CKV = 7168
COL_CQ = 7680
COL_KR = 8448
D_IN_PAD = 8704

LANE = 128
VMEM_LIMIT = 56 * 1024 * 1024


def _cparams(sem, vmem=VMEM_LIMIT):
    return pltpu.CompilerParams(dimension_semantics=sem, vmem_limit_bytes=vmem)


def _rms_scale(x):
    return lax.rsqrt(jnp.mean(x * x, axis=-1, keepdims=True) + EPS)


NORM_ROWS = 16
NORM_UNROLL = 4


def _norm_rows_into(xn_ref, x_ref, g_ref):
    rows = x_ref.shape[0]

    def body(c, carry):
        r = pl.ds(pl.multiple_of(c * NORM_ROWS, NORM_ROWS), NORM_ROWS)
        x = x_ref[r, :]
        xn_ref[r, :] = ((x * _rms_scale(x)) * g_ref[...]).astype(BF16)
        return carry

    lax.fori_loop(0, rows // NORM_ROWS, body, 0, unroll=NORM_UNROLL)


def _norm_matmul_kernel(x_ref, g_ref, w_ref, o_ref, xn_ref):
    @pl.when(pl.program_id(1) == 0)
    def _():
        _norm_rows_into(xn_ref, x_ref, g_ref)

    o_ref[...] = jnp.dot(xn_ref[...], w_ref[...], preferred_element_type=F32).astype(o_ref.dtype)


def _norm_swiglu_kernel(x_ref, g_ref, wg_ref, wu_ref, o_ref, xn_ref):
    @pl.when(pl.program_id(1) == 0)
    def _():
        _norm_rows_into(xn_ref, x_ref, g_ref)

    xn = xn_ref[...]
    gate = jnp.dot(xn, wg_ref[...], preferred_element_type=F32)
    up = jnp.dot(xn, wu_ref[...], preferred_element_type=F32)
    o_ref[...] = ((gate * jax.nn.sigmoid(gate)) * up).astype(o_ref.dtype)


def _norm_matmul(x, g, ws, layer, out_dtype, tm, tn):
    t, d = x.shape
    n = ws[0].shape[2]
    kern = _norm_matmul_kernel if len(ws) == 1 else _norm_swiglu_kernel
    w_specs = [pl.BlockSpec((None, d, tn), lambda i, j: (layer, 0, j)) for _ in ws]
    return pl.pallas_call(
        kern,
        grid=(t // tm, pl.cdiv(n, tn)),
        in_specs=[pl.BlockSpec((tm, d), lambda i, j: (i, 0), pipeline_mode=pl.Buffered(1)),
                  pl.BlockSpec((1, d), lambda i, j: (0, 0))] + w_specs,
        out_specs=pl.BlockSpec((tm, tn), lambda i, j: (i, j)),
        out_shape=jax.ShapeDtypeStruct((t, n), out_dtype),
        scratch_shapes=[pltpu.VMEM((tm, d), BF16)],
        compiler_params=_cparams(("parallel", "arbitrary")),
        name="norm_matmul" if len(ws) == 1 else "norm_swiglu",
    )(x, g.reshape(1, d), *ws)


def _rope_tables(pos_ref):
    lane = lax.broadcasted_iota(jnp.int32, (1, LANE), 1)
    half = MLA_ROPE // 2
    idx = (lane % half).astype(F32)
    inv_freq = jnp.exp((-math.log(ROPE_THETA) * 2.0) * idx / MLA_ROPE)
    inv_freq = jnp.where(lane < MLA_ROPE, inv_freq, 0.0)
    ang = pos_ref[...].astype(F32) * inv_freq
    cos, sin = jnp.cos(ang), jnp.sin(ang)
    sin_hi = jnp.where((lane >= half) & (lane < MLA_ROPE), sin, 0.0)
    sin_lo = jnp.where(lane < half, -sin, 0.0)
    return cos, sin_hi, sin_lo


def _rope(x, tables):
    cos, sin_hi, sin_lo = tables
    half = MLA_ROPE // 2
    return x * cos + pltpu.roll(x, half, 1) * sin_hi + pltpu.roll(x, LANE - half, 1) * sin_lo


def _rope_tables_t(pos_row_ref):
    half = MLA_ROPE // 2
    idx = lax.broadcasted_iota(jnp.int32, (half, 1), 0).astype(F32)
    inv_freq = jnp.exp((-math.log(ROPE_THETA) * 2.0) * idx / MLA_ROPE)
    ang = inv_freq * pos_row_ref[...].astype(F32)
    return jnp.cos(ang), jnp.sin(ang)


def _mla_proj_kernel(cq_ref, ckv_ref, kr_ref, pos_ref, pos_row_ref, gq_ref, gkv_ref, wqt_ref, wk_ref, wvt_ref,
                     qt_ref, k_ref, vt_ref):
    tables = _rope_tables(pos_ref)
    cos_t, sin_t = _rope_tables_t(pos_row_ref)
    cq = cq_ref[...]
    cqn = ((cq * _rms_scale(cq)) * gq_ref[...]).astype(BF16)
    ckv = ckv_ref[...]
    ckvn = ((ckv * _rms_scale(ckv)) * gkv_ref[...]).astype(BF16)
    k_pe = _rope(kr_ref[...], tables).astype(BF16)
    for h in range(MLA_HEADS):
        c0 = h * MLA_QK_PAD
        kh = jnp.dot(ckvn, wk_ref[:, h * LANE:(h + 1) * LANE], preferred_element_type=F32)
        k_ref[:, c0:c0 + LANE] = kh.astype(BF16)
        k_ref[:, c0 + LANE:c0 + MLA_QK_PAD] = k_pe
    group = 4
    half = MLA_ROPE // 2
    nt_dims = (((1,), (1,)), ((), ()))
    for h0 in range(0, MLA_HEADS, group):
        qt = lax.dot_general(wqt_ref[h0 * MLA_QK_PAD:(h0 + group) * MLA_QK_PAD, :], cqn, nt_dims,
                             preferred_element_type=F32) * MLA_EXP_SCALE
        for h in range(h0, h0 + group):
            r0, g0 = h * MLA_QK_PAD, (h - h0) * MLA_QK_PAD
            t1 = qt[g0 + MLA_NOPE:g0 + MLA_NOPE + half, :]
            t2 = qt[g0 + MLA_NOPE + half:g0 + MLA_NOPE + MLA_ROPE, :]
            qt_ref[r0:r0 + MLA_NOPE, :] = qt[g0:g0 + MLA_NOPE, :].astype(BF16)
            qt_ref[r0 + MLA_NOPE:r0 + MLA_NOPE + half, :] = (t1 * cos_t - t2 * sin_t).astype(BF16)
            qt_ref[r0 + MLA_NOPE + half:r0 + MLA_NOPE + MLA_ROPE, :] = (t1 * sin_t + t2 * cos_t).astype(BF16)
            qt_ref[r0 + MLA_NOPE + MLA_ROPE:r0 + MLA_QK_PAD, :] = (
                qt[g0 + MLA_NOPE + MLA_ROPE:g0 + MLA_QK_PAD, :].astype(BF16))
    ones = jnp.ones((MLA_ONES, vt_ref.shape[2]), BF16)
    for h0 in range(0, MLA_HEADS, group):
        vt = lax.dot_general(wvt_ref[h0 * MLA_V:(h0 + group) * MLA_V, :], ckvn, nt_dims,
                             preferred_element_type=F32).astype(BF16)
        for h in range(h0, h0 + group):
            r0 = h * MLA_VT_ROWS
            vt_ref[0, r0:r0 + MLA_V, :] = vt[(h - h0) * MLA_V:(h - h0 + 1) * MLA_V, :]
            vt_ref[0, r0 + MLA_V:r0 + MLA_VT_ROWS, :] = ones


def _mla_proj(h, positions, gq, gkv, wqt, wk, wvt, layer, tm):
    t = h.shape[0]
    qk_w = MLA_HEADS * MLA_QK_PAD
    const = lambda i: (0, 0)
    return pl.pallas_call(
        _mla_proj_kernel,
        grid=(t // tm,),
        in_specs=[pl.BlockSpec((tm, MLA_Q_RANK), lambda i: (i, COL_CQ // MLA_Q_RANK)),
                  pl.BlockSpec((tm, MLA_KV_RANK), lambda i: (i, COL_CKV // MLA_KV_RANK)),
                  pl.BlockSpec((tm, LANE), lambda i: (i, COL_KR // LANE)),
                  pl.BlockSpec((tm, 1), lambda i: (i, 0)),
                  pl.BlockSpec((1, tm), lambda i: (0, i)),
                  pl.BlockSpec((1, MLA_Q_RANK), const),
                  pl.BlockSpec((1, MLA_KV_RANK), const),
                  pl.BlockSpec((None,) + wqt.shape[1:], lambda i: (layer, 0, 0)),
                  pl.BlockSpec((None,) + wk.shape[1:], lambda i: (layer, 0, 0)),
                  pl.BlockSpec((None,) + wvt.shape[1:], lambda i: (layer, 0, 0))],
        out_specs=[pl.BlockSpec((qk_w, tm), lambda i: (0, i)),
                   pl.BlockSpec((tm, qk_w), lambda i: (i, 0)),
                   pl.BlockSpec((1, MLA_HEADS * MLA_VT_ROWS, tm), lambda i: (i, 0, 0))],
        out_shape=[jax.ShapeDtypeStruct((qk_w, t), BF16),
                   jax.ShapeDtypeStruct((t, qk_w), BF16),
                   jax.ShapeDtypeStruct((t // tm, MLA_HEADS * MLA_VT_ROWS, tm), BF16)],
        compiler_params=_cparams(("parallel",)),
        name="mla_proj",
    )(h, h, h, positions, positions.reshape(1, t), gq.reshape(1, -1), gkv.reshape(1, -1), wqt, wk, wvt)


MLA_TQ = 2048
MLA_TK = 512
MLA_QBLK = 256
MLA_KBLK = 128


def _mla_attn_kernel(qt_ref, k_ref, vt_ref, o_ref, sa_ref, sb_ref, xa_ref, xb_ref, pa_ref, pb_ref,
                     m_ref, alpha_ref, acc_ref):
    i = pl.program_id(2)
    tq, tk = MLA_TQ, MLA_TK
    m_ref[...] = jnp.full(m_ref.shape, -jnp.inf, F32)
    alpha_ref[...] = jnp.ones(alpha_ref.shape, F32)
    acc_ref[...] = jnp.zeros(acc_ref.shape, F32)
    pb_ref[...] = jnp.zeros(pb_ref.shape, BF16)

    def scores_into(s_ref, x_ref, t, q_lo=0):
        r = pl.ds(pl.multiple_of(t * tk, tk), tk)
        s = jnp.dot(k_ref[r, :], qt_ref[:, q_lo:], preferred_element_type=F32)
        s_ref[:, q_lo:] = s
        x_ref[:, q_lo:] = jnp.max(s, axis=0, keepdims=True)

    def add_values(p_ref, vt_tile, q_lo=0):
        acc_ref[:, q_lo:] = alpha_ref[:, q_lo:] * acc_ref[:, q_lo:] + jnp.dot(
            vt_tile, p_ref[:, q_lo:], preferred_element_type=F32)

    def softmax_terms(s_ref, x_ref, p_ref, chunk_shift, q_lo=0):
        for q0 in range(q_lo, tq, MLA_QBLK):
            cols = slice(q0, q0 + MLA_QBLK)

            def masked(s, k0):
                kc = (lax.broadcasted_iota(jnp.int32, s.shape, 0) + k0) // CHUNK + chunk_shift
                qc = (lax.broadcasted_iota(jnp.int32, s.shape, 1) + q0) // CHUNK
                return jnp.where(kc <= qc, s, MASK_VALUE)

            if chunk_shift is None:
                tile_max = x_ref[:, cols]
            else:
                tile_max = jnp.max(masked(s_ref[:, cols], 0), axis=0, keepdims=True)
            m_old = m_ref[:, cols]
            m_new = jnp.maximum(m_old, tile_max)
            m_ref[:, cols] = m_new
            alpha_ref[:, cols] = jnp.exp2(m_old - m_new)
            for k0 in range(0, tk, MLA_KBLK):
                rows = slice(k0, k0 + MLA_KBLK)
                s = s_ref[rows, cols]
                if chunk_shift is not None:
                    s = masked(s, k0)
                p_ref[rows, cols] = jnp.exp2(s - m_new).astype(BF16)

    nt = tq // tk
    first_diag = nt * i

    def pair(u, d):
        def lo(dd):
            return 0 if dd is None or dd < 0 else dd * tk

        def shift(dd):
            return None if dd is None else dd * (tk // CHUNK)

        d_odd = None if d is None else d + 1
        d_prev = None if d is None else d - 1
        scores_into(sb_ref, xb_ref, 2 * u + 1, lo(d_odd))
        add_values(pb_ref, vt_ref[jnp.maximum(2 * u - 1, 0)], lo(d_prev))
        softmax_terms(sa_ref, xa_ref, pa_ref, shift(d), lo(d))
        if d is None:
            scores_into(sa_ref, xa_ref, 2 * u + 2)
        elif d + 2 < nt:
            scores_into(sa_ref, xa_ref, 2 * u + 2, lo(d + 2))
        add_values(pa_ref, vt_ref[2 * u], lo(d))
        softmax_terms(sb_ref, xb_ref, pb_ref, shift(d_odd), lo(d_odd))

    scores_into(sa_ref, xa_ref, 0)

    def body(u, carry):
        pair(u, None)
        return carry

    lax.fori_loop(0, first_diag // 2, body, 0)
    for d in range(0, nt, 2):
        pair((first_diag + d) // 2, d)
    add_values(pb_ref, vt_ref[first_diag + nt - 1], (nt - 1) * tk)
    o_ref[...] = (acc_ref[:MLA_V, :] / acc_ref[MLA_V:MLA_V + 1, :]).T


def _mla_attn(qt, k, vt, batch, seq):
    tq = MLA_TQ
    nq = seq // tq
    assert vt.shape[2] == MLA_TK
    return pl.pallas_call(
        _mla_attn_kernel,
        grid=(batch, MLA_HEADS, nq),
        in_specs=[pl.BlockSpec((MLA_QK_PAD, tq), lambda b, h, i: (h, b * nq + i)),
                  pl.BlockSpec((seq, MLA_QK_PAD), lambda b, h, i: (b, h)),
                  pl.BlockSpec((seq // MLA_TK, MLA_VT_ROWS, MLA_TK), lambda b, h, i: (b, h, 0))],
        out_specs=pl.BlockSpec((tq, MLA_V), lambda b, h, i: (b * nq + i, h)),
        out_shape=jax.ShapeDtypeStruct((batch * seq, MLA_WIDTH), F32),
        scratch_shapes=[pltpu.VMEM((MLA_TK, tq), F32), pltpu.VMEM((MLA_TK, tq), F32),
                        pltpu.VMEM((1, tq), F32), pltpu.VMEM((1, tq), F32),
                        pltpu.VMEM((MLA_TK, tq), BF16), pltpu.VMEM((MLA_TK, tq), BF16),
                        pltpu.VMEM((1, tq), F32), pltpu.VMEM((1, tq), F32),
                        pltpu.VMEM((MLA_VT_ROWS, tq), F32)],
        compiler_params=_cparams(("parallel", "parallel", "arbitrary")),
        name="mla_attn",
    )(qt, k, vt)


def _block_cumsum(x, row):
    r = row % HG_BLOCK
    s = 1
    while s < HG_BLOCK:
        x = x + jnp.where(r >= s, pltpu.roll(x, s, 0), 0.0)
        s *= 2
    return x


def _hgrn_chunk(hq, hf, hi, lb, state):
    c = HG_CHUNK
    nb = c // HG_BLOCK
    row = lax.broadcasted_iota(jnp.int32, (c, HG_DIM), 0)
    col = lax.broadcasted_iota(jnp.int32, (c, HG_DIM), 1)

    q = hq * jax.nn.sigmoid(hq)
    f = lb + (1.0 - lb) * jax.nn.sigmoid(hf)
    k = (1.0 - lb) * jax.nn.sigmoid(-hf)
    b = _block_cumsum(jnp.log(jnp.maximum(f, TINY)), row)
    b3 = b.reshape(nb, HG_BLOCK, HG_DIM)
    b_last3 = jnp.broadcast_to(b3[:, HG_BLOCK - 1:HG_BLOCK, :], b3.shape)
    b_last = b_last3.reshape(c, HG_DIM)

    q3 = q.reshape(nb, HG_BLOCK, HG_DIM)
    k3 = k.reshape(nb, HG_BLOCK, HG_DIM)
    v3 = hi.reshape(nb, HG_BLOCK, HG_DIM)
    irow = lax.broadcasted_iota(jnp.int32, b3.shape, 1)
    b3_log2 = b3 * math.log2(math.e)
    sub = 8
    ws = []
    for j in range(HG_BLOCK):
        lo = (j // sub) * sub
        w = q3[:, lo:, :] * k3[:, j:j + 1, :] * jnp.exp2(b3_log2[:, lo:, :] - b3_log2[:, j:j + 1, :])
        if j > lo:
            w = jnp.where(irow[:, lo:, :] >= j, w, 0.0)
        ws.append(w.reshape(nb * (HG_BLOCK - lo), HG_DIM))
    sums = jnp.dot(jnp.concatenate(ws, axis=0).astype(BF16), jnp.ones((HG_DIM, HG_DIM), BF16),
                   preferred_element_type=F32)
    o_parts = [jnp.zeros((nb, sub, HG_DIM), F32) for _ in range(HG_BLOCK // sub)]
    r_at = 0
    for j in range(HG_BLOCK):
        lo = (j // sub) * sub
        n_rows = nb * (HG_BLOCK - lo)
        a_j = sums[r_at:r_at + n_rows, :].reshape(nb, HG_BLOCK - lo, HG_DIM)
        r_at += n_rows
        wv = a_j * v3[:, j:j + 1, :]
        for part in range(lo // sub, HG_BLOCK // sub):
            r0 = part * sub - lo
            o_parts[part] = o_parts[part] + wv[:, r0:r0 + sub, :]
    o = jnp.concatenate(o_parts, axis=1).reshape(c, HG_DIM)

    q_dec = q * jnp.exp(b)
    k_dec_t = (k * jnp.exp(b_last - b)).T
    b_last_t = b_last.T
    v_bf = hi.astype(BF16)
    blk_of_col = col // HG_BLOCK
    k_stack = jnp.concatenate(
        [jnp.where(blk_of_col == j, k_dec_t, 0.0) for j in range(nb)], axis=0).astype(BF16)
    u_all = jnp.dot(k_stack, v_bf, preferred_element_type=F32)
    states = []
    for j in range(nb):
        states.append(state.astype(BF16))
        decay = jnp.exp(b_last_t[:, j * HG_BLOCK:j * HG_BLOCK + 1])
        state = decay * state + u_all[j * HG_DIM:(j + 1) * HG_DIM, :]
    s_stack = jnp.concatenate(states, axis=0)
    blk_of_row = row // HG_BLOCK
    q_exp = jnp.concatenate(
        [jnp.where(blk_of_row == j, q_dec, 0.0) for j in range(nb)], axis=1).astype(BF16)
    o = o + jnp.dot(q_exp, s_stack, preferred_element_type=F32)
    return o, state


def _hgrn_kernel(hq_ref, hf_ref, hi_ref, hg_ref, lbraw_ref, gn_ref, o_ref, state_ref, *, layer):
    @pl.when(pl.program_id(2) == 0)
    def _():
        state_ref[...] = jnp.zeros(state_ref.shape, F32)

    raw = lbraw_ref[...]
    e = jnp.exp(raw - jnp.max(raw, axis=0, keepdims=True))
    p = e / jnp.sum(e, axis=0, keepdims=True)
    lb = jnp.sum(p[:layer + 1, :], axis=0, keepdims=True) - p[0:1, :]

    def body(ci, carry):
        r = pl.ds(pl.multiple_of(ci * HG_CHUNK, HG_CHUNK), HG_CHUNK)
        o, state = _hgrn_chunk(hq_ref[r, :], hf_ref[r, :], hi_ref[r, :], lb, state_ref[...])
        state_ref[...] = state
        o = (o * _rms_scale(o)) * gn_ref[...]
        g = hg_ref[r, :]
        o_ref[r, :] = (o * (g * jax.nn.sigmoid(g))).astype(o_ref.dtype)
        return carry

    lax.fori_loop(0, hq_ref.shape[0] // HG_CHUNK, body, 0, unroll=2)


def _hgrn(h, lb_raw, gn, layer, batch, seq):
    rows = HG_ROWS
    nr = seq // rows
    depth = lb_raw.shape[0]

    def col_spec(col0):
        return pl.BlockSpec((rows, HG_DIM), lambda b, hh, c: (b * nr + c, col0 // HG_DIM + hh))

    return pl.pallas_call(
        functools.partial(_hgrn_kernel, layer=layer),
        grid=(batch, HG_HEADS, nr),
        in_specs=[col_spec(COL_HQ), col_spec(COL_HF), col_spec(COL_HI), col_spec(COL_HG),
                  pl.BlockSpec((depth, HG_DIM), lambda b, hh, c: (0, hh)),
                  pl.BlockSpec((1, HG_DIM), lambda b, hh, c: (0, hh))],
        out_specs=pl.BlockSpec((rows, HG_DIM), lambda b, hh, c: (b * nr + c, hh)),
        out_shape=jax.ShapeDtypeStruct((batch * seq, HG_WIDTH), BF16),
        scratch_shapes=[pltpu.VMEM((HG_DIM, HG_DIM), F32)],
        compiler_params=_cparams(("parallel", "parallel", "arbitrary")),
        name="hgrn",
    )(h, h, h, h, lb_raw, gn.reshape(1, -1))


def _ca_bias_rows(rel_bias):
    idx = jnp.arange(CA_BIAS_LEN)
    m = jnp.where(idx < CA_WIN, idx, idx - CA_BIAS_LEN)
    bucket = jnp.clip(2 * CA_TQ - m, -CA_REL_CLIP, CA_REL_CLIP) + CA_REL_CLIP
    return rel_bias[:, bucket].astype(F32)


def _ca_kernel(q_ref, k0_ref, k1_ref, k2_ref, v0_ref, v1_ref, v2_ref, brow_ref, gn_ref,
               o_ref, bias_ref):
    t = pl.program_id(1)
    tq = CA_TQ

    log2e = math.log2(math.e)

    @pl.when((pl.program_id(0) == 0) & (t == 0))
    def _():
        qc = lax.broadcasted_iota(jnp.int32, (tq, CA_WIN), 0) // CHUNK
        kk = lax.broadcasted_iota(jnp.int32, (tq, CA_WIN), 1)
        kc = kk // CHUNK
        band = (kc >= qc) & (kc <= qc + CA_LEFT_CHUNKS)
        for h in range(CA_HEADS):
            rows = jnp.broadcast_to(brow_ref[h:h + 1, :], (tq, CA_BIAS_LEN))
            rolled = pltpu.roll(rows, 0, 1, stride=1, stride_axis=0)
            base = jnp.where(band, rolled[:, :CA_WIN] * log2e, MASK_VALUE)
            for v in range(CA_SEQ_VARIANTS):
                first_valid = max(2 * tq - v * tq, 0)
                bias_ref[v, h] = jnp.where(kk >= first_valid, base, MASK_VALUE) if first_valid else base

    variant = jnp.minimum(t, CA_SEQ_VARIANTS - 1)
    c_exp = CA_DIM ** -0.5 * log2e
    ones = jnp.ones((CA_WIN, CA_DIM), BF16)
    outs = []
    for h in range(CA_HEADS):
        c = slice(h * CA_DIM, (h + 1) * CA_DIM)
        qh = (q_ref[:, c] * c_exp).astype(BF16)
        kh = jnp.concatenate([k0_ref[:, c], k1_ref[:, c], k2_ref[:, c]], axis=0).astype(BF16)
        vh = jnp.concatenate([v0_ref[:, c], v1_ref[:, c], v2_ref[:, c]], axis=0).astype(BF16)
        s = lax.dot_general(qh, kh, (((1,), (1,)), ((), ())), preferred_element_type=F32)
        s = s + bias_ref[variant, h]
        p = jnp.exp2(s - jnp.max(s, axis=-1, keepdims=True)).astype(BF16)
        ov = jnp.dot(p, jnp.concatenate([vh, ones], axis=1), preferred_element_type=F32)
        outs.append(ov[:, :CA_DIM] / ov[:, CA_DIM:CA_DIM + 1])
    o = jnp.concatenate(outs, axis=1)
    o_ref[...] = ((o * _rms_scale(o)) * gn_ref[...]).astype(o_ref.dtype)


def _chunk_attn(h, rel_bias, gn, batch, seq):
    tq = CA_TQ
    nq = seq // tq
    cq, ck, cv = COL_AQ // CA_WIDTH, COL_AK // CA_WIDTH, COL_AV // CA_WIDTH

    def kv_spec(col, back):
        return pl.BlockSpec((tq, CA_WIDTH), lambda b, t: (b * nq + jnp.maximum(t - back, 0), col))

    return pl.pallas_call(
        _ca_kernel,
        grid=(batch, nq),
        in_specs=[pl.BlockSpec((tq, CA_WIDTH), lambda b, t: (b * nq + t, cq)),
                  kv_spec(ck, 2), kv_spec(ck, 1), kv_spec(ck, 0),
                  kv_spec(cv, 2), kv_spec(cv, 1), kv_spec(cv, 0),
                  pl.BlockSpec((CA_HEADS, CA_BIAS_LEN), lambda b, t: (0, 0)),
                  pl.BlockSpec((1, CA_WIDTH), lambda b, t: (0, 0))],
        out_specs=pl.BlockSpec((tq, CA_WIDTH), lambda b, t: (b * nq + t, 0)),
        out_shape=jax.ShapeDtypeStruct((batch * seq, CA_WIDTH), BF16),
        scratch_shapes=[pltpu.VMEM((CA_SEQ_VARIANTS, CA_HEADS, tq, CA_WIN), F32)],
        compiler_params=_cparams(("arbitrary", "arbitrary")),
        name="chunk_attn",
    )(h, h, h, h, h, h, h, _ca_bias_rows(rel_bias), gn.reshape(1, -1))


X_SLICES = 8


def _residual_norm(o_ref, xs_ref, g_ref):
    rows, d = o_ref.shape
    w = d // X_SLICES

    step = NORM_ROWS * NORM_UNROLL

    def body(c, carry):
        base = pl.multiple_of(c * step, step)
        chunks = [pl.ds(base + u * NORM_ROWS, NORM_ROWS) for u in range(NORM_UNROLL)]
        ys = [o_ref[r, :] for r in chunks]
        yns = [(y * _rms_scale(y)) * g_ref[...] for y in ys]
        for r, yn in zip(chunks, yns):
            for p in range(X_SLICES):
                o_ref[r, p * w:(p + 1) * w] = xs_ref[p, r, :] + yn[:, p * w:(p + 1) * w]
        return carry

    lax.fori_loop(0, rows // step, body, 0)


def _x_slice_spec(tm, d):
    return pl.BlockSpec((tm, d // X_SLICES), lambda i, k: (i, jnp.minimum(k, X_SLICES - 1)))


def _accumulate_then_residual_norm(a_ref, w_ref, x_ref, g_ref, o_ref, xs_ref, ragged):
    k = pl.program_id(1)
    last = pl.num_programs(1) - 1

    @pl.when(k < X_SLICES)
    def _():
        xs_ref[k] = x_ref[...]

    @pl.when(k == 0)
    def _():
        o_ref[...] = jnp.dot(a_ref[...], w_ref[...], preferred_element_type=F32)

    if ragged is None:
        @pl.when(k > 0)
        def _():
            o_ref[...] += jnp.dot(a_ref[...], w_ref[...], preferred_element_type=F32)
    else:
        @pl.when((k > 0) & (k < last))
        def _():
            o_ref[...] += jnp.dot(a_ref[...], w_ref[...], preferred_element_type=F32)

        @pl.when(k == last)
        def _():
            a, w = a_ref[...], w_ref[...]
            a = jnp.where(lax.broadcasted_iota(jnp.int32, a.shape, 1) < ragged, a, jnp.zeros_like(a))
            w = jnp.where(lax.broadcasted_iota(jnp.int32, w.shape, 0) < ragged, w, jnp.zeros_like(w))
            o_ref[...] += jnp.dot(a, w, preferred_element_type=F32)

    @pl.when(k == last)
    def _():
        _residual_norm(o_ref, xs_ref, g_ref)


OUT_TK = 512


def _out_proj_kernel(mla_ref, hg_ref, ca_ref, w_ref, x_ref, gm_ref, gp_ref, o_ref, a_ref, xs_ref):
    @pl.when(pl.program_id(1) == 0)
    def _():
        rows = mla_ref.shape[0]
        n_mla, n_hg, n_ca = MLA_WIDTH // OUT_TK, HG_WIDTH // OUT_TK, CA_WIDTH // OUT_TK

        def body(c, carry):
            r = pl.ds(pl.multiple_of(c * NORM_ROWS, NORM_ROWS), NORM_ROWS)
            m = mla_ref[r, :]
            mn = ((m * _rms_scale(m)) * gm_ref[...]).astype(BF16)
            for p in range(n_mla):
                a_ref[p, r, :] = mn[:, p * OUT_TK:(p + 1) * OUT_TK]
            return carry

        lax.fori_loop(0, rows // NORM_ROWS, body, 0, unroll=NORM_UNROLL)
        for p in range(n_hg):
            a_ref[n_mla + p] = hg_ref[:, p * OUT_TK:(p + 1) * OUT_TK]
        for p in range(n_ca):
            a_ref[n_mla + n_hg + p] = ca_ref[:, p * OUT_TK:(p + 1) * OUT_TK]

    _accumulate_then_residual_norm(a_ref.at[pl.program_id(1)], w_ref, x_ref, gp_ref, o_ref, xs_ref, None)


def _out_proj(o_mla, o_hg, o_ca, x, g_mla, g_post, w, layer, tm):
    t, d = x.shape
    kdim = w.shape[1]
    nk = kdim // OUT_TK
    assert MLA_WIDTH % OUT_TK == 0 and HG_WIDTH % OUT_TK == 0 and CA_WIDTH % OUT_TK == 0
    assert kdim == MLA_WIDTH + HG_WIDTH + CA_WIDTH and nk >= X_SLICES
    return pl.pallas_call(
        _out_proj_kernel,
        grid=(t // tm, nk),
        in_specs=[pl.BlockSpec((tm, MLA_WIDTH), lambda i, k: (i, 0)),
                  pl.BlockSpec((tm, HG_WIDTH), lambda i, k: (i, 0)),
                  pl.BlockSpec((tm, CA_WIDTH), lambda i, k: (i, 0)),
                  pl.BlockSpec((None, OUT_TK, d), lambda i, k: (layer, k, 0)),
                  _x_slice_spec(tm, d),
                  pl.BlockSpec((1, MLA_WIDTH), lambda i, k: (0, 0)),
                  pl.BlockSpec((1, d), lambda i, k: (0, 0))],
        out_specs=pl.BlockSpec((tm, d), lambda i, k: (i, 0)),
        out_shape=jax.ShapeDtypeStruct((t, d), F32),
        scratch_shapes=[pltpu.VMEM((nk, tm, OUT_TK), BF16),
                        pltpu.VMEM((X_SLICES, tm, d // X_SLICES), F32)],
        compiler_params=_cparams(("parallel", "arbitrary")),
        name="out_proj",
    )(o_mla, o_hg, o_ca, w, x, g_mla.reshape(1, -1), g_post.reshape(1, -1))


def _down_proj_kernel(a_ref, w_ref, x_ref, g_ref, o_ref, xs_ref, *, ragged):
    _accumulate_then_residual_norm(a_ref, w_ref, x_ref, g_ref, o_ref, xs_ref, ragged)


def _down_proj(a, w, x, g, layer, tm, tk):
    t, d = x.shape
    kdim = a.shape[1]
    ragged = kdim % tk or None
    nk = pl.cdiv(kdim, tk)
    assert nk >= X_SLICES
    return pl.pallas_call(
        functools.partial(_down_proj_kernel, ragged=ragged),
        grid=(t // tm, nk),
        in_specs=[pl.BlockSpec((tm, tk), lambda i, k: (i, k)),
                  pl.BlockSpec((None, tk, d), lambda i, k: (layer, k, 0)),
                  _x_slice_spec(tm, d),
                  pl.BlockSpec((1, d), lambda i, k: (0, 0))],
        out_specs=pl.BlockSpec((tm, d), lambda i, k: (i, 0)),
        out_shape=jax.ShapeDtypeStruct((t, d), F32),
        scratch_shapes=[pltpu.VMEM((X_SLICES, tm, d // X_SLICES), F32)],
        compiler_params=_cparams(("parallel", "arbitrary")),
        name="down_proj",
    )(a, w, x, g.reshape(1, -1))


PREP_ROWS = 256


def _prep_w_in(w):
    depth, d, _ = w.shape
    o_ckv = MLA_Q_RANK
    o_kr = o_ckv + MLA_KV_RANK
    o_hg = o_kr + MLA_ROPE
    o_ca = o_hg + 4 * HG_WIDTH
    assert (COL_AQ, COL_AK, COL_AV) == (0, CA_WIDTH, 2 * CA_WIDTH) and COL_HQ == 3 * CA_WIDTH
    assert (COL_HF, COL_HI, COL_HG) == (COL_HQ + HG_WIDTH, COL_HQ + 2 * HG_WIDTH, COL_HQ + 3 * HG_WIDTH)
    assert COL_CKV == COL_HG + HG_WIDTH and COL_CQ == COL_CKV + MLA_KV_RANK and COL_KR == COL_CQ + MLA_Q_RANK
    n_in = w.shape[2]

    def regroup_kernel(w_ref, o_ref):
        x = w_ref[...]
        z = jnp.zeros((x.shape[0], D_IN_PAD - COL_KR - MLA_ROPE), x.dtype)
        o_ref[...] = jnp.concatenate([x[:, o_ca:], x[:, o_hg:o_ca], x[:, o_ckv:o_kr], x[:, :o_ckv],
                                      x[:, o_kr:o_hg], z], axis=1).astype(BF16)

    rows = PREP_ROWS
    return pl.pallas_call(
        regroup_kernel,
        grid=(depth, d // rows),
        in_specs=[pl.BlockSpec((None, rows, n_in), lambda l, i: (l, i, 0))],
        out_specs=pl.BlockSpec((None, rows, D_IN_PAD), lambda l, i: (l, i, 0)),
        out_shape=jax.ShapeDtypeStruct((depth, d, D_IN_PAD), BF16),
        compiler_params=_cparams(("parallel", "parallel")),
        name="w_in_regroup",
    )(w)


def _prep_w_uq(w):
    depth, r, _ = w.shape
    w4 = w.astype(BF16).reshape(depth, r, MLA_HEADS, MLA_NOPE + MLA_ROPE)
    pad = jnp.zeros((depth, r, MLA_HEADS, MLA_QK_PAD - MLA_NOPE - MLA_ROPE), BF16)
    wq = jnp.concatenate([w4, pad], axis=3).reshape(depth, r, MLA_HEADS * MLA_QK_PAD)
    return jnp.swapaxes(wq, 1, 2)


def _prep_w_ukv(w):
    depth, r, _ = w.shape
    w4 = w.astype(BF16).reshape(depth, r, MLA_HEADS, MLA_NOPE + MLA_V)
    wk = w4[:, :, :, :MLA_NOPE].reshape(depth, r, MLA_HEADS * MLA_NOPE)
    wv = w4[:, :, :, MLA_NOPE:].reshape(depth, r, MLA_HEADS * MLA_V)
    return wk, jnp.swapaxes(wv, 1, 2)


def kernel(x, positions, attn_pre_norm, attn_post_norm, w_in, mla_q_norm, mla_kv_norm, w_uq, w_ukv,
           mla_out_norm, hg_lower_bounds, hg_out_norm, ca_rel_bias, ca_out_norm, w_out, ffn_pre_norm,
           ffn_post_norm, w_gate, w_up, w_down):
    batch, seq, d = x.shape
    t = batch * seq
    depth = w_in.shape[0]
    xf = x.reshape(t, d)
    pos = positions.reshape(t, 1)
    w_in_b = _prep_w_in(w_in)
    w_uq_b = _prep_w_uq(w_uq)
    w_uk_b, w_uvt_b = _prep_w_ukv(w_ukv)
    w_out_b, w_gate_b, w_up_b, w_down_b = (w.astype(BF16) for w in (w_out, w_gate, w_up, w_down))
    for l in range(depth):
        h = _norm_matmul(xf, attn_pre_norm[l], [w_in_b], l, F32, tm=1024, tn=512)
        q, k, vt = _mla_proj(h, pos, mla_q_norm[l], mla_kv_norm[l], w_uq_b, w_uk_b, w_uvt_b, l, tm=MLA_TK)
        o_mla = _mla_attn(q, k, vt, batch, seq)
        o_hg = _hgrn(h, hg_lower_bounds, hg_out_norm[l], l, batch, seq)
        o_ca = _chunk_attn(h, ca_rel_bias[l], ca_out_norm[l], batch, seq)
        xf = _out_proj(o_mla, o_hg, o_ca, xf, mla_out_norm[l], attn_post_norm[l], w_out_b, l, tm=512)
        hid = _norm_matmul(xf, ffn_pre_norm[l], [w_gate_b, w_up_b], l, BF16, tm=1024, tn=512)
        xf = _down_proj(hid, w_down_b, xf, ffn_post_norm[l], l, tm=512, tk=1024)
    return xf.reshape(batch, seq, d)
```

```python
import functools
import math

import jax
import jax.numpy as jnp
from jax import lax
from jax.experimental import pallas as pl
from jax.experimental.pallas import tpu as pltpu

F32 = jnp.float32
BF16 = jnp.bfloat16

EPS = 1e-6
MASK_VALUE = -1e30
TINY = 1e-30
CHUNK = 64

MLA_HEADS = 16
MLA_Q_RANK = 768
MLA_KV_RANK = 512
MLA_NOPE = 128
MLA_ROPE = 64
MLA_V = 128
ROPE_THETA = 10000.0
MLA_QK_PAD = 256
MLA_ONES = 16
MLA_VT_ROWS = MLA_V + MLA_ONES
MLA_EXP_SCALE = (MLA_NOPE + MLA_ROPE) ** -0.5 * math.log2(math.e)

HG_HEADS = 8
HG_DIM = 128
HG_BLOCK = 16
HG_CHUNK = 128
HG_ROWS = 512

CA_HEADS = 8
CA_DIM = 128
CA_LEFT_CHUNKS = 8
CA_REL_CLIP = 256
CA_TQ = 256
CA_WIN = 3 * CA_TQ
CA_BIAS_LEN = 1024
CA_SEQ_VARIANTS = 3

MLA_WIDTH = MLA_HEADS * MLA_V
HG_WIDTH = HG_HEADS * HG_DIM
CA_WIDTH = CA_HEADS * CA_DIM

COL_AQ = 0
COL_AK = 1024
COL_AV = 2048
COL_HQ = 3072
COL_HF = 4096
COL_HI = 5120
COL_HG = 6144
COL_CKV = 7168
COL_CQ = 7680
COL_KR = 8448
D_IN_PAD = 8704

LANE = 128
VMEM_LIMIT = 56 * 1024 * 1024


def _cparams(sem, vmem=VMEM_LIMIT):
    return pltpu.CompilerParams(dimension_semantics=sem, vmem_limit_bytes=vmem)


def _rms_scale(x):
    return lax.rsqrt(jnp.mean(x * x, axis=-1, keepdims=True) + EPS)


NORM_ROWS = 16
NORM_UNROLL = 4


def _norm_rows_into(xn_ref, x_ref, g_ref):
    rows = x_ref.shape[0]

    def body(c, carry):
        r = pl.ds(pl.multiple_of(c * NORM_ROWS, NORM_ROWS), NORM_ROWS)
        x = x_ref[r, :]
        xn_ref[r, :] = ((x * _rms_scale(x)) * g_ref[...]).astype(BF16)
        return carry

    lax.fori_loop(0, rows // NORM_ROWS, body, 0, unroll=NORM_UNROLL)


def _norm_matmul_kernel(x_ref, g_ref, w_ref, o_ref, xn_ref):
    @pl.when(pl.program_id(1) == 0)
    def _():
        _norm_rows_into(xn_ref, x_ref, g_ref)

    o_ref[...] = jnp.dot(xn_ref[...], w_ref[...], preferred_element_type=F32).astype(o_ref.dtype)


def _norm_swiglu_kernel(x_ref, g_ref, wg_ref, wu_ref, o_ref, xn_ref):
    @pl.when(pl.program_id(1) == 0)
    def _():
        _norm_rows_into(xn_ref, x_ref, g_ref)

    xn = xn_ref[...]
    gate = jnp.dot(xn, wg_ref[...], preferred_element_type=F32)
    up = jnp.dot(xn, wu_ref[...], preferred_element_type=F32)
    o_ref[...] = ((gate * jax.nn.sigmoid(gate)) * up).astype(o_ref.dtype)


def _norm_matmul(x, g, ws, layer, out_dtype, tm, tn):
    t, d = x.shape
    n = ws[0].shape[2]
    kern = _norm_matmul_kernel if len(ws) == 1 else _norm_swiglu_kernel
    w_specs = [pl.BlockSpec((None, d, tn), lambda i, j: (layer, 0, j)) for _ in ws]
    return pl.pallas_call(
        kern,
        grid=(t // tm, pl.cdiv(n, tn)),
        in_specs=[pl.BlockSpec((tm, d), lambda i, j: (i, 0), pipeline_mode=pl.Buffered(1)),
                  pl.BlockSpec((1, d), lambda i, j: (0, 0))] + w_specs,
        out_specs=pl.BlockSpec((tm, tn), lambda i, j: (i, j)),
        out_shape=jax.ShapeDtypeStruct((t, n), out_dtype),
        scratch_shapes=[pltpu.VMEM((tm, d), BF16)],
        compiler_params=_cparams(("parallel", "arbitrary")),
        name="norm_matmul" if len(ws) == 1 else "norm_swiglu",
    )(x, g.reshape(1, d), *ws)


def _rope_tables(pos_ref):
    lane = lax.broadcasted_iota(jnp.int32, (1, LANE), 1)
    half = MLA_ROPE // 2
    idx = (lane % half).astype(F32)
    inv_freq = jnp.exp((-math.log(ROPE_THETA) * 2.0) * idx / MLA_ROPE)
    inv_freq = jnp.where(lane < MLA_ROPE, inv_freq, 0.0)
    ang = pos_ref[...].astype(F32) * inv_freq
    cos, sin = jnp.cos(ang), jnp.sin(ang)
    sin_hi = jnp.where((lane >= half) & (lane < MLA_ROPE), sin, 0.0)
    sin_lo = jnp.where(lane < half, -sin, 0.0)
    return cos, sin_hi, sin_lo


def _rope(x, tables):
    cos, sin_hi, sin_lo = tables
    half = MLA_ROPE // 2
    return x * cos + pltpu.roll(x, half, 1) * sin_hi + pltpu.roll(x, LANE - half, 1) * sin_lo


def _rope_tables_t(pos_row_ref):
    half = MLA_ROPE // 2
    idx = lax.broadcasted_iota(jnp.int32, (half, 1), 0).astype(F32)
    inv_freq = jnp.exp((-math.log(ROPE_THETA) * 2.0) * idx / MLA_ROPE)
    ang = inv_freq * pos_row_ref[...].astype(F32)
    return jnp.cos(ang), jnp.sin(ang)


def _mla_proj_kernel(cq_ref, ckv_ref, kr_ref, pos_ref, pos_row_ref, gq_ref, gkv_ref, wqt_ref, wk_ref, wvt_ref,
                     qt_ref, k_ref, vt_ref):
    tables = _rope_tables(pos_ref)
    cos_t, sin_t = _rope_tables_t(pos_row_ref)
    cq = cq_ref[...]
    cqn = ((cq * _rms_scale(cq)) * gq_ref[...]).astype(BF16)
    ckv = ckv_ref[...]
    ckvn = ((ckv * _rms_scale(ckv)) * gkv_ref[...]).astype(BF16)
    k_pe = _rope(kr_ref[...], tables).astype(BF16)
    for h in range(MLA_HEADS):
        c0 = h * MLA_QK_PAD
        kh = jnp.dot(ckvn, wk_ref[:, h * LANE:(h + 1) * LANE], preferred_element_type=F32)
        k_ref[:, c0:c0 + LANE] = kh.astype(BF16)
        k_ref[:, c0 + LANE:c0 + MLA_QK_PAD] = k_pe
    group = 4
    half = MLA_ROPE // 2
    nt_dims = (((1,), (1,)), ((), ()))
    for h0 in range(0, MLA_HEADS, group):
        qt = lax.dot_general(wqt_ref[h0 * MLA_QK_PAD:(h0 + group) * MLA_QK_PAD, :], cqn, nt_dims,
                             preferred_element_type=F32) * MLA_EXP_SCALE
        for h in range(h0, h0 + group):
            r0, g0 = h * MLA_QK_PAD, (h - h0) * MLA_QK_PAD
            t1 = qt[g0 + MLA_NOPE:g0 + MLA_NOPE + half, :]
            t2 = qt[g0 + MLA_NOPE + half:g0 + MLA_NOPE + MLA_ROPE, :]
            qt_ref[r0:r0 + MLA_NOPE, :] = qt[g0:g0 + MLA_NOPE, :].astype(BF16)
            qt_ref[r0 + MLA_NOPE:r0 + MLA_NOPE + half, :] = (t1 * cos_t - t2 * sin_t).astype(BF16)
            qt_ref[r0 + MLA_NOPE + half:r0 + MLA_NOPE + MLA_ROPE, :] = (t1 * sin_t + t2 * cos_t).astype(BF16)
            qt_ref[r0 + MLA_NOPE + MLA_ROPE:r0 + MLA_QK_PAD, :] = (
                qt[g0 + MLA_NOPE + MLA_ROPE:g0 + MLA_QK_PAD, :].astype(BF16))
    ones = jnp.ones((MLA_ONES, vt_ref.shape[2]), BF16)
    for h0 in range(0, MLA_HEADS, group):
        vt = lax.dot_general(wvt_ref[h0 * MLA_V:(h0 + group) * MLA_V, :], ckvn, nt_dims,
                             preferred_element_type=F32).astype(BF16)
        for h in range(h0, h0 + group):
            r0 = h * MLA_VT_ROWS
            vt_ref[0, r0:r0 + MLA_V, :] = vt[(h - h0) * MLA_V:(h - h0 + 1) * MLA_V, :]
            vt_ref[0, r0 + MLA_V:r0 + MLA_VT_ROWS, :] = ones


def _mla_proj(h, positions, gq, gkv, wqt, wk, wvt, layer, tm):
    t = h.shape[0]
    qk_w = MLA_HEADS * MLA_QK_PAD
    const = lambda i: (0, 0)
    return pl.pallas_call(
        _mla_proj_kernel,
        grid=(t // tm,),
        in_specs=[pl.BlockSpec((tm, MLA_Q_RANK), lambda i: (i, COL_CQ // MLA_Q_RANK)),
                  pl.BlockSpec((tm, MLA_KV_RANK), lambda i: (i, COL_CKV // MLA_KV_RANK)),
                  pl.BlockSpec((tm, LANE), lambda i: (i, COL_KR // LANE)),
                  pl.BlockSpec((tm, 1), lambda i: (i, 0)),
                  pl.BlockSpec((1, tm), lambda i: (0, i)),
                  pl.BlockSpec((1, MLA_Q_RANK), const),
                  pl.BlockSpec((1, MLA_KV_RANK), const),
                  pl.BlockSpec((None,) + wqt.shape[1:], lambda i: (layer, 0, 0)),
                  pl.BlockSpec((None,) + wk.shape[1:], lambda i: (layer, 0, 0)),
                  pl.BlockSpec((None,) + wvt.shape[1:], lambda i: (layer, 0, 0))],
        out_specs=[pl.BlockSpec((qk_w, tm), lambda i: (0, i)),
                   pl.BlockSpec((tm, qk_w), lambda i: (i, 0)),
                   pl.BlockSpec((1, MLA_HEADS * MLA_VT_ROWS, tm), lambda i: (i, 0, 0))],
        out_shape=[jax.ShapeDtypeStruct((qk_w, t), BF16),
                   jax.ShapeDtypeStruct((t, qk_w), BF16),
                   jax.ShapeDtypeStruct((t // tm, MLA_HEADS * MLA_VT_ROWS, tm), BF16)],
        compiler_params=_cparams(("parallel",)),
        name="mla_proj",
    )(h, h, h, positions, positions.reshape(1, t), gq.reshape(1, -1), gkv.reshape(1, -1), wqt, wk, wvt)


MLA_TQ = 2048
MLA_TK = 512
MLA_QBLK = 256
MLA_KBLK = 128


def _mla_attn_kernel(qt_ref, k_ref, vt_ref, o_ref, sa_ref, sb_ref, xa_ref, xb_ref, pa_ref, pb_ref,
                     m_ref, alpha_ref, acc_ref):
    i = pl.program_id(2)
    tq, tk = MLA_TQ, MLA_TK
    m_ref[...] = jnp.full(m_ref.shape, -jnp.inf, F32)
    alpha_ref[...] = jnp.ones(alpha_ref.shape, F32)
    acc_ref[...] = jnp.zeros(acc_ref.shape, F32)
    pb_ref[...] = jnp.zeros(pb_ref.shape, BF16)

    def scores_into(s_ref, x_ref, t, q_lo=0):
        r = pl.ds(pl.multiple_of(t * tk, tk), tk)
        s = jnp.dot(k_ref[r, :], qt_ref[:, q_lo:], preferred_element_type=F32)
        s_ref[:, q_lo:] = s
        x_ref[:, q_lo:] = jnp.max(s, axis=0, keepdims=True)

    def add_values(p_ref, vt_tile, q_lo=0):
        acc_ref[:, q_lo:] = alpha_ref[:, q_lo:] * acc_ref[:, q_lo:] + jnp.dot(
            vt_tile, p_ref[:, q_lo:], preferred_element_type=F32)

    def softmax_terms(s_ref, x_ref, p_ref, chunk_shift, q_lo=0):
        for q0 in range(q_lo, tq, MLA_QBLK):
            cols = slice(q0, q0 + MLA_QBLK)

            def masked(s, k0):
                kc = (lax.broadcasted_iota(jnp.int32, s.shape, 0) + k0) // CHUNK + chunk_shift
                qc = (lax.broadcasted_iota(jnp.int32, s.shape, 1) + q0) // CHUNK
                return jnp.where(kc <= qc, s, MASK_VALUE)

            def visibility(k0):
                if chunk_shift is None:
                    return "all"
                k_min, k_max = k0 // CHUNK + chunk_shift, (k0 + MLA_KBLK - 1) // CHUNK + chunk_shift
                q_min, q_max = q0 // CHUNK, (q0 + MLA_QBLK - 1) // CHUNK
                return "all" if k_max <= q_min else "none" if k_min > q_max else "some"

            key_blocks = [(k0, visibility(k0)) for k0 in range(0, tk, MLA_KBLK)]
            if all(vis == "all" for _, vis in key_blocks):
                tile_max = x_ref[:, cols]
            else:
                tile_max = functools.reduce(jnp.maximum, [
                    jnp.max(s_ref[k0:k0 + MLA_KBLK, cols] if vis == "all"
                            else masked(s_ref[k0:k0 + MLA_KBLK, cols], k0), axis=0, keepdims=True)
                    for k0, vis in key_blocks if vis != "none"])
            m_old = m_ref[:, cols]
            m_new = jnp.maximum(m_old, tile_max)
            m_ref[:, cols] = m_new
            alpha_ref[:, cols] = jnp.exp2(m_old - m_new)
            for k0, vis in key_blocks:
                rows = slice(k0, k0 + MLA_KBLK)
                if vis == "none":
                    p_ref[rows, cols] = jnp.zeros((MLA_KBLK, MLA_QBLK), BF16)
                    continue
                s = s_ref[rows, cols]
                if vis == "some":
                    s = masked(s, k0)
                p_ref[rows, cols] = jnp.exp2(s - m_new).astype(BF16)

    nt = tq // tk
    first_diag = nt * i

    def pair(u, d):
        def lo(dd):
            return 0 if dd is None or dd < 0 else dd * tk

        def shift(dd):
            return None if dd is None else dd * (tk // CHUNK)

        d_odd = None if d is None else d + 1
        d_prev = None if d is None else d - 1
        scores_into(sb_ref, xb_ref, 2 * u + 1, lo(d_odd))
        add_values(pb_ref, vt_ref[jnp.maximum(2 * u - 1, 0)], lo(d_prev))
        softmax_terms(sa_ref, xa_ref, pa_ref, shift(d), lo(d))
        if d is None:
            scores_into(sa_ref, xa_ref, 2 * u + 2)
        elif d + 2 < nt:
            scores_into(sa_ref, xa_ref, 2 * u + 2, lo(d + 2))
        add_values(pa_ref, vt_ref[2 * u], lo(d))
        softmax_terms(sb_ref, xb_ref, pb_ref, shift(d_odd), lo(d_odd))

    scores_into(sa_ref, xa_ref, 0)

    def body(u, carry):
        pair(u, None)
        return carry

    lax.fori_loop(0, first_diag // 2, body, 0)
    for d in range(0, nt, 2):
        pair((first_diag + d) // 2, d)
    add_values(pb_ref, vt_ref[first_diag + nt - 1], (nt - 1) * tk)
    o_ref[...] = (acc_ref[:MLA_V, :] / acc_ref[MLA_V:MLA_V + 1, :]).T


def _mla_attn(qt, k, vt, batch, seq):
    tq = MLA_TQ
    nq = seq // tq
    assert vt.shape[2] == MLA_TK
    return pl.pallas_call(
        _mla_attn_kernel,
        grid=(batch, MLA_HEADS, nq),
        in_specs=[pl.BlockSpec((MLA_QK_PAD, tq), lambda b, h, i: (h, b * nq + i)),
                  pl.BlockSpec((seq, MLA_QK_PAD), lambda b, h, i: (b, h)),
                  pl.BlockSpec((seq // MLA_TK, MLA_VT_ROWS, MLA_TK), lambda b, h, i: (b, h, 0))],
        out_specs=pl.BlockSpec((tq, MLA_V), lambda b, h, i: (b * nq + i, h)),
        out_shape=jax.ShapeDtypeStruct((batch * seq, MLA_WIDTH), F32),
        scratch_shapes=[pltpu.VMEM((MLA_TK, tq), F32), pltpu.VMEM((MLA_TK, tq), F32),
                        pltpu.VMEM((1, tq), F32), pltpu.VMEM((1, tq), F32),
                        pltpu.VMEM((MLA_TK, tq), BF16), pltpu.VMEM((MLA_TK, tq), BF16),
                        pltpu.VMEM((1, tq), F32), pltpu.VMEM((1, tq), F32),
                        pltpu.VMEM((MLA_VT_ROWS, tq), F32)],
        compiler_params=_cparams(("parallel", "parallel", "arbitrary")),
        name="mla_attn",
    )(qt, k, vt)


def _block_cumsum(x, row):
    r = row % HG_BLOCK
    s = 1
    while s < HG_BLOCK:
        x = x + jnp.where(r >= s, pltpu.roll(x, s, 0), 0.0)
        s *= 2
    return x


def _hgrn_chunk(hq, hf, hi, lb, state):
    c = HG_CHUNK
    nb = c // HG_BLOCK
    row = lax.broadcasted_iota(jnp.int32, (c, HG_DIM), 0)
    col = lax.broadcasted_iota(jnp.int32, (c, HG_DIM), 1)

    q = hq * jax.nn.sigmoid(hq)
    f = lb + (1.0 - lb) * jax.nn.sigmoid(hf)
    k = (1.0 - lb) * jax.nn.sigmoid(-hf)
    b = _block_cumsum(jnp.log(jnp.maximum(f, TINY)), row)
    b3 = b.reshape(nb, HG_BLOCK, HG_DIM)
    b_last3 = jnp.broadcast_to(b3[:, HG_BLOCK - 1:HG_BLOCK, :], b3.shape)
    b_last = b_last3.reshape(c, HG_DIM)

    q3 = q.reshape(nb, HG_BLOCK, HG_DIM)
    k3 = k.reshape(nb, HG_BLOCK, HG_DIM)
    v3 = hi.reshape(nb, HG_BLOCK, HG_DIM)
    irow = lax.broadcasted_iota(jnp.int32, b3.shape, 1)
    b3_log2 = b3 * math.log2(math.e)
    sub = 8
    ws = []
    for j in range(HG_BLOCK):
        lo = (j // sub) * sub
        w = q3[:, lo:, :] * k3[:, j:j + 1, :] * jnp.exp2(b3_log2[:, lo:, :] - b3_log2[:, j:j + 1, :])
        if j > lo:
            w = jnp.where(irow[:, lo:, :] >= j, w, 0.0)
        ws.append(w.reshape(nb * (HG_BLOCK - lo), HG_DIM))
    sums = jnp.dot(jnp.concatenate(ws, axis=0).astype(BF16), jnp.ones((HG_DIM, HG_DIM), BF16),
                   preferred_element_type=F32)
    o_parts = [jnp.zeros((nb, sub, HG_DIM), F32) for _ in range(HG_BLOCK // sub)]
    r_at = 0
    for j in range(HG_BLOCK):
        lo = (j // sub) * sub
        n_rows = nb * (HG_BLOCK - lo)
        a_j = sums[r_at:r_at + n_rows, :].reshape(nb, HG_BLOCK - lo, HG_DIM)
        r_at += n_rows
        wv = a_j * v3[:, j:j + 1, :]
        for part in range(lo // sub, HG_BLOCK // sub):
            r0 = part * sub - lo
            o_parts[part] = o_parts[part] + wv[:, r0:r0 + sub, :]
    o = jnp.concatenate(o_parts, axis=1).reshape(c, HG_DIM)

    q_dec = q * jnp.exp(b)
    k_dec_t = (k * jnp.exp(b_last - b)).T
    b_last_t = b_last.T
    v_bf = hi.astype(BF16)
    blk_of_col = col // HG_BLOCK
    k_stack = jnp.concatenate(
        [jnp.where(blk_of_col == j, k_dec_t, 0.0) for j in range(nb)], axis=0).astype(BF16)
    u_all = jnp.dot(k_stack, v_bf, preferred_element_type=F32)
    states = []
    for j in range(nb):
        states.append(state.astype(BF16))
        decay = jnp.exp(b_last_t[:, j * HG_BLOCK:j * HG_BLOCK + 1])
        state = decay * state + u_all[j * HG_DIM:(j + 1) * HG_DIM, :]
    s_stack = jnp.concatenate(states, axis=0)
    blk_of_row = row // HG_BLOCK
    q_exp = jnp.concatenate(
        [jnp.where(blk_of_row == j, q_dec, 0.0) for j in range(nb)], axis=1).astype(BF16)
    o = o + jnp.dot(q_exp, s_stack, preferred_element_type=F32)
    return o, state


def _hgrn_kernel(hq_ref, hf_ref, hi_ref, hg_ref, lbraw_ref, gn_ref, o_ref, state_ref, *, layer):
    @pl.when(pl.program_id(2) == 0)
    def _():
        state_ref[...] = jnp.zeros(state_ref.shape, F32)

    raw = lbraw_ref[...]
    e = jnp.exp(raw - jnp.max(raw, axis=0, keepdims=True))
    p = e / jnp.sum(e, axis=0, keepdims=True)
    lb = jnp.sum(p[:layer + 1, :], axis=0, keepdims=True) - p[0:1, :]

    def body(ci, carry):
        r = pl.ds(pl.multiple_of(ci * HG_CHUNK, HG_CHUNK), HG_CHUNK)
        o, state = _hgrn_chunk(hq_ref[r, :], hf_ref[r, :], hi_ref[r, :], lb, state_ref[...])
        state_ref[...] = state
        o = (o * _rms_scale(o)) * gn_ref[...]
        g = hg_ref[r, :]
        o_ref[r, :] = (o * (g * jax.nn.sigmoid(g))).astype(o_ref.dtype)
        return carry

    lax.fori_loop(0, hq_ref.shape[0] // HG_CHUNK, body, 0, unroll=2)


def _hgrn(h, lb_raw, gn, layer, batch, seq):
    rows = HG_ROWS
    nr = seq // rows
    depth = lb_raw.shape[0]

    def col_spec(col0):
        return pl.BlockSpec((rows, HG_DIM), lambda b, hh, c: (b * nr + c, col0 // HG_DIM + hh))

    return pl.pallas_call(
        functools.partial(_hgrn_kernel, layer=layer),
        grid=(batch, HG_HEADS, nr),
        in_specs=[col_spec(COL_HQ), col_spec(COL_HF), col_spec(COL_HI), col_spec(COL_HG),
                  pl.BlockSpec((depth, HG_DIM), lambda b, hh, c: (0, hh)),
                  pl.BlockSpec((1, HG_DIM), lambda b, hh, c: (0, hh))],
        out_specs=pl.BlockSpec((rows, HG_DIM), lambda b, hh, c: (b * nr + c, hh)),
        out_shape=jax.ShapeDtypeStruct((batch * seq, HG_WIDTH), BF16),
        scratch_shapes=[pltpu.VMEM((HG_DIM, HG_DIM), F32)],
        compiler_params=_cparams(("parallel", "parallel", "arbitrary")),
        name="hgrn",
    )(h, h, h, h, lb_raw, gn.reshape(1, -1))


def _ca_bias_rows(rel_bias):
    idx = jnp.arange(CA_BIAS_LEN)
    m = jnp.where(idx < CA_WIN, idx, idx - CA_BIAS_LEN)
    bucket = jnp.clip(2 * CA_TQ - m, -CA_REL_CLIP, CA_REL_CLIP) + CA_REL_CLIP
    return rel_bias[:, bucket].astype(F32)


def _ca_kernel(q_ref, k0_ref, k1_ref, k2_ref, v0_ref, v1_ref, v2_ref, brow_ref, gn_ref,
               o_ref, bias_ref):
    t = pl.program_id(1)
    tq = CA_TQ

    log2e = math.log2(math.e)

    @pl.when((pl.program_id(0) == 0) & (t == 0))
    def _():
        qc = lax.broadcasted_iota(jnp.int32, (tq, CA_WIN), 0) // CHUNK
        kk = lax.broadcasted_iota(jnp.int32, (tq, CA_WIN), 1)
        kc = kk // CHUNK
        band = (kc >= qc) & (kc <= qc + CA_LEFT_CHUNKS)
        for h in range(CA_HEADS):
            rows = jnp.broadcast_to(brow_ref[h:h + 1, :], (tq, CA_BIAS_LEN))
            rolled = pltpu.roll(rows, 0, 1, stride=1, stride_axis=0)
            base = jnp.where(band, rolled[:, :CA_WIN] * log2e, MASK_VALUE)
            for v in range(CA_SEQ_VARIANTS):
                first_valid = max(2 * tq - v * tq, 0)
                bias_ref[v, h] = jnp.where(kk >= first_valid, base, MASK_VALUE) if first_valid else base

    variant = jnp.minimum(t, CA_SEQ_VARIANTS - 1)
    c_exp = CA_DIM ** -0.5 * log2e
    ones = jnp.ones((CA_WIN, CA_DIM), BF16)
    outs = []
    for h in range(CA_HEADS):
        c = slice(h * CA_DIM, (h + 1) * CA_DIM)
        qh = (q_ref[:, c] * c_exp).astype(BF16)
        kh = jnp.concatenate([k0_ref[:, c], k1_ref[:, c], k2_ref[:, c]], axis=0).astype(BF16)
        vh = jnp.concatenate([v0_ref[:, c], v1_ref[:, c], v2_ref[:, c]], axis=0).astype(BF16)
        s = lax.dot_general(qh, kh, (((1,), (1,)), ((), ())), preferred_element_type=F32)
        s = s + bias_ref[variant, h]
        p = jnp.exp2(s - jnp.max(s, axis=-1, keepdims=True)).astype(BF16)
        ov = jnp.dot(p, jnp.concatenate([vh, ones], axis=1), preferred_element_type=F32)
        outs.append(ov[:, :CA_DIM] / ov[:, CA_DIM:CA_DIM + 1])
    o = jnp.concatenate(outs, axis=1)
    o_ref[...] = ((o * _rms_scale(o)) * gn_ref[...]).astype(o_ref.dtype)


def _chunk_attn(h, rel_bias, gn, batch, seq):
    tq = CA_TQ
    nq = seq // tq
    cq, ck, cv = COL_AQ // CA_WIDTH, COL_AK // CA_WIDTH, COL_AV // CA_WIDTH

    def kv_spec(col, back):
        return pl.BlockSpec((tq, CA_WIDTH), lambda b, t: (b * nq + jnp.maximum(t - back, 0), col))

    return pl.pallas_call(
        _ca_kernel,
        grid=(batch, nq),
        in_specs=[pl.BlockSpec((tq, CA_WIDTH), lambda b, t: (b * nq + t, cq)),
                  kv_spec(ck, 2), kv_spec(ck, 1), kv_spec(ck, 0),
                  kv_spec(cv, 2), kv_spec(cv, 1), kv_spec(cv, 0),
                  pl.BlockSpec((CA_HEADS, CA_BIAS_LEN), lambda b, t: (0, 0)),
                  pl.BlockSpec((1, CA_WIDTH), lambda b, t: (0, 0))],
        out_specs=pl.BlockSpec((tq, CA_WIDTH), lambda b, t: (b * nq + t, 0)),
        out_shape=jax.ShapeDtypeStruct((batch * seq, CA_WIDTH), BF16),
        scratch_shapes=[pltpu.VMEM((CA_SEQ_VARIANTS, CA_HEADS, tq, CA_WIN), F32)],
        compiler_params=_cparams(("arbitrary", "arbitrary")),
        name="chunk_attn",
    )(h, h, h, h, h, h, h, _ca_bias_rows(rel_bias), gn.reshape(1, -1))


X_SLICES = 8


def _residual_norm(o_ref, xs_ref, g_ref):
    rows, d = o_ref.shape
    w = d // X_SLICES

    step = NORM_ROWS * NORM_UNROLL

    def body(c, carry):
        base = pl.multiple_of(c * step, step)
        chunks = [pl.ds(base + u * NORM_ROWS, NORM_ROWS) for u in range(NORM_UNROLL)]
        ys = [o_ref[r, :] for r in chunks]
        yns = [(y * _rms_scale(y)) * g_ref[...] for y in ys]
        for r, yn in zip(chunks, yns):
            for p in range(X_SLICES):
                o_ref[r, p * w:(p + 1) * w] = xs_ref[p, r, :] + yn[:, p * w:(p + 1) * w]
        return carry

    lax.fori_loop(0, rows // step, body, 0)


def _x_slice_spec(tm, d):
    return pl.BlockSpec((tm, d // X_SLICES), lambda i, k: (i, jnp.minimum(k, X_SLICES - 1)))


def _accumulate_then_residual_norm(a_ref, w_ref, x_ref, g_ref, o_ref, xs_ref, ragged):
    k = pl.program_id(1)
    last = pl.num_programs(1) - 1

    @pl.when(k < X_SLICES)
    def _():
        xs_ref[k] = x_ref[...]

    @pl.when(k == 0)
    def _():
        o_ref[...] = jnp.dot(a_ref[...], w_ref[...], preferred_element_type=F32)

    if ragged is None:
        @pl.when(k > 0)
        def _():
            o_ref[...] += jnp.dot(a_ref[...], w_ref[...], preferred_element_type=F32)
    else:
        @pl.when((k > 0) & (k < last))
        def _():
            o_ref[...] += jnp.dot(a_ref[...], w_ref[...], preferred_element_type=F32)

        @pl.when(k == last)
        def _():
            a, w = a_ref[...], w_ref[...]
            a = jnp.where(lax.broadcasted_iota(jnp.int32, a.shape, 1) < ragged, a, jnp.zeros_like(a))
            w = jnp.where(lax.broadcasted_iota(jnp.int32, w.shape, 0) < ragged, w, jnp.zeros_like(w))
            o_ref[...] += jnp.dot(a, w, preferred_element_type=F32)

    @pl.when(k == last)
    def _():
        _residual_norm(o_ref, xs_ref, g_ref)


OUT_TK = 512


def _out_proj_kernel(mla_ref, hg_ref, ca_ref, w_ref, x_ref, gm_ref, gp_ref, o_ref, a_ref, xs_ref):
    @pl.when(pl.program_id(1) == 0)
    def _():
        rows = mla_ref.shape[0]
        n_mla, n_hg, n_ca = MLA_WIDTH // OUT_TK, HG_WIDTH // OUT_TK, CA_WIDTH // OUT_TK

        def body(c, carry):
            r = pl.ds(pl.multiple_of(c * NORM_ROWS, NORM_ROWS), NORM_ROWS)
            m = mla_ref[r, :]
            mn = ((m * _rms_scale(m)) * gm_ref[...]).astype(BF16)
            for p in range(n_mla):
                a_ref[p, r, :] = mn[:, p * OUT_TK:(p + 1) * OUT_TK]
            return carry

        lax.fori_loop(0, rows // NORM_ROWS, body, 0, unroll=NORM_UNROLL)
        for p in range(n_hg):
            a_ref[n_mla + p] = hg_ref[:, p * OUT_TK:(p + 1) * OUT_TK]
        for p in range(n_ca):
            a_ref[n_mla + n_hg + p] = ca_ref[:, p * OUT_TK:(p + 1) * OUT_TK]

    _accumulate_then_residual_norm(a_ref.at[pl.program_id(1)], w_ref, x_ref, gp_ref, o_ref, xs_ref, None)


def _out_proj(o_mla, o_hg, o_ca, x, g_mla, g_post, w, layer, tm):
    t, d = x.shape
    kdim = w.shape[1]
    nk = kdim // OUT_TK
    assert MLA_WIDTH % OUT_TK == 0 and HG_WIDTH % OUT_TK == 0 and CA_WIDTH % OUT_TK == 0
    assert kdim == MLA_WIDTH + HG_WIDTH + CA_WIDTH and nk >= X_SLICES
    return pl.pallas_call(
        _out_proj_kernel,
        grid=(t // tm, nk),
        in_specs=[pl.BlockSpec((tm, MLA_WIDTH), lambda i, k: (i, 0)),
                  pl.BlockSpec((tm, HG_WIDTH), lambda i, k: (i, 0)),
                  pl.BlockSpec((tm, CA_WIDTH), lambda i, k: (i, 0)),
                  pl.BlockSpec((None, OUT_TK, d), lambda i, k: (layer, k, 0)),
                  _x_slice_spec(tm, d),
                  pl.BlockSpec((1, MLA_WIDTH), lambda i, k: (0, 0)),
                  pl.BlockSpec((1, d), lambda i, k: (0, 0))],
        out_specs=pl.BlockSpec((tm, d), lambda i, k: (i, 0)),
        out_shape=jax.ShapeDtypeStruct((t, d), F32),
        scratch_shapes=[pltpu.VMEM((nk, tm, OUT_TK), BF16),
                        pltpu.VMEM((X_SLICES, tm, d // X_SLICES), F32)],
        compiler_params=_cparams(("parallel", "arbitrary")),
        name="out_proj",
    )(o_mla, o_hg, o_ca, w, x, g_mla.reshape(1, -1), g_post.reshape(1, -1))


def _down_proj_kernel(a_ref, w_ref, x_ref, g_ref, o_ref, xs_ref, *, ragged):
    _accumulate_then_residual_norm(a_ref, w_ref, x_ref, g_ref, o_ref, xs_ref, ragged)


def _down_proj(a, w, x, g, layer, tm, tk):
    t, d = x.shape
    kdim = a.shape[1]
    ragged = kdim % tk or None
    nk = pl.cdiv(kdim, tk)
    assert nk >= X_SLICES
    return pl.pallas_call(
        functools.partial(_down_proj_kernel, ragged=ragged),
        grid=(t // tm, nk),
        in_specs=[pl.BlockSpec((tm, tk), lambda i, k: (i, k)),
                  pl.BlockSpec((None, tk, d), lambda i, k: (layer, k, 0)),
                  _x_slice_spec(tm, d),
                  pl.BlockSpec((1, d), lambda i, k: (0, 0))],
        out_specs=pl.BlockSpec((tm, d), lambda i, k: (i, 0)),
        out_shape=jax.ShapeDtypeStruct((t, d), F32),
        scratch_shapes=[pltpu.VMEM((X_SLICES, tm, d // X_SLICES), F32)],
        compiler_params=_cparams(("parallel", "arbitrary")),
        name="down_proj",
    )(a, w, x, g.reshape(1, -1))


PREP_ROWS = 256


def _prep_w_in(w):
    depth, d, _ = w.shape
    o_ckv = MLA_Q_RANK
    o_kr = o_ckv + MLA_KV_RANK
    o_hg = o_kr + MLA_ROPE
    o_ca = o_hg + 4 * HG_WIDTH
    assert (COL_AQ, COL_AK, COL_AV) == (0, CA_WIDTH, 2 * CA_WIDTH) and COL_HQ == 3 * CA_WIDTH
    assert (COL_HF, COL_HI, COL_HG) == (COL_HQ + HG_WIDTH, COL_HQ + 2 * HG_WIDTH, COL_HQ + 3 * HG_WIDTH)
    assert COL_CKV == COL_HG + HG_WIDTH and COL_CQ == COL_CKV + MLA_KV_RANK and COL_KR == COL_CQ + MLA_Q_RANK
    n_in = w.shape[2]

    def regroup_kernel(w_ref, o_ref):
        x = w_ref[...]
        z = jnp.zeros((x.shape[0], D_IN_PAD - COL_KR - MLA_ROPE), x.dtype)
        o_ref[...] = jnp.concatenate([x[:, o_ca:], x[:, o_hg:o_ca], x[:, o_ckv:o_kr], x[:, :o_ckv],
                                      x[:, o_kr:o_hg], z], axis=1).astype(BF16)

    rows = PREP_ROWS
    return pl.pallas_call(
        regroup_kernel,
        grid=(depth, d // rows),
        in_specs=[pl.BlockSpec((None, rows, n_in), lambda l, i: (l, i, 0))],
        out_specs=pl.BlockSpec((None, rows, D_IN_PAD), lambda l, i: (l, i, 0)),
        out_shape=jax.ShapeDtypeStruct((depth, d, D_IN_PAD), BF16),
        compiler_params=_cparams(("parallel", "parallel")),
        name="w_in_regroup",
    )(w)


def _prep_w_uq(w):
    depth, r, _ = w.shape
    w4 = w.astype(BF16).reshape(depth, r, MLA_HEADS, MLA_NOPE + MLA_ROPE)
    pad = jnp.zeros((depth, r, MLA_HEADS, MLA_QK_PAD - MLA_NOPE - MLA_ROPE), BF16)
    wq = jnp.concatenate([w4, pad], axis=3).reshape(depth, r, MLA_HEADS * MLA_QK_PAD)
    return jnp.swapaxes(wq, 1, 2)


def _prep_w_ukv(w):
    depth, r, _ = w.shape
    w4 = w.astype(BF16).reshape(depth, r, MLA_HEADS, MLA_NOPE + MLA_V)
    wk = w4[:, :, :, :MLA_NOPE].reshape(depth, r, MLA_HEADS * MLA_NOPE)
    wv = w4[:, :, :, MLA_NOPE:].reshape(depth, r, MLA_HEADS * MLA_V)
    return wk, jnp.swapaxes(wv, 1, 2)


def kernel(x, positions, attn_pre_norm, attn_post_norm, w_in, mla_q_norm, mla_kv_norm, w_uq, w_ukv,
           mla_out_norm, hg_lower_bounds, hg_out_norm, ca_rel_bias, ca_out_norm, w_out, ffn_pre_norm,
           ffn_post_norm, w_gate, w_up, w_down):
    batch, seq, d = x.shape
    t = batch * seq
    depth = w_in.shape[0]
    xf = x.reshape(t, d)
    pos = positions.reshape(t, 1)
    w_in_b = _prep_w_in(w_in)
    w_uq_b = _prep_w_uq(w_uq)
    w_uk_b, w_uvt_b = _prep_w_ukv(w_ukv)
    w_out_b, w_gate_b, w_up_b, w_down_b = (w.astype(BF16) for w in (w_out, w_gate, w_up, w_down))
    for l in range(depth):
        h = _norm_matmul(xf, attn_pre_norm[l], [w_in_b], l, F32, tm=1024, tn=512)
        q, k, vt = _mla_proj(h, pos, mla_q_norm[l], mla_kv_norm[l], w_uq_b, w_uk_b, w_uvt_b, l, tm=MLA_TK)
        o_mla = _mla_attn(q, k, vt, batch, seq)
        o_hg = _hgrn(h, hg_lower_bounds, hg_out_norm[l], l, batch, seq)
        o_ca = _chunk_attn(h, ca_rel_bias[l], ca_out_norm[l], batch, seq)
        xf = _out_proj(o_mla, o_hg, o_ca, xf, mla_out_norm[l], attn_post_norm[l], w_out_b, l, tm=512)
        hid = _norm_matmul(xf, ffn_pre_norm[l], [w_gate_b, w_up_b], l, BF16, tm=1024, tn=512)
        xf = _down_proj(hid, w_down_b, xf, ffn_post_norm[l], l, tm=512, tk=1024)
    return xf.reshape(batch, seq, d)
```

```python
import functools
import math

import jax
import jax.numpy as jnp
from jax import lax
from jax.experimental import pallas as pl
from jax.experimental.pallas import tpu as pltpu

F32 = jnp.float32
BF16 = jnp.bfloat16

EPS = 1e-6
MASK_VALUE = -1e30
TINY = 1e-30
CHUNK = 64

MLA_HEADS = 16
MLA_Q_RANK = 768
MLA_KV_RANK = 512
MLA_NOPE = 128
MLA_ROPE = 64
MLA_V = 128
ROPE_THETA = 10000.0
MLA_QK_PAD = 256
MLA_ONES = 16
MLA_VT_ROWS = MLA_V + MLA_ONES
MLA_EXP_SCALE = (MLA_NOPE + MLA_ROPE) ** -0.5 * math.log2(math.e)

HG_HEADS = 8
HG_DIM = 128
HG_BLOCK = 16
HG_CHUNK = 128
HG_ROWS = 1024

CA_HEADS = 8
CA_DIM = 128
CA_LEFT_CHUNKS = 8
CA_REL_CLIP = 256
CA_TQ = 256
CA_WIN = 3 * CA_TQ
CA_BIAS_LEN = 1024
CA_SEQ_VARIANTS = 3

MLA_WIDTH = MLA_HEADS * MLA_V
HG_WIDTH = HG_HEADS * HG_DIM
CA_WIDTH = CA_HEADS * CA_DIM

COL_AQ = 0
COL_AK = 1024
COL_AV = 2048
COL_HQ = 3072
COL_HF = 4096
COL_HI = 5120
COL_HG = 6144
COL_CKV = 7168
COL_CQ = 7680
COL_KR = 8448
D_IN_PAD = 8704

LANE = 128
SUBLANE = 8
V7X_VMEM_BYTES = 64 * 1024 * 1024
VMEM_LIMIT = V7X_VMEM_BYTES * 7 // 8

NORM_MATMUL_TM = 1024
NORM_MATMUL_TN = 512
RESIDUAL_TM = 512
DOWN_TK = 1024


def _cparams(sem):
    return pltpu.CompilerParams(dimension_semantics=sem, vmem_limit_bytes=VMEM_LIMIT)


def _rms_scale(x):
    return lax.rsqrt(jnp.mean(x * x, axis=-1, keepdims=True) + EPS)


NORM_ROWS = 16
NORM_UNROLL = 4


def _norm_rows_into(xn_ref, x_ref, g_ref):
    rows = x_ref.shape[0]

    def body(c, carry):
        r = pl.ds(pl.multiple_of(c * NORM_ROWS, NORM_ROWS), NORM_ROWS)
        x = x_ref[r, :]
        xn_ref[r, :] = ((x * _rms_scale(x)) * g_ref[...]).astype(BF16)
        return carry

    lax.fori_loop(0, rows // NORM_ROWS, body, 0, unroll=NORM_UNROLL)


def _norm_matmul_kernel(x_ref, g_ref, w_ref, o_ref, xn_ref):
    @pl.when(pl.program_id(1) == 0)
    def _():
        _norm_rows_into(xn_ref, x_ref, g_ref)

    o_ref[...] = jnp.dot(xn_ref[...], w_ref[...], preferred_element_type=F32).astype(o_ref.dtype)


def _norm_swiglu_kernel(x_ref, g_ref, wg_ref, wu_ref, o_ref, xn_ref):
    @pl.when(pl.program_id(1) == 0)
    def _():
        _norm_rows_into(xn_ref, x_ref, g_ref)

    xn = xn_ref[...]
    gate = jnp.dot(xn, wg_ref[...], preferred_element_type=F32)
    up = jnp.dot(xn, wu_ref[...], preferred_element_type=F32)
    o_ref[...] = ((gate * jax.nn.sigmoid(gate)) * up).astype(o_ref.dtype)


def _norm_matmul(x, g, ws, layer, out_dtype, tm, tn):
    t, d = x.shape
    n = ws[0].shape[2]
    kern = _norm_matmul_kernel if len(ws) == 1 else _norm_swiglu_kernel
    w_specs = [pl.BlockSpec((None, d, tn), lambda i, j: (layer, 0, j)) for _ in ws]
    return pl.pallas_call(
        kern,
        grid=(t // tm, pl.cdiv(n, tn)),
        in_specs=[pl.BlockSpec((tm, d), lambda i, j: (i, 0), pipeline_mode=pl.Buffered(1)),
                  pl.BlockSpec((1, d), lambda i, j: (0, 0))] + w_specs,
        out_specs=pl.BlockSpec((tm, tn), lambda i, j: (i, j)),
        out_shape=jax.ShapeDtypeStruct((t, n), out_dtype),
        scratch_shapes=[pltpu.VMEM((tm, d), BF16)],
        compiler_params=_cparams(("parallel", "arbitrary")),
        name="norm_matmul" if len(ws) == 1 else "norm_swiglu",
    )(x, g.reshape(1, d), *ws)


def _rope_tables(pos_ref):
    lane = lax.broadcasted_iota(jnp.int32, (1, LANE), 1)
    half = MLA_ROPE // 2
    idx = (lane % half).astype(F32)
    inv_freq = jnp.exp((-math.log(ROPE_THETA) * 2.0) * idx / MLA_ROPE)
    inv_freq = jnp.where(lane < MLA_ROPE, inv_freq, 0.0)
    ang = pos_ref[...].astype(F32) * inv_freq
    cos, sin = jnp.cos(ang), jnp.sin(ang)
    sin_hi = jnp.where((lane >= half) & (lane < MLA_ROPE), sin, 0.0)
    sin_lo = jnp.where(lane < half, -sin, 0.0)
    return cos, sin_hi, sin_lo


def _rope(x, tables):
    cos, sin_hi, sin_lo = tables
    half = MLA_ROPE // 2
    return x * cos + pltpu.roll(x, half, 1) * sin_hi + pltpu.roll(x, LANE - half, 1) * sin_lo


def _rope_tables_t(pos_row_ref):
    half = MLA_ROPE // 2
    idx = lax.broadcasted_iota(jnp.int32, (half, 1), 0).astype(F32)
    inv_freq = jnp.exp((-math.log(ROPE_THETA) * 2.0) * idx / MLA_ROPE)
    ang = inv_freq * pos_row_ref[...].astype(F32)
    return jnp.cos(ang), jnp.sin(ang)


def _mla_proj_kernel(cq_ref, ckv_ref, kr_ref, pos_ref, pos_row_ref, gq_ref, gkv_ref, wqt_ref, wk_ref, wvt_ref,
                     qt_ref, k_ref, vt_ref):
    tables = _rope_tables(pos_ref)
    cos_t, sin_t = _rope_tables_t(pos_row_ref)
    cq = cq_ref[...]
    cqn = ((cq * _rms_scale(cq)) * gq_ref[...]).astype(BF16)
    ckv = ckv_ref[...]
    ckvn = ((ckv * _rms_scale(ckv)) * gkv_ref[...]).astype(BF16)
    k_pe = _rope(kr_ref[...], tables).astype(BF16)
    for h in range(MLA_HEADS):
        c0 = h * MLA_QK_PAD
        kh = jnp.dot(ckvn, wk_ref[:, h * LANE:(h + 1) * LANE], preferred_element_type=F32)
        k_ref[:, c0:c0 + LANE] = kh.astype(BF16)
        k_ref[:, c0 + LANE:c0 + MLA_QK_PAD] = k_pe
    group = 4
    half = MLA_ROPE // 2
    nt_dims = (((1,), (1,)), ((), ()))
    for h0 in range(0, MLA_HEADS, group):
        qt = lax.dot_general(wqt_ref[h0 * MLA_QK_PAD:(h0 + group) * MLA_QK_PAD, :], cqn, nt_dims,
                             preferred_element_type=F32) * MLA_EXP_SCALE
        for h in range(h0, h0 + group):
            r0, g0 = h * MLA_QK_PAD, (h - h0) * MLA_QK_PAD
            t1 = qt[g0 + MLA_NOPE:g0 + MLA_NOPE + half, :]
            t2 = qt[g0 + MLA_NOPE + half:g0 + MLA_NOPE + MLA_ROPE, :]
            qt_ref[r0:r0 + MLA_NOPE, :] = qt[g0:g0 + MLA_NOPE, :].astype(BF16)
            qt_ref[r0 + MLA_NOPE:r0 + MLA_NOPE + half, :] = (t1 * cos_t - t2 * sin_t).astype(BF16)
            qt_ref[r0 + MLA_NOPE + half:r0 + MLA_NOPE + MLA_ROPE, :] = (t1 * sin_t + t2 * cos_t).astype(BF16)
            qt_ref[r0 + MLA_NOPE + MLA_ROPE:r0 + MLA_QK_PAD, :] = (
                qt[g0 + MLA_NOPE + MLA_ROPE:g0 + MLA_QK_PAD, :].astype(BF16))
    ones = jnp.ones((MLA_ONES, vt_ref.shape[2]), BF16)
    for h0 in range(0, MLA_HEADS, group):
        vt = lax.dot_general(wvt_ref[h0 * MLA_V:(h0 + group) * MLA_V, :], ckvn, nt_dims,
                             preferred_element_type=F32).astype(BF16)
        for h in range(h0, h0 + group):
            r0 = h * MLA_VT_ROWS
            vt_ref[0, r0:r0 + MLA_V, :] = vt[(h - h0) * MLA_V:(h - h0 + 1) * MLA_V, :]
            vt_ref[0, r0 + MLA_V:r0 + MLA_VT_ROWS, :] = ones


def _mla_proj(h, positions, gq, gkv, wqt, wk, wvt, layer, tm):
    t = h.shape[0]
    qk_w = MLA_HEADS * MLA_QK_PAD
    const = lambda i: (0, 0)
    return pl.pallas_call(
        _mla_proj_kernel,
        grid=(t // tm,),
        in_specs=[pl.BlockSpec((tm, MLA_Q_RANK), lambda i: (i, COL_CQ // MLA_Q_RANK)),
                  pl.BlockSpec((tm, MLA_KV_RANK), lambda i: (i, COL_CKV // MLA_KV_RANK)),
                  pl.BlockSpec((tm, LANE), lambda i: (i, COL_KR // LANE)),
                  pl.BlockSpec((tm, 1), lambda i: (i, 0)),
                  pl.BlockSpec((1, tm), lambda i: (0, i)),
                  pl.BlockSpec((1, MLA_Q_RANK), const),
                  pl.BlockSpec((1, MLA_KV_RANK), const),
                  pl.BlockSpec((None,) + wqt.shape[1:], lambda i: (layer, 0, 0)),
                  pl.BlockSpec((None,) + wk.shape[1:], lambda i: (layer, 0, 0)),
                  pl.BlockSpec((None,) + wvt.shape[1:], lambda i: (layer, 0, 0))],
        out_specs=[pl.BlockSpec((qk_w, tm), lambda i: (0, i)),
                   pl.BlockSpec((tm, qk_w), lambda i: (i, 0)),
                   pl.BlockSpec((1, MLA_HEADS * MLA_VT_ROWS, tm), lambda i: (i, 0, 0))],
        out_shape=[jax.ShapeDtypeStruct((qk_w, t), BF16),
                   jax.ShapeDtypeStruct((t, qk_w), BF16),
                   jax.ShapeDtypeStruct((t // tm, MLA_HEADS * MLA_VT_ROWS, tm), BF16)],
        compiler_params=_cparams(("parallel",)),
        name="mla_proj",
    )(h, h, h, positions, positions.reshape(1, t), gq.reshape(1, -1), gkv.reshape(1, -1), wqt, wk, wvt)


MLA_TQ = 2048
MLA_TK = 512
MLA_QBLK = 256
MLA_KBLK = 128


def _mla_attn_kernel(qt_ref, k_ref, vt_ref, o_ref, sa_ref, sb_ref, xa_ref, xb_ref, pa_ref, pb_ref,
                     m_ref, alpha_ref, acc_ref):
    i = pl.program_id(2)
    tq, tk = MLA_TQ, MLA_TK
    m_ref[...] = jnp.full(m_ref.shape, -jnp.inf, F32)
    alpha_ref[...] = jnp.ones(alpha_ref.shape, F32)
    acc_ref[...] = jnp.zeros(acc_ref.shape, F32)
    pb_ref[...] = jnp.zeros(pb_ref.shape, BF16)

    def scores_into(s_ref, x_ref, t, q_lo=0):
        r = pl.ds(pl.multiple_of(t * tk, tk), tk)
        s = jnp.dot(k_ref[r, :], qt_ref[:, q_lo:], preferred_element_type=F32)
        s_ref[:, q_lo:] = s
        x_ref[:, q_lo:] = jnp.max(s, axis=0, keepdims=True)

    def add_values(p_ref, vt_tile, q_lo=0):
        acc_ref[:, q_lo:] = alpha_ref[:, q_lo:] * acc_ref[:, q_lo:] + jnp.dot(
            vt_tile, p_ref[:, q_lo:], preferred_element_type=F32)

    def softmax_terms(s_ref, x_ref, p_ref, chunk_shift, q_lo=0):
        for q0 in range(q_lo, tq, MLA_QBLK):
            cols = slice(q0, q0 + MLA_QBLK)

            def masked(s, k0):
                kc = (lax.broadcasted_iota(jnp.int32, s.shape, 0) + k0) // CHUNK + chunk_shift
                qc = (lax.broadcasted_iota(jnp.int32, s.shape, 1) + q0) // CHUNK
                return jnp.where(kc <= qc, s, MASK_VALUE)

            def visibility(k0):
                if chunk_shift is None:
                    return "all"
                k_min, k_max = k0 // CHUNK + chunk_shift, (k0 + MLA_KBLK - 1) // CHUNK + chunk_shift
                q_min, q_max = q0 // CHUNK, (q0 + MLA_QBLK - 1) // CHUNK
                return "all" if k_max <= q_min else "none" if k_min > q_max else "some"

            key_blocks = [(k0, visibility(k0)) for k0 in range(0, tk, MLA_KBLK)]
            if all(vis == "all" for _, vis in key_blocks):
                tile_max = x_ref[:, cols]
            else:
                tile_max = functools.reduce(jnp.maximum, [
                    jnp.max(s_ref[k0:k0 + MLA_KBLK, cols] if vis == "all"
                            else masked(s_ref[k0:k0 + MLA_KBLK, cols], k0), axis=0, keepdims=True)
                    for k0, vis in key_blocks if vis != "none"])
            m_old = m_ref[:, cols]
            m_new = jnp.maximum(m_old, tile_max)
            m_ref[:, cols] = m_new
            alpha_ref[:, cols] = jnp.exp2(m_old - m_new)
            for k0, vis in key_blocks:
                rows = slice(k0, k0 + MLA_KBLK)
                if vis == "none":
                    p_ref[rows, cols] = jnp.zeros((MLA_KBLK, MLA_QBLK), BF16)
                    continue
                s = s_ref[rows, cols]
                if vis == "some":
                    s = masked(s, k0)
                p_ref[rows, cols] = jnp.exp2(s - m_new).astype(BF16)

    nt = tq // tk
    first_diag = nt * i

    def pair(u, d):
        def lo(dd):
            return 0 if dd is None or dd < 0 else dd * tk

        def shift(dd):
            return None if dd is None else dd * (tk // CHUNK)

        d_odd = None if d is None else d + 1
        d_prev = None if d is None else d - 1
        scores_into(sb_ref, xb_ref, 2 * u + 1, lo(d_odd))
        add_values(pb_ref, vt_ref[jnp.maximum(2 * u - 1, 0)], lo(d_prev))
        softmax_terms(sa_ref, xa_ref, pa_ref, shift(d), lo(d))
        if d is None:
            scores_into(sa_ref, xa_ref, 2 * u + 2)
        elif d + 2 < nt:
            scores_into(sa_ref, xa_ref, 2 * u + 2, lo(d + 2))
        add_values(pa_ref, vt_ref[2 * u], lo(d))
        softmax_terms(sb_ref, xb_ref, pb_ref, shift(d_odd), lo(d_odd))

    scores_into(sa_ref, xa_ref, 0)

    def body(u, carry):
        pair(u, None)
        return carry

    lax.fori_loop(0, first_diag // 2, body, 0)
    for d in range(0, nt, 2):
        pair((first_diag + d) // 2, d)
    add_values(pb_ref, vt_ref[first_diag + nt - 1], (nt - 1) * tk)
    o_ref[...] = (acc_ref[:MLA_V, :] / acc_ref[MLA_V:MLA_V + 1, :]).T


def _mla_attn(qt, k, vt, batch, seq):
    tq = MLA_TQ
    nq = seq // tq
    assert vt.shape[2] == MLA_TK
    return pl.pallas_call(
        _mla_attn_kernel,
        grid=(batch, MLA_HEADS, nq),
        in_specs=[pl.BlockSpec((MLA_QK_PAD, tq), lambda b, h, i: (h, b * nq + i)),
                  pl.BlockSpec((seq, MLA_QK_PAD), lambda b, h, i: (b, h)),
                  pl.BlockSpec((seq // MLA_TK, MLA_VT_ROWS, MLA_TK), lambda b, h, i: (b, h, 0))],
        out_specs=pl.BlockSpec((tq, MLA_V), lambda b, h, i: (b * nq + i, h)),
        out_shape=jax.ShapeDtypeStruct((batch * seq, MLA_WIDTH), F32),
        scratch_shapes=[pltpu.VMEM((MLA_TK, tq), F32), pltpu.VMEM((MLA_TK, tq), F32),
                        pltpu.VMEM((1, tq), F32), pltpu.VMEM((1, tq), F32),
                        pltpu.VMEM((MLA_TK, tq), BF16), pltpu.VMEM((MLA_TK, tq), BF16),
                        pltpu.VMEM((1, tq), F32), pltpu.VMEM((1, tq), F32),
                        pltpu.VMEM((MLA_VT_ROWS, tq), F32)],
        compiler_params=_cparams(("parallel", "parallel", "arbitrary")),
        name="mla_attn",
    )(qt, k, vt)


def _block_cumsum(x, row):
    r = row % HG_BLOCK
    s = 1
    while s < HG_BLOCK:
        x = x + jnp.where(r >= s, pltpu.roll(x, s, 0), 0.0)
        s *= 2
    return x


def _hgrn_chunk(hq, hf, hi, lb, state):
    c = HG_CHUNK
    nb = c // HG_BLOCK
    row = lax.broadcasted_iota(jnp.int32, (c, HG_DIM), 0)
    col = lax.broadcasted_iota(jnp.int32, (c, HG_DIM), 1)

    q = hq * jax.nn.sigmoid(hq)
    f = lb + (1.0 - lb) * jax.nn.sigmoid(hf)
    k = (1.0 - lb) * jax.nn.sigmoid(-hf)
    b = _block_cumsum(jnp.log(jnp.maximum(f, TINY)), row)
    b3 = b.reshape(nb, HG_BLOCK, HG_DIM)
    b_last3 = jnp.broadcast_to(b3[:, HG_BLOCK - 1:HG_BLOCK, :], b3.shape)
    b_last = b_last3.reshape(c, HG_DIM)

    q3 = q.reshape(nb, HG_BLOCK, HG_DIM)
    k3 = k.reshape(nb, HG_BLOCK, HG_DIM)
    v3 = hi.reshape(nb, HG_BLOCK, HG_DIM)
    irow = lax.broadcasted_iota(jnp.int32, b3.shape, 1)
    b3_log2 = b3 * math.log2(math.e)
    sub = SUBLANE
    ws = []
    for j in range(HG_BLOCK):
        lo = (j // sub) * sub
        w = q3[:, lo:, :] * k3[:, j:j + 1, :] * jnp.exp2(b3_log2[:, lo:, :] - b3_log2[:, j:j + 1, :])
        if j > lo:
            w = jnp.where(irow[:, lo:, :] >= j, w, 0.0)
        ws.append(w.reshape(nb * (HG_BLOCK - lo), HG_DIM))
    sums = jnp.dot(jnp.concatenate(ws, axis=0).astype(BF16), jnp.ones((HG_DIM, HG_DIM), BF16),
                   preferred_element_type=F32)
    o_parts = [jnp.zeros((nb, sub, HG_DIM), F32) for _ in range(HG_BLOCK // sub)]
    r_at = 0
    for j in range(HG_BLOCK):
        lo = (j // sub) * sub
        n_rows = nb * (HG_BLOCK - lo)
        a_j = sums[r_at:r_at + n_rows, :].reshape(nb, HG_BLOCK - lo, HG_DIM)
        r_at += n_rows
        wv = a_j * v3[:, j:j + 1, :]
        for part in range(lo // sub, HG_BLOCK // sub):
            r0 = part * sub - lo
            o_parts[part] = o_parts[part] + wv[:, r0:r0 + sub, :]
    o = jnp.concatenate(o_parts, axis=1).reshape(c, HG_DIM)

    q_dec = q * jnp.exp(b)
    k_dec_t = (k * jnp.exp(b_last - b)).T
    b_last_t = b_last.T
    v_bf = hi.astype(BF16)
    blk_of_col = col // HG_BLOCK
    k_stack = jnp.concatenate(
        [jnp.where(blk_of_col == j, k_dec_t, 0.0) for j in range(nb)], axis=0).astype(BF16)
    u_all = jnp.dot(k_stack, v_bf, preferred_element_type=F32)
    states = []
    for j in range(nb):
        states.append(state.astype(BF16))
        decay = jnp.exp(b_last_t[:, j * HG_BLOCK:j * HG_BLOCK + 1])
        state = decay * state + u_all[j * HG_DIM:(j + 1) * HG_DIM, :]
    s_stack = jnp.concatenate(states, axis=0)
    blk_of_row = row // HG_BLOCK
    q_exp = jnp.concatenate(
        [jnp.where(blk_of_row == j, q_dec, 0.0) for j in range(nb)], axis=1).astype(BF16)
    o = o + jnp.dot(q_exp, s_stack, preferred_element_type=F32)
    return o, state


def _hgrn_kernel(hq_ref, hf_ref, hi_ref, hg_ref, lbraw_ref, gn_ref, o_ref, state_ref, *, layer):
    @pl.when(pl.program_id(2) == 0)
    def _():
        state_ref[...] = jnp.zeros(state_ref.shape, F32)

    raw = lbraw_ref[...]
    e = jnp.exp(raw - jnp.max(raw, axis=0, keepdims=True))
    p = e / jnp.sum(e, axis=0, keepdims=True)
    lb = jnp.sum(p[:layer + 1, :], axis=0, keepdims=True) - p[0:1, :]

    def body(ci, carry):
        r = pl.ds(pl.multiple_of(ci * HG_CHUNK, HG_CHUNK), HG_CHUNK)
        o, state = _hgrn_chunk(hq_ref[r, :], hf_ref[r, :], hi_ref[r, :], lb, state_ref[...])
        state_ref[...] = state
        o = (o * _rms_scale(o)) * gn_ref[...]
        g = hg_ref[r, :]
        o_ref[r, :] = (o * (g * jax.nn.sigmoid(g))).astype(o_ref.dtype)
        return carry

    lax.fori_loop(0, hq_ref.shape[0] // HG_CHUNK, body, 0, unroll=2)


def _hgrn(h, lb_raw, gn, layer, batch, seq):
    rows = HG_ROWS
    nr = seq // rows
    depth = lb_raw.shape[0]

    def col_spec(col0):
        return pl.BlockSpec((rows, HG_DIM), lambda b, hh, c: (b * nr + c, col0 // HG_DIM + hh))

    return pl.pallas_call(
        functools.partial(_hgrn_kernel, layer=layer),
        grid=(batch, HG_HEADS, nr),
        in_specs=[col_spec(COL_HQ), col_spec(COL_HF), col_spec(COL_HI), col_spec(COL_HG),
                  pl.BlockSpec((depth, HG_DIM), lambda b, hh, c: (0, hh)),
                  pl.BlockSpec((1, HG_DIM), lambda b, hh, c: (0, hh))],
        out_specs=pl.BlockSpec((rows, HG_DIM), lambda b, hh, c: (b * nr + c, hh)),
        out_shape=jax.ShapeDtypeStruct((batch * seq, HG_WIDTH), BF16),
        scratch_shapes=[pltpu.VMEM((HG_DIM, HG_DIM), F32)],
        compiler_params=_cparams(("parallel", "parallel", "arbitrary")),
        name="hgrn",
    )(h, h, h, h, lb_raw, gn.reshape(1, -1))


def _ca_bias_rows(rel_bias):
    idx = jnp.arange(CA_BIAS_LEN)
    m = jnp.where(idx < CA_WIN, idx, idx - CA_BIAS_LEN)
    bucket = jnp.clip(2 * CA_TQ - m, -CA_REL_CLIP, CA_REL_CLIP) + CA_REL_CLIP
    return rel_bias[:, bucket].astype(F32)


def _ca_kernel(q_ref, k0_ref, k1_ref, k2_ref, v0_ref, v1_ref, v2_ref, brow_ref, gn_ref,
               o_ref, bias_ref):
    t = pl.program_id(1)
    tq = CA_TQ

    log2e = math.log2(math.e)

    @pl.when((pl.program_id(0) == 0) & (t == 0))
    def _():
        qc = lax.broadcasted_iota(jnp.int32, (tq, CA_WIN), 0) // CHUNK
        kk = lax.broadcasted_iota(jnp.int32, (tq, CA_WIN), 1)
        kc = kk // CHUNK
        band = (kc >= qc) & (kc <= qc + CA_LEFT_CHUNKS)
        for h in range(CA_HEADS):
            rows = jnp.broadcast_to(brow_ref[h:h + 1, :], (tq, CA_BIAS_LEN))
            rolled = pltpu.roll(rows, 0, 1, stride=1, stride_axis=0)
            base = jnp.where(band, rolled[:, :CA_WIN] * log2e, MASK_VALUE)
            for v in range(CA_SEQ_VARIANTS):
                first_valid = max(2 * tq - v * tq, 0)
                bias_ref[v, h] = jnp.where(kk >= first_valid, base, MASK_VALUE) if first_valid else base

    variant = jnp.minimum(t, CA_SEQ_VARIANTS - 1)
    c_exp = CA_DIM ** -0.5 * log2e
    ones = jnp.ones((CA_WIN, CA_DIM), BF16)
    outs = []
    for h in range(CA_HEADS):
        c = slice(h * CA_DIM, (h + 1) * CA_DIM)
        qh = (q_ref[:, c] * c_exp).astype(BF16)
        kh = jnp.concatenate([k0_ref[:, c], k1_ref[:, c], k2_ref[:, c]], axis=0).astype(BF16)
        vh = jnp.concatenate([v0_ref[:, c], v1_ref[:, c], v2_ref[:, c]], axis=0).astype(BF16)
        s = lax.dot_general(qh, kh, (((1,), (1,)), ((), ())), preferred_element_type=F32)
        s = s + bias_ref[variant, h]
        p = jnp.exp2(s - jnp.max(s, axis=-1, keepdims=True)).astype(BF16)
        ov = jnp.dot(p, jnp.concatenate([vh, ones], axis=1), preferred_element_type=F32)
        outs.append(ov[:, :CA_DIM] / ov[:, CA_DIM:CA_DIM + 1])
    o = jnp.concatenate(outs, axis=1)
    o_ref[...] = ((o * _rms_scale(o)) * gn_ref[...]).astype(o_ref.dtype)


def _chunk_attn(h, rel_bias, gn, batch, seq):
    tq = CA_TQ
    nq = seq // tq
    cq, ck, cv = COL_AQ // CA_WIDTH, COL_AK // CA_WIDTH, COL_AV // CA_WIDTH

    def kv_spec(col, back):
        return pl.BlockSpec((tq, CA_WIDTH), lambda b, t: (b * nq + jnp.maximum(t - back, 0), col))

    return pl.pallas_call(
        _ca_kernel,
        grid=(batch, nq),
        in_specs=[pl.BlockSpec((tq, CA_WIDTH), lambda b, t: (b * nq + t, cq)),
                  kv_spec(ck, 2), kv_spec(ck, 1), kv_spec(ck, 0),
                  kv_spec(cv, 2), kv_spec(cv, 1), kv_spec(cv, 0),
                  pl.BlockSpec((CA_HEADS, CA_BIAS_LEN), lambda b, t: (0, 0)),
                  pl.BlockSpec((1, CA_WIDTH), lambda b, t: (0, 0))],
        out_specs=pl.BlockSpec((tq, CA_WIDTH), lambda b, t: (b * nq + t, 0)),
        out_shape=jax.ShapeDtypeStruct((batch * seq, CA_WIDTH), BF16),
        scratch_shapes=[pltpu.VMEM((CA_SEQ_VARIANTS, CA_HEADS, tq, CA_WIN), F32)],
        compiler_params=_cparams(("arbitrary", "arbitrary")),
        name="chunk_attn",
    )(h, h, h, h, h, h, h, _ca_bias_rows(rel_bias), gn.reshape(1, -1))


X_SLICES = 8


def _residual_norm(o_ref, xs_ref, g_ref):
    rows, d = o_ref.shape
    w = d // X_SLICES

    step = NORM_ROWS * NORM_UNROLL

    def body(c, carry):
        base = pl.multiple_of(c * step, step)
        chunks = [pl.ds(base + u * NORM_ROWS, NORM_ROWS) for u in range(NORM_UNROLL)]
        ys = [o_ref[r, :] for r in chunks]
        yns = [(y * _rms_scale(y)) * g_ref[...] for y in ys]
        for r, yn in zip(chunks, yns):
            for p in range(X_SLICES):
                o_ref[r, p * w:(p + 1) * w] = xs_ref[p, r, :] + yn[:, p * w:(p + 1) * w]
        return carry

    lax.fori_loop(0, rows // step, body, 0)


def _x_slice_spec(tm, d):
    return pl.BlockSpec((tm, d // X_SLICES), lambda i, k: (i, jnp.minimum(k, X_SLICES - 1)))


def _accumulate_then_residual_norm(a_ref, w_ref, x_ref, g_ref, o_ref, xs_ref, ragged):
    k = pl.program_id(1)
    last = pl.num_programs(1) - 1

    @pl.when(k < X_SLICES)
    def _():
        xs_ref[k] = x_ref[...]

    @pl.when(k == 0)
    def _():
        o_ref[...] = jnp.dot(a_ref[...], w_ref[...], preferred_element_type=F32)

    if ragged is None:
        @pl.when(k > 0)
        def _():
            o_ref[...] += jnp.dot(a_ref[...], w_ref[...], preferred_element_type=F32)
    else:
        @pl.when((k > 0) & (k < last))
        def _():
            o_ref[...] += jnp.dot(a_ref[...], w_ref[...], preferred_element_type=F32)

        @pl.when(k == last)
        def _():
            a, w = a_ref[...], w_ref[...]
            a = jnp.where(lax.broadcasted_iota(jnp.int32, a.shape, 1) < ragged, a, jnp.zeros_like(a))
            w = jnp.where(lax.broadcasted_iota(jnp.int32, w.shape, 0) < ragged, w, jnp.zeros_like(w))
            o_ref[...] += jnp.dot(a, w, preferred_element_type=F32)

    @pl.when(k == last)
    def _():
        _residual_norm(o_ref, xs_ref, g_ref)


OUT_TK = 512


def _out_proj_kernel(mla_ref, hg_ref, ca_ref, w_ref, x_ref, gm_ref, gp_ref, o_ref, a_ref, xs_ref):
    @pl.when(pl.program_id(1) == 0)
    def _():
        rows = mla_ref.shape[0]
        n_mla, n_hg, n_ca = MLA_WIDTH // OUT_TK, HG_WIDTH // OUT_TK, CA_WIDTH // OUT_TK

        def body(c, carry):
            r = pl.ds(pl.multiple_of(c * NORM_ROWS, NORM_ROWS), NORM_ROWS)
            m = mla_ref[r, :]
            mn = ((m * _rms_scale(m)) * gm_ref[...]).astype(BF16)
            for p in range(n_mla):
                a_ref[p, r, :] = mn[:, p * OUT_TK:(p + 1) * OUT_TK]
            return carry

        lax.fori_loop(0, rows // NORM_ROWS, body, 0, unroll=NORM_UNROLL)
        for p in range(n_hg):
            a_ref[n_mla + p] = hg_ref[:, p * OUT_TK:(p + 1) * OUT_TK]
        for p in range(n_ca):
            a_ref[n_mla + n_hg + p] = ca_ref[:, p * OUT_TK:(p + 1) * OUT_TK]

    _accumulate_then_residual_norm(a_ref.at[pl.program_id(1)], w_ref, x_ref, gp_ref, o_ref, xs_ref, None)


def _out_proj(o_mla, o_hg, o_ca, x, g_mla, g_post, w, layer, tm):
    t, d = x.shape
    kdim = w.shape[1]
    nk = kdim // OUT_TK
    assert MLA_WIDTH % OUT_TK == 0 and HG_WIDTH % OUT_TK == 0 and CA_WIDTH % OUT_TK == 0
    assert kdim == MLA_WIDTH + HG_WIDTH + CA_WIDTH and nk >= X_SLICES
    return pl.pallas_call(
        _out_proj_kernel,
        grid=(t // tm, nk),
        in_specs=[pl.BlockSpec((tm, MLA_WIDTH), lambda i, k: (i, 0)),
                  pl.BlockSpec((tm, HG_WIDTH), lambda i, k: (i, 0)),
                  pl.BlockSpec((tm, CA_WIDTH), lambda i, k: (i, 0)),
                  pl.BlockSpec((None, OUT_TK, d), lambda i, k: (layer, k, 0)),
                  _x_slice_spec(tm, d),
                  pl.BlockSpec((1, MLA_WIDTH), lambda i, k: (0, 0)),
                  pl.BlockSpec((1, d), lambda i, k: (0, 0))],
        out_specs=pl.BlockSpec((tm, d), lambda i, k: (i, 0)),
        out_shape=jax.ShapeDtypeStruct((t, d), F32),
        scratch_shapes=[pltpu.VMEM((nk, tm, OUT_TK), BF16),
                        pltpu.VMEM((X_SLICES, tm, d // X_SLICES), F32)],
        compiler_params=_cparams(("parallel", "arbitrary")),
        name="out_proj",
    )(o_mla, o_hg, o_ca, w, x, g_mla.reshape(1, -1), g_post.reshape(1, -1))


def _down_proj_kernel(a_ref, w_ref, x_ref, g_ref, o_ref, xs_ref, *, ragged):
    _accumulate_then_residual_norm(a_ref, w_ref, x_ref, g_ref, o_ref, xs_ref, ragged)


def _down_proj(a, w, x, g, layer, tm, tk):
    t, d = x.shape
    kdim = a.shape[1]
    ragged = kdim % tk or None
    nk = pl.cdiv(kdim, tk)
    assert nk >= X_SLICES
    return pl.pallas_call(
        functools.partial(_down_proj_kernel, ragged=ragged),
        grid=(t // tm, nk),
        in_specs=[pl.BlockSpec((tm, tk), lambda i, k: (i, k)),
                  pl.BlockSpec((None, tk, d), lambda i, k: (layer, k, 0)),
                  _x_slice_spec(tm, d),
                  pl.BlockSpec((1, d), lambda i, k: (0, 0))],
        out_specs=pl.BlockSpec((tm, d), lambda i, k: (i, 0)),
        out_shape=jax.ShapeDtypeStruct((t, d), F32),
        scratch_shapes=[pltpu.VMEM((X_SLICES, tm, d // X_SLICES), F32)],
        compiler_params=_cparams(("parallel", "arbitrary")),
        name="down_proj",
    )(a, w, x, g.reshape(1, -1))


PREP_ROWS = 256


def _prep_w_in(w):
    depth, d, _ = w.shape
    o_ckv = MLA_Q_RANK
    o_kr = o_ckv + MLA_KV_RANK
    o_hg = o_kr + MLA_ROPE
    o_ca = o_hg + 4 * HG_WIDTH
    assert (COL_AQ, COL_AK, COL_AV) == (0, CA_WIDTH, 2 * CA_WIDTH) and COL_HQ == 3 * CA_WIDTH
    assert (COL_HF, COL_HI, COL_HG) == (COL_HQ + HG_WIDTH, COL_HQ + 2 * HG_WIDTH, COL_HQ + 3 * HG_WIDTH)
    assert COL_CKV == COL_HG + HG_WIDTH and COL_CQ == COL_CKV + MLA_KV_RANK and COL_KR == COL_CQ + MLA_Q_RANK
    n_in = w.shape[2]

    def regroup_kernel(w_ref, o_ref):
        x = w_ref[...]
        z = jnp.zeros((x.shape[0], D_IN_PAD - COL_KR - MLA_ROPE), x.dtype)
        o_ref[...] = jnp.concatenate([x[:, o_ca:], x[:, o_hg:o_ca], x[:, o_ckv:o_kr], x[:, :o_ckv],
                                      x[:, o_kr:o_hg], z], axis=1).astype(BF16)

    rows = PREP_ROWS
    return pl.pallas_call(
        regroup_kernel,
        grid=(depth, d // rows),
        in_specs=[pl.BlockSpec((None, rows, n_in), lambda l, i: (l, i, 0))],
        out_specs=pl.BlockSpec((None, rows, D_IN_PAD), lambda l, i: (l, i, 0)),
        out_shape=jax.ShapeDtypeStruct((depth, d, D_IN_PAD), BF16),
        compiler_params=_cparams(("parallel", "parallel")),
        name="w_in_regroup",
    )(w)


def _prep_w_uq(w):
    depth, r, _ = w.shape
    w4 = w.astype(BF16).reshape(depth, r, MLA_HEADS, MLA_NOPE + MLA_ROPE)
    pad = jnp.zeros((depth, r, MLA_HEADS, MLA_QK_PAD - MLA_NOPE - MLA_ROPE), BF16)
    wq = jnp.concatenate([w4, pad], axis=3).reshape(depth, r, MLA_HEADS * MLA_QK_PAD)
    return jnp.swapaxes(wq, 1, 2)


def _prep_w_ukv(w):
    depth, r, _ = w.shape
    w4 = w.astype(BF16).reshape(depth, r, MLA_HEADS, MLA_NOPE + MLA_V)
    wk = w4[:, :, :, :MLA_NOPE].reshape(depth, r, MLA_HEADS * MLA_NOPE)
    wv = w4[:, :, :, MLA_NOPE:].reshape(depth, r, MLA_HEADS * MLA_V)
    return wk, jnp.swapaxes(wv, 1, 2)


def kernel(x, positions, attn_pre_norm, attn_post_norm, w_in, mla_q_norm, mla_kv_norm, w_uq, w_ukv,
           mla_out_norm, hg_lower_bounds, hg_out_norm, ca_rel_bias, ca_out_norm, w_out, ffn_pre_norm,
           ffn_post_norm, w_gate, w_up, w_down):
    batch, seq, d = x.shape
    t = batch * seq
    depth = w_in.shape[0]
    xf = x.reshape(t, d)
    pos = positions.reshape(t, 1)
    w_in_b = _prep_w_in(w_in)
    w_uq_b = _prep_w_uq(w_uq)
    w_uk_b, w_uvt_b = _prep_w_ukv(w_ukv)
    w_out_b, w_gate_b, w_up_b, w_down_b = (w.astype(BF16) for w in (w_out, w_gate, w_up, w_down))
    for l in range(depth):
        h = _norm_matmul(xf, attn_pre_norm[l], [w_in_b], l, F32, tm=NORM_MATMUL_TM, tn=NORM_MATMUL_TN)
        q, k, vt = _mla_proj(h, pos, mla_q_norm[l], mla_kv_norm[l], w_uq_b, w_uk_b, w_uvt_b, l, tm=MLA_TK)
        o_mla = _mla_attn(q, k, vt, batch, seq)
        o_hg = _hgrn(h, hg_lower_bounds, hg_out_norm[l], l, batch, seq)
        o_ca = _chunk_attn(h, ca_rel_bias[l], ca_out_norm[l], batch, seq)
        xf = _out_proj(o_mla, o_hg, o_ca, xf, mla_out_norm[l], attn_post_norm[l], w_out_b, l, tm=RESIDUAL_TM)
        hid = _norm_matmul(xf, ffn_pre_norm[l], [w_gate_b, w_up_b], l, BF16,
                           tm=NORM_MATMUL_TM, tn=NORM_MATMUL_TN)
        xf = _down_proj(hid, w_down_b, xf, ffn_post_norm[l], l, tm=RESIDUAL_TM, tk=DOWN_TK)
    return xf.reshape(batch, seq, d)
```

```python
import functools
import math

import jax
import jax.numpy as jnp
from jax import lax
from jax.experimental import pallas as pl
from jax.experimental.pallas import tpu as pltpu

F32 = jnp.float32
BF16 = jnp.bfloat16

EPS = 1e-6
MASK_VALUE = -1e30
TINY = 1e-30
CHUNK = 64

MLA_HEADS = 16
MLA_Q_RANK = 768
MLA_KV_RANK = 512
MLA_NOPE = 128
MLA_ROPE = 64
MLA_V = 128
ROPE_THETA = 10000.0
MLA_QK_PAD = 256
MLA_ONES = 16
MLA_VT_ROWS = MLA_V + MLA_ONES
MLA_EXP_SCALE = (MLA_NOPE + MLA_ROPE) ** -0.5 * math.log2(math.e)

HG_HEADS = 8
HG_DIM = 128
HG_BLOCK = 16
HG_CHUNK = 128
HG_ROWS = 1024

CA_HEADS = 8
CA_DIM = 128
CA_LEFT_CHUNKS = 8
CA_REL_CLIP = 256
CA_TQ = 256
CA_WIN = 3 * CA_TQ
CA_BIAS_LEN = 1024
CA_SEQ_VARIANTS = 3

MLA_WIDTH = MLA_HEADS * MLA_V
HG_WIDTH = HG_HEADS * HG_DIM
CA_WIDTH = CA_HEADS * CA_DIM

COL_AQ = 0
COL_AK = 1024
COL_AV = 2048
COL_HQ = 3072
COL_HF = 4096
COL_HI = 5120
COL_HG = 6144
COL_CKV = 7168
COL_CQ = 7680
COL_KR = 8448
D_IN_PAD = 8704

LANE = 128
SUBLANE = 8
V7X_VMEM_BYTES = 64 * 1024 * 1024
VMEM_LIMIT = V7X_VMEM_BYTES * 7 // 8

NORM_MATMUL_TM = 1024
NORM_MATMUL_TN = 512
RESIDUAL_TM = 512
DOWN_TK = 1024


VMEM_LIMIT_HIGH = V7X_VMEM_BYTES * 15 // 16


def _cparams(sem, vmem=VMEM_LIMIT):
    return pltpu.CompilerParams(dimension_semantics=sem, vmem_limit_bytes=vmem)


def _rms_scale(x):
    return lax.rsqrt(jnp.mean(x * x, axis=-1, keepdims=True) + EPS)


NORM_ROWS = 16
NORM_UNROLL = 4


def _norm_rows_into(xn_ref, x_ref, g_ref):
    rows = x_ref.shape[0]

    def body(c, carry):
        r = pl.ds(pl.multiple_of(c * NORM_ROWS, NORM_ROWS), NORM_ROWS)
        x = x_ref[r, :]
        xn_ref[r, :] = ((x * _rms_scale(x)) * g_ref[...]).astype(BF16)
        return carry

    lax.fori_loop(0, rows // NORM_ROWS, body, 0, unroll=NORM_UNROLL)


def _norm_matmul_kernel(x_ref, g_ref, w_ref, o_ref, xn_ref):
    @pl.when(pl.program_id(1) == 0)
    def _():
        _norm_rows_into(xn_ref, x_ref, g_ref)

    o_ref[...] = jnp.dot(xn_ref[...], w_ref[...], preferred_element_type=F32).astype(o_ref.dtype)


def _norm_swiglu_kernel(x_ref, g_ref, wg_ref, wu_ref, o_ref, xn_ref):
    @pl.when(pl.program_id(1) == 0)
    def _():
        _norm_rows_into(xn_ref, x_ref, g_ref)

    xn = xn_ref[...]
    gate = jnp.dot(xn, wg_ref[...], preferred_element_type=F32)
    up = jnp.dot(xn, wu_ref[...], preferred_element_type=F32)
    o_ref[...] = ((gate * jax.nn.sigmoid(gate)) * up).astype(o_ref.dtype)


def _norm_matmul(x, g, ws, layer, out_dtype, tm, tn):
    t, d = x.shape
    n = ws[0].shape[2]
    kern = _norm_matmul_kernel if len(ws) == 1 else _norm_swiglu_kernel
    w_specs = [pl.BlockSpec((None, d, tn), lambda i, j: (layer, 0, j)) for _ in ws]
    x_buffers = 2 if len(ws) == 1 else 1
    return pl.pallas_call(
        kern,
        grid=(t // tm, pl.cdiv(n, tn)),
        in_specs=[pl.BlockSpec((tm, d), lambda i, j: (i, 0), pipeline_mode=pl.Buffered(x_buffers)),
                  pl.BlockSpec((1, d), lambda i, j: (0, 0))] + w_specs,
        out_specs=pl.BlockSpec((tm, tn), lambda i, j: (i, j)),
        out_shape=jax.ShapeDtypeStruct((t, n), out_dtype),
        scratch_shapes=[pltpu.VMEM((tm, d), BF16)],
        compiler_params=_cparams(("parallel", "arbitrary"), VMEM_LIMIT_HIGH if x_buffers == 2 else VMEM_LIMIT),
        name="norm_matmul" if len(ws) == 1 else "norm_swiglu",
    )(x, g.reshape(1, d), *ws)


def _rope_tables(pos_ref):
    lane = lax.broadcasted_iota(jnp.int32, (1, LANE), 1)
    half = MLA_ROPE // 2
    idx = (lane % half).astype(F32)
    inv_freq = jnp.exp((-math.log(ROPE_THETA) * 2.0) * idx / MLA_ROPE)
    inv_freq = jnp.where(lane < MLA_ROPE, inv_freq, 0.0)
    ang = pos_ref[...].astype(F32) * inv_freq
    cos, sin = jnp.cos(ang), jnp.sin(ang)
    sin_hi = jnp.where((lane >= half) & (lane < MLA_ROPE), sin, 0.0)
    sin_lo = jnp.where(lane < half, -sin, 0.0)
    return cos, sin_hi, sin_lo


def _rope(x, tables):
    cos, sin_hi, sin_lo = tables
    half = MLA_ROPE // 2
    return x * cos + pltpu.roll(x, half, 1) * sin_hi + pltpu.roll(x, LANE - half, 1) * sin_lo


def _rope_tables_t(pos_row_ref):
    half = MLA_ROPE // 2
    idx = lax.broadcasted_iota(jnp.int32, (half, 1), 0).astype(F32)
    inv_freq = jnp.exp((-math.log(ROPE_THETA) * 2.0) * idx / MLA_ROPE)
    ang = inv_freq * pos_row_ref[...].astype(F32)
    return jnp.cos(ang), jnp.sin(ang)


def _mla_proj_kernel(cq_ref, ckv_ref, kr_ref, pos_ref, pos_row_ref, gq_ref, gkv_ref, wqt_ref, wk_ref, wvt_ref,
                     qt_ref, k_ref, vt_ref):
    tables = _rope_tables(pos_ref)
    cos_t, sin_t = _rope_tables_t(pos_row_ref)
    cq = cq_ref[...]
    cqn = ((cq * _rms_scale(cq)) * gq_ref[...]).astype(BF16)
    ckv = ckv_ref[...]
    ckvn = ((ckv * _rms_scale(ckv)) * gkv_ref[...]).astype(BF16)
    k_pe = _rope(kr_ref[...], tables).astype(BF16)
    for h in range(MLA_HEADS):
        c0 = h * MLA_QK_PAD
        kh = jnp.dot(ckvn, wk_ref[:, h * LANE:(h + 1) * LANE], preferred_element_type=F32)
        k_ref[:, c0:c0 + LANE] = kh.astype(BF16)
        k_ref[:, c0 + LANE:c0 + MLA_QK_PAD] = k_pe
    group = 4
    half = MLA_ROPE // 2
    nt_dims = (((1,), (1,)), ((), ()))
    for h0 in range(0, MLA_HEADS, group):
        qt = lax.dot_general(wqt_ref[h0 * MLA_QK_PAD:(h0 + group) * MLA_QK_PAD, :], cqn, nt_dims,
                             preferred_element_type=F32) * MLA_EXP_SCALE
        for h in range(h0, h0 + group):
            r0, g0 = h * MLA_QK_PAD, (h - h0) * MLA_QK_PAD
            t1 = qt[g0 + MLA_NOPE:g0 + MLA_NOPE + half, :]
            t2 = qt[g0 + MLA_NOPE + half:g0 + MLA_NOPE + MLA_ROPE, :]
            qt_ref[r0:r0 + MLA_NOPE, :] = qt[g0:g0 + MLA_NOPE, :].astype(BF16)
            qt_ref[r0 + MLA_NOPE:r0 + MLA_NOPE + half, :] = (t1 * cos_t - t2 * sin_t).astype(BF16)
            qt_ref[r0 + MLA_NOPE + half:r0 + MLA_NOPE + MLA_ROPE, :] = (t1 * sin_t + t2 * cos_t).astype(BF16)
            qt_ref[r0 + MLA_NOPE + MLA_ROPE:r0 + MLA_QK_PAD, :] = (
                qt[g0 + MLA_NOPE + MLA_ROPE:g0 + MLA_QK_PAD, :].astype(BF16))
    ones = jnp.ones((MLA_ONES, vt_ref.shape[2]), BF16)
    for h0 in range(0, MLA_HEADS, group):
        vt = lax.dot_general(wvt_ref[h0 * MLA_V:(h0 + group) * MLA_V, :], ckvn, nt_dims,
                             preferred_element_type=F32).astype(BF16)
        for h in range(h0, h0 + group):
            r0 = h * MLA_VT_ROWS
            vt_ref[0, r0:r0 + MLA_V, :] = vt[(h - h0) * MLA_V:(h - h0 + 1) * MLA_V, :]
            vt_ref[0, r0 + MLA_V:r0 + MLA_VT_ROWS, :] = ones


def _mla_proj(h, positions, gq, gkv, wqt, wk, wvt, layer, tm):
    t = h.shape[0]
    qk_w = MLA_HEADS * MLA_QK_PAD
    const = lambda i: (0, 0)
    return pl.pallas_call(
        _mla_proj_kernel,
        grid=(t // tm,),
        in_specs=[pl.BlockSpec((tm, MLA_Q_RANK), lambda i: (i, COL_CQ // MLA_Q_RANK)),
                  pl.BlockSpec((tm, MLA_KV_RANK), lambda i: (i, COL_CKV // MLA_KV_RANK)),
                  pl.BlockSpec((tm, LANE), lambda i: (i, COL_KR // LANE)),
                  pl.BlockSpec((tm, 1), lambda i: (i, 0)),
                  pl.BlockSpec((1, tm), lambda i: (0, i)),
                  pl.BlockSpec((1, MLA_Q_RANK), const),
                  pl.BlockSpec((1, MLA_KV_RANK), const),
                  pl.BlockSpec((None,) + wqt.shape[1:], lambda i: (layer, 0, 0)),
                  pl.BlockSpec((None,) + wk.shape[1:], lambda i: (layer, 0, 0)),
                  pl.BlockSpec((None,) + wvt.shape[1:], lambda i: (layer, 0, 0))],
        out_specs=[pl.BlockSpec((qk_w, tm), lambda i: (0, i)),
                   pl.BlockSpec((tm, qk_w), lambda i: (i, 0)),
                   pl.BlockSpec((1, MLA_HEADS * MLA_VT_ROWS, tm), lambda i: (i, 0, 0))],
        out_shape=[jax.ShapeDtypeStruct((qk_w, t), BF16),
                   jax.ShapeDtypeStruct((t, qk_w), BF16),
                   jax.ShapeDtypeStruct((t // tm, MLA_HEADS * MLA_VT_ROWS, tm), BF16)],
        compiler_params=_cparams(("parallel",)),
        name="mla_proj",
    )(h, h, h, positions, positions.reshape(1, t), gq.reshape(1, -1), gkv.reshape(1, -1), wqt, wk, wvt)


MLA_TQ = 2048
MLA_TK = 512
MLA_QBLK = 256
MLA_KBLK = 128


def _mla_attn_kernel(qt_ref, k_ref, vt_ref, o_ref, sa_ref, sb_ref, xa_ref, xb_ref, pa_ref, pb_ref,
                     m_ref, alpha_ref, acc_ref):
    i = pl.program_id(2)
    tq, tk = MLA_TQ, MLA_TK
    m_ref[...] = jnp.full(m_ref.shape, -jnp.inf, F32)
    alpha_ref[...] = jnp.ones(alpha_ref.shape, F32)
    acc_ref[...] = jnp.zeros(acc_ref.shape, F32)
    pb_ref[...] = jnp.zeros(pb_ref.shape, BF16)

    def scores_into(s_ref, x_ref, t, q_lo=0):
        r = pl.ds(pl.multiple_of(t * tk, tk), tk)
        s = jnp.dot(k_ref[r, :], qt_ref[:, q_lo:], preferred_element_type=F32)
        s_ref[:, q_lo:] = s
        x_ref[:, q_lo:] = jnp.max(s, axis=0, keepdims=True)

    def add_values(p_ref, vt_tile, q_lo=0):
        acc_ref[:, q_lo:] = alpha_ref[:, q_lo:] * acc_ref[:, q_lo:] + jnp.dot(
            vt_tile, p_ref[:, q_lo:], preferred_element_type=F32)

    def softmax_terms(s_ref, x_ref, p_ref, chunk_shift, q_lo=0):
        for q0 in range(q_lo, tq, MLA_QBLK):
            cols = slice(q0, q0 + MLA_QBLK)

            def masked(s, k0):
                kc = (lax.broadcasted_iota(jnp.int32, s.shape, 0) + k0) // CHUNK + chunk_shift
                qc = (lax.broadcasted_iota(jnp.int32, s.shape, 1) + q0) // CHUNK
                return jnp.where(kc <= qc, s, MASK_VALUE)

            def visibility(k0):
                if chunk_shift is None:
                    return "all"
                k_min, k_max = k0 // CHUNK + chunk_shift, (k0 + MLA_KBLK - 1) // CHUNK + chunk_shift
                q_min, q_max = q0 // CHUNK, (q0 + MLA_QBLK - 1) // CHUNK
                return "all" if k_max <= q_min else "none" if k_min > q_max else "some"

            key_blocks = [(k0, visibility(k0)) for k0 in range(0, tk, MLA_KBLK)]
            if all(vis == "all" for _, vis in key_blocks):
                tile_max = x_ref[:, cols]
            else:
                tile_max = functools.reduce(jnp.maximum, [
                    jnp.max(s_ref[k0:k0 + MLA_KBLK, cols] if vis == "all"
                            else masked(s_ref[k0:k0 + MLA_KBLK, cols], k0), axis=0, keepdims=True)
                    for k0, vis in key_blocks if vis != "none"])
            m_old = m_ref[:, cols]
            m_new = jnp.maximum(m_old, tile_max)
            m_ref[:, cols] = m_new
            alpha_ref[:, cols] = jnp.exp2(m_old - m_new)
            for k0, vis in key_blocks:
                rows = slice(k0, k0 + MLA_KBLK)
                if vis == "none":
                    p_ref[rows, cols] = jnp.zeros((MLA_KBLK, MLA_QBLK), BF16)
                    continue
                s = s_ref[rows, cols]
                if vis == "some":
                    s = masked(s, k0)
                p_ref[rows, cols] = jnp.exp2(s - m_new).astype(BF16)

    nt = tq // tk
    first_diag = nt * i

    def pair(u, d):
        def lo(dd):
            return 0 if dd is None or dd < 0 else dd * tk

        def shift(dd):
            return None if dd is None else dd * (tk // CHUNK)

        d_odd = None if d is None else d + 1
        d_prev = None if d is None else d - 1
        scores_into(sb_ref, xb_ref, 2 * u + 1, lo(d_odd))
        add_values(pb_ref, vt_ref[jnp.maximum(2 * u - 1, 0)], lo(d_prev))
        softmax_terms(sa_ref, xa_ref, pa_ref, shift(d), lo(d))
        if d is None:
            scores_into(sa_ref, xa_ref, 2 * u + 2)
        elif d + 2 < nt:
            scores_into(sa_ref, xa_ref, 2 * u + 2, lo(d + 2))
        add_values(pa_ref, vt_ref[2 * u], lo(d))
        softmax_terms(sb_ref, xb_ref, pb_ref, shift(d_odd), lo(d_odd))

    scores_into(sa_ref, xa_ref, 0)

    def body(u, carry):
        pair(u, None)
        return carry

    lax.fori_loop(0, first_diag // 2, body, 0)
    for d in range(0, nt, 2):
        pair((first_diag + d) // 2, d)
    add_values(pb_ref, vt_ref[first_diag + nt - 1], (nt - 1) * tk)
    o_ref[...] = (acc_ref[:MLA_V, :] / acc_ref[MLA_V:MLA_V + 1, :]).T


def _mla_attn(qt, k, vt, batch, seq):
    tq = MLA_TQ
    nq = seq // tq
    assert vt.shape[2] == MLA_TK
    return pl.pallas_call(
        _mla_attn_kernel,
        grid=(batch, MLA_HEADS, nq),
        in_specs=[pl.BlockSpec((MLA_QK_PAD, tq), lambda b, h, i: (h, b * nq + i)),
                  pl.BlockSpec((seq, MLA_QK_PAD), lambda b, h, i: (b, h)),
                  pl.BlockSpec((seq // MLA_TK, MLA_VT_ROWS, MLA_TK), lambda b, h, i: (b, h, 0))],
        out_specs=pl.BlockSpec((tq, MLA_V), lambda b, h, i: (b * nq + i, h)),
        out_shape=jax.ShapeDtypeStruct((batch * seq, MLA_WIDTH), F32),
        scratch_shapes=[pltpu.VMEM((MLA_TK, tq), F32), pltpu.VMEM((MLA_TK, tq), F32),
                        pltpu.VMEM((1, tq), F32), pltpu.VMEM((1, tq), F32),
                        pltpu.VMEM((MLA_TK, tq), BF16), pltpu.VMEM((MLA_TK, tq), BF16),
                        pltpu.VMEM((1, tq), F32), pltpu.VMEM((1, tq), F32),
                        pltpu.VMEM((MLA_VT_ROWS, tq), F32)],
        compiler_params=_cparams(("parallel", "parallel", "arbitrary")),
        name="mla_attn",
    )(qt, k, vt)


def _block_cumsum(x, row):
    r = row % HG_BLOCK
    s = 1
    while s < HG_BLOCK:
        x = x + jnp.where(r >= s, pltpu.roll(x, s, 0), 0.0)
        s *= 2
    return x


def _hgrn_chunk(hq, hf, hi, lb, state):
    c = HG_CHUNK
    nb = c // HG_BLOCK
    row = lax.broadcasted_iota(jnp.int32, (c, HG_DIM), 0)
    col = lax.broadcasted_iota(jnp.int32, (c, HG_DIM), 1)

    q = hq * jax.nn.sigmoid(hq)
    f = lb + (1.0 - lb) * jax.nn.sigmoid(hf)
    k = (1.0 - lb) * jax.nn.sigmoid(-hf)
    b = _block_cumsum(jnp.log(jnp.maximum(f, TINY)), row)
    b3 = b.reshape(nb, HG_BLOCK, HG_DIM)
    b_last3 = jnp.broadcast_to(b3[:, HG_BLOCK - 1:HG_BLOCK, :], b3.shape)
    b_last = b_last3.reshape(c, HG_DIM)

    q3 = q.reshape(nb, HG_BLOCK, HG_DIM)
    k3 = k.reshape(nb, HG_BLOCK, HG_DIM)
    v3 = hi.reshape(nb, HG_BLOCK, HG_DIM)
    irow = lax.broadcasted_iota(jnp.int32, b3.shape, 1)
    b3_log2 = b3 * math.log2(math.e)
    sub = SUBLANE
    ws = []
    for j in range(HG_BLOCK):
        lo = (j // sub) * sub
        w = q3[:, lo:, :] * k3[:, j:j + 1, :] * jnp.exp2(b3_log2[:, lo:, :] - b3_log2[:, j:j + 1, :])
        if j > lo:
            w = jnp.where(irow[:, lo:, :] >= j, w, 0.0)
        ws.append(w.reshape(nb * (HG_BLOCK - lo), HG_DIM))
    sums = jnp.dot(jnp.concatenate(ws, axis=0).astype(BF16), jnp.ones((HG_DIM, HG_DIM), BF16),
                   preferred_element_type=F32)
    o_parts = [jnp.zeros((nb, sub, HG_DIM), F32) for _ in range(HG_BLOCK // sub)]
    r_at = 0
    for j in range(HG_BLOCK):
        lo = (j // sub) * sub
        n_rows = nb * (HG_BLOCK - lo)
        a_j = sums[r_at:r_at + n_rows, :].reshape(nb, HG_BLOCK - lo, HG_DIM)
        r_at += n_rows
        wv = a_j * v3[:, j:j + 1, :]
        for part in range(lo // sub, HG_BLOCK // sub):
            r0 = part * sub - lo
            o_parts[part] = o_parts[part] + wv[:, r0:r0 + sub, :]
    o = jnp.concatenate(o_parts, axis=1).reshape(c, HG_DIM)

    q_dec = q * jnp.exp(b)
    k_dec_t = (k * jnp.exp(b_last - b)).T
    b_last_t = b_last.T
    v_bf = hi.astype(BF16)
    blk_of_col = col // HG_BLOCK
    k_stack = jnp.concatenate(
        [jnp.where(blk_of_col == j, k_dec_t, 0.0) for j in range(nb)], axis=0).astype(BF16)
    u_all = jnp.dot(k_stack, v_bf, preferred_element_type=F32)
    states = []
    for j in range(nb):
        states.append(state.astype(BF16))
        decay = jnp.exp(b_last_t[:, j * HG_BLOCK:j * HG_BLOCK + 1])
        state = decay * state + u_all[j * HG_DIM:(j + 1) * HG_DIM, :]
    s_stack = jnp.concatenate(states, axis=0)
    blk_of_row = row // HG_BLOCK
    q_exp = jnp.concatenate(
        [jnp.where(blk_of_row == j, q_dec, 0.0) for j in range(nb)], axis=1).astype(BF16)
    o = o + jnp.dot(q_exp, s_stack, preferred_element_type=F32)
    return o, state


def _hgrn_kernel(hq_ref, hf_ref, hi_ref, hg_ref, lbraw_ref, gn_ref, o_ref, state_ref, *, layer):
    @pl.when(pl.program_id(2) == 0)
    def _():
        state_ref[...] = jnp.zeros(state_ref.shape, F32)

    raw = lbraw_ref[...]
    e = jnp.exp(raw - jnp.max(raw, axis=0, keepdims=True))
    p = e / jnp.sum(e, axis=0, keepdims=True)
    lb = jnp.sum(p[:layer + 1, :], axis=0, keepdims=True) - p[0:1, :]

    def body(ci, carry):
        r = pl.ds(pl.multiple_of(ci * HG_CHUNK, HG_CHUNK), HG_CHUNK)
        o, state = _hgrn_chunk(hq_ref[r, :], hf_ref[r, :], hi_ref[r, :], lb, state_ref[...])
        state_ref[...] = state
        o = (o * _rms_scale(o)) * gn_ref[...]
        g = hg_ref[r, :]
        o_ref[r, :] = (o * (g * jax.nn.sigmoid(g))).astype(o_ref.dtype)
        return carry

    lax.fori_loop(0, hq_ref.shape[0] // HG_CHUNK, body, 0, unroll=2)


def _hgrn(h, lb_raw, gn, layer, batch, seq):
    rows = HG_ROWS
    nr = seq // rows
    depth = lb_raw.shape[0]

    def col_spec(col0):
        return pl.BlockSpec((rows, HG_DIM), lambda b, hh, c: (b * nr + c, col0 // HG_DIM + hh))

    return pl.pallas_call(
        functools.partial(_hgrn_kernel, layer=layer),
        grid=(batch, HG_HEADS, nr),
        in_specs=[col_spec(COL_HQ), col_spec(COL_HF), col_spec(COL_HI), col_spec(COL_HG),
                  pl.BlockSpec((depth, HG_DIM), lambda b, hh, c: (0, hh)),
                  pl.BlockSpec((1, HG_DIM), lambda b, hh, c: (0, hh))],
        out_specs=pl.BlockSpec((rows, HG_DIM), lambda b, hh, c: (b * nr + c, hh)),
        out_shape=jax.ShapeDtypeStruct((batch * seq, HG_WIDTH), BF16),
        scratch_shapes=[pltpu.VMEM((HG_DIM, HG_DIM), F32)],
        compiler_params=_cparams(("parallel", "parallel", "arbitrary")),
        name="hgrn",
    )(h, h, h, h, lb_raw, gn.reshape(1, -1))


def _ca_bias_rows(rel_bias):
    idx = jnp.arange(CA_BIAS_LEN)
    m = jnp.where(idx < CA_WIN, idx, idx - CA_BIAS_LEN)
    bucket = jnp.clip(2 * CA_TQ - m, -CA_REL_CLIP, CA_REL_CLIP) + CA_REL_CLIP
    return rel_bias[:, bucket].astype(F32)


def _ca_kernel(q_ref, k0_ref, k1_ref, k2_ref, v0_ref, v1_ref, v2_ref, brow_ref, gn_ref,
               o_ref, bias_ref):
    t = pl.program_id(1)
    tq = CA_TQ

    log2e = math.log2(math.e)

    @pl.when((pl.program_id(0) == 0) & (t == 0))
    def _():
        qc = lax.broadcasted_iota(jnp.int32, (tq, CA_WIN), 0) // CHUNK
        kk = lax.broadcasted_iota(jnp.int32, (tq, CA_WIN), 1)
        kc = kk // CHUNK
        band = (kc >= qc) & (kc <= qc + CA_LEFT_CHUNKS)
        for h in range(CA_HEADS):
            rows = jnp.broadcast_to(brow_ref[h:h + 1, :], (tq, CA_BIAS_LEN))
            rolled = pltpu.roll(rows, 0, 1, stride=1, stride_axis=0)
            base = jnp.where(band, rolled[:, :CA_WIN] * log2e, MASK_VALUE)
            for v in range(CA_SEQ_VARIANTS):
                first_valid = max(2 * tq - v * tq, 0)
                bias_ref[v, h] = jnp.where(kk >= first_valid, base, MASK_VALUE) if first_valid else base

    variant = jnp.minimum(t, CA_SEQ_VARIANTS - 1)
    c_exp = CA_DIM ** -0.5 * log2e
    ones = jnp.ones((CA_WIN, CA_DIM), BF16)
    outs = []
    for h in range(CA_HEADS):
        c = slice(h * CA_DIM, (h + 1) * CA_DIM)
        qh = (q_ref[:, c] * c_exp).astype(BF16)
        kh = jnp.concatenate([k0_ref[:, c], k1_ref[:, c], k2_ref[:, c]], axis=0).astype(BF16)
        vh = jnp.concatenate([v0_ref[:, c], v1_ref[:, c], v2_ref[:, c]], axis=0).astype(BF16)
        s = lax.dot_general(qh, kh, (((1,), (1,)), ((), ())), preferred_element_type=F32)
        s = s + bias_ref[variant, h]
        p = jnp.exp2(s - jnp.max(s, axis=-1, keepdims=True)).astype(BF16)
        ov = jnp.dot(p, jnp.concatenate([vh, ones], axis=1), preferred_element_type=F32)
        outs.append(ov[:, :CA_DIM] / ov[:, CA_DIM:CA_DIM + 1])
    o = jnp.concatenate(outs, axis=1)
    o_ref[...] = ((o * _rms_scale(o)) * gn_ref[...]).astype(o_ref.dtype)


def _chunk_attn(h, rel_bias, gn, batch, seq):
    tq = CA_TQ
    nq = seq // tq
    cq, ck, cv = COL_AQ // CA_WIDTH, COL_AK // CA_WIDTH, COL_AV // CA_WIDTH

    def kv_spec(col, back):
        return pl.BlockSpec((tq, CA_WIDTH), lambda b, t: (b * nq + jnp.maximum(t - back, 0), col))

    return pl.pallas_call(
        _ca_kernel,
        grid=(batch, nq),
        in_specs=[pl.BlockSpec((tq, CA_WIDTH), lambda b, t: (b * nq + t, cq)),
                  kv_spec(ck, 2), kv_spec(ck, 1), kv_spec(ck, 0),
                  kv_spec(cv, 2), kv_spec(cv, 1), kv_spec(cv, 0),
                  pl.BlockSpec((CA_HEADS, CA_BIAS_LEN), lambda b, t: (0, 0)),
                  pl.BlockSpec((1, CA_WIDTH), lambda b, t: (0, 0))],
        out_specs=pl.BlockSpec((tq, CA_WIDTH), lambda b, t: (b * nq + t, 0)),
        out_shape=jax.ShapeDtypeStruct((batch * seq, CA_WIDTH), BF16),
        scratch_shapes=[pltpu.VMEM((CA_SEQ_VARIANTS, CA_HEADS, tq, CA_WIN), F32)],
        compiler_params=_cparams(("arbitrary", "arbitrary")),
        name="chunk_attn",
    )(h, h, h, h, h, h, h, _ca_bias_rows(rel_bias), gn.reshape(1, -1))


X_SLICES = 8


def _residual_norm(o_ref, xs_ref, g_ref):
    rows, d = o_ref.shape
    w = d // X_SLICES

    step = NORM_ROWS * NORM_UNROLL

    def body(c, carry):
        base = pl.multiple_of(c * step, step)
        chunks = [pl.ds(base + u * NORM_ROWS, NORM_ROWS) for u in range(NORM_UNROLL)]
        ys = [o_ref[r, :] for r in chunks]
        yns = [(y * _rms_scale(y)) * g_ref[...] for y in ys]
        for r, yn in zip(chunks, yns):
            for p in range(X_SLICES):
                o_ref[r, p * w:(p + 1) * w] = xs_ref[p, r, :] + yn[:, p * w:(p + 1) * w]
        return carry

    lax.fori_loop(0, rows // step, body, 0)


def _x_slice_spec(tm, d):
    return pl.BlockSpec((tm, d // X_SLICES), lambda i, k: (i, jnp.minimum(k, X_SLICES - 1)))


def _accumulate_then_residual_norm(a_ref, w_ref, x_ref, g_ref, o_ref, xs_ref, ragged):
    k = pl.program_id(1)
    last = pl.num_programs(1) - 1

    @pl.when(k < X_SLICES)
    def _():
        xs_ref[k] = x_ref[...]

    @pl.when(k == 0)
    def _():
        o_ref[...] = jnp.dot(a_ref[...], w_ref[...], preferred_element_type=F32)

    if ragged is None:
        @pl.when(k > 0)
        def _():
            o_ref[...] += jnp.dot(a_ref[...], w_ref[...], preferred_element_type=F32)
    else:
        @pl.when((k > 0) & (k < last))
        def _():
            o_ref[...] += jnp.dot(a_ref[...], w_ref[...], preferred_element_type=F32)

        @pl.when(k == last)
        def _():
            a, w = a_ref[...], w_ref[...]
            a = jnp.where(lax.broadcasted_iota(jnp.int32, a.shape, 1) < ragged, a, jnp.zeros_like(a))
            w = jnp.where(lax.broadcasted_iota(jnp.int32, w.shape, 0) < ragged, w, jnp.zeros_like(w))
            o_ref[...] += jnp.dot(a, w, preferred_element_type=F32)

    @pl.when(k == last)
    def _():
        _residual_norm(o_ref, xs_ref, g_ref)


OUT_TK = 512


def _out_proj_kernel(mla_ref, hg_ref, ca_ref, w_ref, x_ref, gm_ref, gp_ref, o_ref, a_ref, xs_ref):
    @pl.when(pl.program_id(1) == 0)
    def _():
        rows = mla_ref.shape[0]
        n_mla, n_hg, n_ca = MLA_WIDTH // OUT_TK, HG_WIDTH // OUT_TK, CA_WIDTH // OUT_TK

        def body(c, carry):
            r = pl.ds(pl.multiple_of(c * NORM_ROWS, NORM_ROWS), NORM_ROWS)
            m = mla_ref[r, :]
            mn = ((m * _rms_scale(m)) * gm_ref[...]).astype(BF16)
            for p in range(n_mla):
                a_ref[p, r, :] = mn[:, p * OUT_TK:(p + 1) * OUT_TK]
            return carry

        lax.fori_loop(0, rows // NORM_ROWS, body, 0, unroll=NORM_UNROLL)
        for p in range(n_hg):
            a_ref[n_mla + p] = hg_ref[:, p * OUT_TK:(p + 1) * OUT_TK]
        for p in range(n_ca):
            a_ref[n_mla + n_hg + p] = ca_ref[:, p * OUT_TK:(p + 1) * OUT_TK]

    _accumulate_then_residual_norm(a_ref.at[pl.program_id(1)], w_ref, x_ref, gp_ref, o_ref, xs_ref, None)


def _out_proj(o_mla, o_hg, o_ca, x, g_mla, g_post, w, layer, tm):
    t, d = x.shape
    kdim = w.shape[1]
    nk = kdim // OUT_TK
    assert MLA_WIDTH % OUT_TK == 0 and HG_WIDTH % OUT_TK == 0 and CA_WIDTH % OUT_TK == 0
    assert kdim == MLA_WIDTH + HG_WIDTH + CA_WIDTH and nk >= X_SLICES
    return pl.pallas_call(
        _out_proj_kernel,
        grid=(t // tm, nk),
        in_specs=[pl.BlockSpec((tm, MLA_WIDTH), lambda i, k: (i, 0)),
                  pl.BlockSpec((tm, HG_WIDTH), lambda i, k: (i, 0)),
                  pl.BlockSpec((tm, CA_WIDTH), lambda i, k: (i, 0)),
                  pl.BlockSpec((None, OUT_TK, d), lambda i, k: (layer, k, 0)),
                  _x_slice_spec(tm, d),
                  pl.BlockSpec((1, MLA_WIDTH), lambda i, k: (0, 0)),
                  pl.BlockSpec((1, d), lambda i, k: (0, 0))],
        out_specs=pl.BlockSpec((tm, d), lambda i, k: (i, 0)),
        out_shape=jax.ShapeDtypeStruct((t, d), F32),
        scratch_shapes=[pltpu.VMEM((nk, tm, OUT_TK), BF16),
                        pltpu.VMEM((X_SLICES, tm, d // X_SLICES), F32)],
        compiler_params=_cparams(("parallel", "arbitrary")),
        name="out_proj",
    )(o_mla, o_hg, o_ca, w, x, g_mla.reshape(1, -1), g_post.reshape(1, -1))


def _down_proj_kernel(a_ref, w_ref, x_ref, g_ref, o_ref, xs_ref, *, ragged):
    _accumulate_then_residual_norm(a_ref, w_ref, x_ref, g_ref, o_ref, xs_ref, ragged)


def _down_proj(a, w, x, g, layer, tm, tk):
    t, d = x.shape
    kdim = a.shape[1]
    ragged = kdim % tk or None
    nk = pl.cdiv(kdim, tk)
    assert nk >= X_SLICES
    return pl.pallas_call(
        functools.partial(_down_proj_kernel, ragged=ragged),
        grid=(t // tm, nk),
        in_specs=[pl.BlockSpec((tm, tk), lambda i, k: (i, k)),
                  pl.BlockSpec((None, tk, d), lambda i, k: (layer, k, 0)),
                  _x_slice_spec(tm, d),
                  pl.BlockSpec((1, d), lambda i, k: (0, 0))],
        out_specs=pl.BlockSpec((tm, d), lambda i, k: (i, 0)),
        out_shape=jax.ShapeDtypeStruct((t, d), F32),
        scratch_shapes=[pltpu.VMEM((X_SLICES, tm, d // X_SLICES), F32)],
        compiler_params=_cparams(("parallel", "arbitrary")),
        name="down_proj",
    )(a, w, x, g.reshape(1, -1))


PREP_ROWS = 256


def _prep_w_in(w):
    depth, d, _ = w.shape
    o_ckv = MLA_Q_RANK
    o_kr = o_ckv + MLA_KV_RANK
    o_hg = o_kr + MLA_ROPE
    o_ca = o_hg + 4 * HG_WIDTH
    assert (COL_AQ, COL_AK, COL_AV) == (0, CA_WIDTH, 2 * CA_WIDTH) and COL_HQ == 3 * CA_WIDTH
    assert (COL_HF, COL_HI, COL_HG) == (COL_HQ + HG_WIDTH, COL_HQ + 2 * HG_WIDTH, COL_HQ + 3 * HG_WIDTH)
    assert COL_CKV == COL_HG + HG_WIDTH and COL_CQ == COL_CKV + MLA_KV_RANK and COL_KR == COL_CQ + MLA_Q_RANK
    n_in = w.shape[2]

    def regroup_kernel(w_ref, o_ref):
        x = w_ref[...]
        z = jnp.zeros((x.shape[0], D_IN_PAD - COL_KR - MLA_ROPE), x.dtype)
        o_ref[...] = jnp.concatenate([x[:, o_ca:], x[:, o_hg:o_ca], x[:, o_ckv:o_kr], x[:, :o_ckv],
                                      x[:, o_kr:o_hg], z], axis=1).astype(BF16)

    rows = PREP_ROWS
    return pl.pallas_call(
        regroup_kernel,
        grid=(depth, d // rows),
        in_specs=[pl.BlockSpec((None, rows, n_in), lambda l, i: (l, i, 0))],
        out_specs=pl.BlockSpec((None, rows, D_IN_PAD), lambda l, i: (l, i, 0)),
        out_shape=jax.ShapeDtypeStruct((depth, d, D_IN_PAD), BF16),
        compiler_params=_cparams(("parallel", "parallel")),
        name="w_in_regroup",
    )(w)


def _prep_w_uq(w):
    depth, r, _ = w.shape
    w4 = w.astype(BF16).reshape(depth, r, MLA_HEADS, MLA_NOPE + MLA_ROPE)
    pad = jnp.zeros((depth, r, MLA_HEADS, MLA_QK_PAD - MLA_NOPE - MLA_ROPE), BF16)
    wq = jnp.concatenate([w4, pad], axis=3).reshape(depth, r, MLA_HEADS * MLA_QK_PAD)
    return jnp.swapaxes(wq, 1, 2)


def _prep_w_ukv(w):
    depth, r, _ = w.shape
    w4 = w.astype(BF16).reshape(depth, r, MLA_HEADS, MLA_NOPE + MLA_V)
    wk = w4[:, :, :, :MLA_NOPE].reshape(depth, r, MLA_HEADS * MLA_NOPE)
    wv = w4[:, :, :, MLA_NOPE:].reshape(depth, r, MLA_HEADS * MLA_V)
    return wk, jnp.swapaxes(wv, 1, 2)


def kernel(x, positions, attn_pre_norm, attn_post_norm, w_in, mla_q_norm, mla_kv_norm, w_uq, w_ukv,
           mla_out_norm, hg_lower_bounds, hg_out_norm, ca_rel_bias, ca_out_norm, w_out, ffn_pre_norm,
           ffn_post_norm, w_gate, w_up, w_down):
    batch, seq, d = x.shape
    t = batch * seq
    depth = w_in.shape[0]
    xf = x.reshape(t, d)
    pos = positions.reshape(t, 1)
    w_in_b = _prep_w_in(w_in)
    w_uq_b = _prep_w_uq(w_uq)
    w_uk_b, w_uvt_b = _prep_w_ukv(w_ukv)
    w_out_b, w_gate_b, w_up_b, w_down_b = (w.astype(BF16) for w in (w_out, w_gate, w_up, w_down))
    for l in range(depth):
        h = _norm_matmul(xf, attn_pre_norm[l], [w_in_b], l, F32, tm=NORM_MATMUL_TM, tn=NORM_MATMUL_TN)
        q, k, vt = _mla_proj(h, pos, mla_q_norm[l], mla_kv_norm[l], w_uq_b, w_uk_b, w_uvt_b, l, tm=MLA_TK)
        o_mla = _mla_attn(q, k, vt, batch, seq)
        o_hg = _hgrn(h, hg_lower_bounds, hg_out_norm[l], l, batch, seq)
        o_ca = _chunk_attn(h, ca_rel_bias[l], ca_out_norm[l], batch, seq)
        xf = _out_proj(o_mla, o_hg, o_ca, xf, mla_out_norm[l], attn_post_norm[l], w_out_b, l, tm=RESIDUAL_TM)
        hid = _norm_matmul(xf, ffn_pre_norm[l], [w_gate_b, w_up_b], l, BF16,
                           tm=NORM_MATMUL_TM, tn=NORM_MATMUL_TN)
        xf = _down_proj(hid, w_down_b, xf, ffn_post_norm[l], l, tm=RESIDUAL_TM, tk=DOWN_TK)
    return xf.reshape(batch, seq, d)
```

```python
import functools
import math

import jax
import jax.numpy as jnp
from jax import lax
from jax.experimental import pallas as pl
from jax.experimental.pallas import tpu as pltpu

F32 = jnp.float32
BF16 = jnp.bfloat16

EPS = 1e-6
MASK_VALUE = -1e30
TINY = 1e-30
CHUNK = 64

MLA_HEADS = 16
MLA_Q_RANK = 768
MLA_KV_RANK = 512
MLA_NOPE = 128
MLA_ROPE = 64
MLA_V = 128
ROPE_THETA = 10000.0
MLA_QK_PAD = 256
MLA_ONES = 16
MLA_VT_ROWS = MLA_V + MLA_ONES
MLA_EXP_SCALE = (MLA_NOPE + MLA_ROPE) ** -0.5 * math.log2(math.e)

HG_HEADS = 8
HG_DIM = 128
HG_BLOCK = 16
HG_CHUNK = 128
HG_ROWS = 1024

CA_HEADS = 8
CA_DIM = 128
CA_LEFT_CHUNKS = 8
CA_REL_CLIP = 256
CA_TQ = 256
CA_WIN = 3 * CA_TQ
CA_BIAS_LEN = 1024
CA_SEQ_VARIANTS = 3

MLA_WIDTH = MLA_HEADS * MLA_V
HG_WIDTH = HG_HEADS * HG_DIM
CA_WIDTH = CA_HEADS * CA_DIM

COL_AQ = 0
COL_AK = 1024
COL_AV = 2048
COL_HQ = 3072
COL_HF = 4096
COL_HI = 5120
COL_HG = 6144
COL_CKV = 7168
COL_CQ = 7680
COL_KR = 8448
D_IN_PAD = 8704

LANE = 128
SUBLANE = 8
V7X_VMEM_BYTES = 64 * 1024 * 1024
VMEM_LIMIT = V7X_VMEM_BYTES * 7 // 8

NORM_MATMUL_TM = 1024
NORM_MATMUL_TN = 512
RESIDUAL_TM = 512
DOWN_TK = 1024


VMEM_LIMIT_HIGH = V7X_VMEM_BYTES * 15 // 16


def _cparams(sem, vmem=VMEM_LIMIT):
    return pltpu.CompilerParams(dimension_semantics=sem, vmem_limit_bytes=vmem)


def _rms_scale(x):
    return lax.rsqrt(jnp.mean(x * x, axis=-1, keepdims=True) + EPS)


NORM_ROWS = 16
NORM_UNROLL = 4


def _norm_rows_into(xn_ref, x_ref, g_ref):
    rows = x_ref.shape[0]

    def body(c, carry):
        r = pl.ds(pl.multiple_of(c * NORM_ROWS, NORM_ROWS), NORM_ROWS)
        x = x_ref[r, :]
        xn_ref[r, :] = ((x * _rms_scale(x)) * g_ref[...]).astype(BF16)
        return carry

    lax.fori_loop(0, rows // NORM_ROWS, body, 0, unroll=NORM_UNROLL)


def _norm_matmul_kernel(x_ref, g_ref, w_ref, o_ref, xn_ref):
    @pl.when(pl.program_id(1) == 0)
    def _():
        _norm_rows_into(xn_ref, x_ref, g_ref)

    o_ref[...] = jnp.dot(xn_ref[...], w_ref[...], preferred_element_type=F32).astype(o_ref.dtype)


def _norm_swiglu_kernel(x_ref, g_ref, wg_ref, wu_ref, o_ref, xn_ref):
    @pl.when(pl.program_id(1) == 0)
    def _():
        _norm_rows_into(xn_ref, x_ref, g_ref)

    xn = xn_ref[...]
    gate = jnp.dot(xn, wg_ref[...], preferred_element_type=F32)
    up = jnp.dot(xn, wu_ref[...], preferred_element_type=F32)
    o_ref[...] = ((gate * jax.nn.sigmoid(gate)) * up).astype(o_ref.dtype)


def _norm_matmul(x, g, ws, layer, out_dtype, tm, tn):
    t, d = x.shape
    n = ws[0].shape[2]
    kern = _norm_matmul_kernel if len(ws) == 1 else _norm_swiglu_kernel
    w_specs = [pl.BlockSpec((None, d, tn), lambda i, j: (layer, 0, j)) for _ in ws]
    x_buffers = 2 if len(ws) == 1 else 1
    return pl.pallas_call(
        kern,
        grid=(t // tm, pl.cdiv(n, tn)),
        in_specs=[pl.BlockSpec((tm, d), lambda i, j: (i, 0), pipeline_mode=pl.Buffered(x_buffers)),
                  pl.BlockSpec((1, d), lambda i, j: (0, 0))] + w_specs,
        out_specs=pl.BlockSpec((tm, tn), lambda i, j: (i, j)),
        out_shape=jax.ShapeDtypeStruct((t, n), out_dtype),
        scratch_shapes=[pltpu.VMEM((tm, d), BF16)],
        compiler_params=_cparams(("parallel", "arbitrary"), VMEM_LIMIT_HIGH if x_buffers == 2 else VMEM_LIMIT),
        name="norm_matmul" if len(ws) == 1 else "norm_swiglu",
    )(x, g.reshape(1, d), *ws)


def _rope_tables(pos_ref):
    lane = lax.broadcasted_iota(jnp.int32, (1, LANE), 1)
    half = MLA_ROPE // 2
    idx = (lane % half).astype(F32)
    inv_freq = jnp.exp((-math.log(ROPE_THETA) * 2.0) * idx / MLA_ROPE)
    inv_freq = jnp.where(lane < MLA_ROPE, inv_freq, 0.0)
    ang = pos_ref[...].astype(F32) * inv_freq
    cos, sin = jnp.cos(ang), jnp.sin(ang)
    sin_hi = jnp.where((lane >= half) & (lane < MLA_ROPE), sin, 0.0)
    sin_lo = jnp.where(lane < half, -sin, 0.0)
    return cos, sin_hi, sin_lo


def _rope(x, tables):
    cos, sin_hi, sin_lo = tables
    half = MLA_ROPE // 2
    return x * cos + pltpu.roll(x, half, 1) * sin_hi + pltpu.roll(x, LANE - half, 1) * sin_lo


def _rope_tables_t(pos_row_ref):
    half = MLA_ROPE // 2
    idx = lax.broadcasted_iota(jnp.int32, (half, 1), 0).astype(F32)
    inv_freq = jnp.exp((-math.log(ROPE_THETA) * 2.0) * idx / MLA_ROPE)
    ang = inv_freq * pos_row_ref[...].astype(F32)
    return jnp.cos(ang), jnp.sin(ang)


def _mla_proj_kernel(cq_ref, ckv_ref, kr_ref, pos_ref, pos_row_ref, gq_ref, gkv_ref, wqt_ref, wk_ref, wvt_ref,
                     qt_ref, k_ref, vt_ref):
    tables = _rope_tables(pos_ref)
    cos_t, sin_t = _rope_tables_t(pos_row_ref)
    cq = cq_ref[...]
    cqn = ((cq * _rms_scale(cq)) * gq_ref[...]).astype(BF16)
    ckv = ckv_ref[...]
    ckvn = ((ckv * _rms_scale(ckv)) * gkv_ref[...]).astype(BF16)
    k_pe = _rope(kr_ref[...], tables).astype(BF16)
    for h in range(MLA_HEADS):
        c0 = h * MLA_QK_PAD
        kh = jnp.dot(ckvn, wk_ref[:, h * LANE:(h + 1) * LANE], preferred_element_type=F32)
        k_ref[:, c0:c0 + LANE] = kh.astype(BF16)
        k_ref[:, c0 + LANE:c0 + MLA_QK_PAD] = k_pe
    group = 4
    half = MLA_ROPE // 2
    nt_dims = (((1,), (1,)), ((), ()))
    for h0 in range(0, MLA_HEADS, group):
        qt = lax.dot_general(wqt_ref[h0 * MLA_QK_PAD:(h0 + group) * MLA_QK_PAD, :], cqn, nt_dims,
                             preferred_element_type=F32) * MLA_EXP_SCALE
        for h in range(h0, h0 + group):
            r0, g0 = h * MLA_QK_PAD, (h - h0) * MLA_QK_PAD
            t1 = qt[g0 + MLA_NOPE:g0 + MLA_NOPE + half, :]
            t2 = qt[g0 + MLA_NOPE + half:g0 + MLA_NOPE + MLA_ROPE, :]
            qt_ref[r0:r0 + MLA_NOPE, :] = qt[g0:g0 + MLA_NOPE, :].astype(BF16)
            qt_ref[r0 + MLA_NOPE:r0 + MLA_NOPE + half, :] = (t1 * cos_t - t2 * sin_t).astype(BF16)
            qt_ref[r0 + MLA_NOPE + half:r0 + MLA_NOPE + MLA_ROPE, :] = (t1 * sin_t + t2 * cos_t).astype(BF16)
            qt_ref[r0 + MLA_NOPE + MLA_ROPE:r0 + MLA_QK_PAD, :] = (
                qt[g0 + MLA_NOPE + MLA_ROPE:g0 + MLA_QK_PAD, :].astype(BF16))
    ones = jnp.ones((MLA_ONES, vt_ref.shape[2]), BF16)
    for h0 in range(0, MLA_HEADS, group):
        vt = lax.dot_general(wvt_ref[h0 * MLA_V:(h0 + group) * MLA_V, :], ckvn, nt_dims,
                             preferred_element_type=F32).astype(BF16)
        for h in range(h0, h0 + group):
            r0 = h * MLA_VT_ROWS
            vt_ref[0, r0:r0 + MLA_V, :] = vt[(h - h0) * MLA_V:(h - h0 + 1) * MLA_V, :]
            vt_ref[0, r0 + MLA_V:r0 + MLA_VT_ROWS, :] = ones


def _mla_proj(h, positions, gq, gkv, wqt, wk, wvt, layer, tm):
    t = h.shape[0]
    qk_w = MLA_HEADS * MLA_QK_PAD
    const = lambda i: (0, 0)
    return pl.pallas_call(
        _mla_proj_kernel,
        grid=(t // tm,),
        in_specs=[pl.BlockSpec((tm, MLA_Q_RANK), lambda i: (i, COL_CQ // MLA_Q_RANK)),
                  pl.BlockSpec((tm, MLA_KV_RANK), lambda i: (i, COL_CKV // MLA_KV_RANK)),
                  pl.BlockSpec((tm, LANE), lambda i: (i, COL_KR // LANE)),
                  pl.BlockSpec((tm, 1), lambda i: (i, 0)),
                  pl.BlockSpec((1, tm), lambda i: (0, i)),
                  pl.BlockSpec((1, MLA_Q_RANK), const),
                  pl.BlockSpec((1, MLA_KV_RANK), const),
                  pl.BlockSpec((None,) + wqt.shape[1:], lambda i: (layer, 0, 0)),
                  pl.BlockSpec((None,) + wk.shape[1:], lambda i: (layer, 0, 0)),
                  pl.BlockSpec((None,) + wvt.shape[1:], lambda i: (layer, 0, 0))],
        out_specs=[pl.BlockSpec((qk_w, tm), lambda i: (0, i)),
                   pl.BlockSpec((tm, qk_w), lambda i: (i, 0)),
                   pl.BlockSpec((1, MLA_HEADS * MLA_VT_ROWS, tm), lambda i: (i, 0, 0))],
        out_shape=[jax.ShapeDtypeStruct((qk_w, t), BF16),
                   jax.ShapeDtypeStruct((t, qk_w), BF16),
                   jax.ShapeDtypeStruct((t // tm, MLA_HEADS * MLA_VT_ROWS, tm), BF16)],
        compiler_params=_cparams(("parallel",)),
        name="mla_proj",
    )(h, h, h, positions, positions.reshape(1, t), gq.reshape(1, -1), gkv.reshape(1, -1), wqt, wk, wvt)


MLA_TQ = 2048
MLA_TK = 512
MLA_QBLK = 256
MLA_KBLK = 128


def _mla_attn_kernel(qt_ref, k_ref, vt_ref, o_ref, sa_ref, sb_ref, xa_ref, xb_ref, pa_ref, pb_ref,
                     m_ref, alpha_ref, acc_ref):
    i = pl.program_id(2)
    tq, tk = MLA_TQ, MLA_TK
    m_ref[...] = jnp.full(m_ref.shape, -jnp.inf, F32)
    alpha_ref[...] = jnp.ones(alpha_ref.shape, F32)
    acc_ref[...] = jnp.zeros(acc_ref.shape, F32)
    pb_ref[...] = jnp.zeros(pb_ref.shape, BF16)

    def scores_into(s_ref, x_ref, t, q_lo=0):
        r = pl.ds(pl.multiple_of(t * tk, tk), tk)
        s = jnp.dot(k_ref[r, :], qt_ref[:, q_lo:], preferred_element_type=F32)
        s_ref[:, q_lo:] = s
        x_ref[:, q_lo:] = jnp.max(s, axis=0, keepdims=True)

    def add_values(p_ref, vt_tile, q_lo=0):
        acc_ref[:, q_lo:] = alpha_ref[:, q_lo:] * acc_ref[:, q_lo:] + jnp.dot(
            vt_tile, p_ref[:, q_lo:], preferred_element_type=F32)

    def softmax_terms(s_ref, x_ref, p_ref, chunk_shift, q_lo=0):
        for q0 in range(q_lo, tq, MLA_QBLK):
            cols = slice(q0, q0 + MLA_QBLK)

            def masked(s, k0):
                kc = (lax.broadcasted_iota(jnp.int32, s.shape, 0) + k0) // CHUNK + chunk_shift
                qc = (lax.broadcasted_iota(jnp.int32, s.shape, 1) + q0) // CHUNK
                return jnp.where(kc <= qc, s, MASK_VALUE)

            def visibility(k0):
                if chunk_shift is None:
                    return "all"
                k_min, k_max = k0 // CHUNK + chunk_shift, (k0 + MLA_KBLK - 1) // CHUNK + chunk_shift
                q_min, q_max = q0 // CHUNK, (q0 + MLA_QBLK - 1) // CHUNK
                return "all" if k_max <= q_min else "none" if k_min > q_max else "some"

            key_blocks = [(k0, visibility(k0)) for k0 in range(0, tk, MLA_KBLK)]
            if all(vis == "all" for _, vis in key_blocks):
                tile_max = x_ref[:, cols]
            else:
                tile_max = functools.reduce(jnp.maximum, [
                    jnp.max(s_ref[k0:k0 + MLA_KBLK, cols] if vis == "all"
                            else masked(s_ref[k0:k0 + MLA_KBLK, cols], k0), axis=0, keepdims=True)
                    for k0, vis in key_blocks if vis != "none"])
            m_old = m_ref[:, cols]
            m_new = jnp.maximum(m_old, tile_max)
            m_ref[:, cols] = m_new
            alpha_ref[:, cols] = jnp.exp2(m_old - m_new)
            for k0, vis in key_blocks:
                rows = slice(k0, k0 + MLA_KBLK)
                if vis == "none":
                    p_ref[rows, cols] = jnp.zeros((MLA_KBLK, MLA_QBLK), BF16)
                    continue
                s = s_ref[rows, cols]
                if vis == "some":
                    s = masked(s, k0)
                p_ref[rows, cols] = jnp.exp2(s - m_new).astype(BF16)

    nt = tq // tk
    first_diag = nt * i

    def pair(u, d):
        def lo(dd):
            return 0 if dd is None or dd < 0 else dd * tk

        def shift(dd):
            return None if dd is None else dd * (tk // CHUNK)

        d_odd = None if d is None else d + 1
        d_prev = None if d is None else d - 1
        scores_into(sb_ref, xb_ref, 2 * u + 1, lo(d_odd))
        add_values(pb_ref, vt_ref[jnp.maximum(2 * u - 1, 0)], lo(d_prev))
        softmax_terms(sa_ref, xa_ref, pa_ref, shift(d), lo(d))
        if d is None:
            scores_into(sa_ref, xa_ref, 2 * u + 2)
        elif d + 2 < nt:
            scores_into(sa_ref, xa_ref, 2 * u + 2, lo(d + 2))
        add_values(pa_ref, vt_ref[2 * u], lo(d))
        softmax_terms(sb_ref, xb_ref, pb_ref, shift(d_odd), lo(d_odd))

    scores_into(sa_ref, xa_ref, 0)

    def body(u, carry):
        pair(u, None)
        return carry

    lax.fori_loop(0, first_diag // 2, body, 0)
    for d in range(0, nt, 2):
        pair((first_diag + d) // 2, d)
    add_values(pb_ref, vt_ref[first_diag + nt - 1], (nt - 1) * tk)
    o_ref[...] = (acc_ref[:MLA_V, :] / acc_ref[MLA_V:MLA_V + 1, :]).T


def _mla_attn(qt, k, vt, batch, seq):
    tq = MLA_TQ
    nq = seq // tq
    assert vt.shape[2] == MLA_TK
    return pl.pallas_call(
        _mla_attn_kernel,
        grid=(batch, MLA_HEADS, nq),
        in_specs=[pl.BlockSpec((MLA_QK_PAD, tq), lambda b, h, i: (h, b * nq + i)),
                  pl.BlockSpec((seq, MLA_QK_PAD), lambda b, h, i: (b, h)),
                  pl.BlockSpec((seq // MLA_TK, MLA_VT_ROWS, MLA_TK), lambda b, h, i: (b, h, 0))],
        out_specs=pl.BlockSpec((tq, MLA_V), lambda b, h, i: (b * nq + i, h)),
        out_shape=jax.ShapeDtypeStruct((batch * seq, MLA_WIDTH), F32),
        scratch_shapes=[pltpu.VMEM((MLA_TK, tq), F32), pltpu.VMEM((MLA_TK, tq), F32),
                        pltpu.VMEM((1, tq), F32), pltpu.VMEM((1, tq), F32),
                        pltpu.VMEM((MLA_TK, tq), BF16), pltpu.VMEM((MLA_TK, tq), BF16),
                        pltpu.VMEM((1, tq), F32), pltpu.VMEM((1, tq), F32),
                        pltpu.VMEM((MLA_VT_ROWS, tq), F32)],
        compiler_params=_cparams(("parallel", "parallel", "arbitrary")),
        name="mla_attn",
    )(qt, k, vt)


def _block_cumsum(x, row):
    r = row % HG_BLOCK
    s = 1
    while s < HG_BLOCK:
        x = x + jnp.where(r >= s, pltpu.roll(x, s, 0), 0.0)
        s *= 2
    return x


def _hgrn_chunk(hq, hf, hi, lb, state):
    c = HG_CHUNK
    nb = c // HG_BLOCK
    row = lax.broadcasted_iota(jnp.int32, (c, HG_DIM), 0)
    col = lax.broadcasted_iota(jnp.int32, (c, HG_DIM), 1)

    q = hq * jax.nn.sigmoid(hq)
    f = lb + (1.0 - lb) * jax.nn.sigmoid(hf)
    k = (1.0 - lb) * jax.nn.sigmoid(-hf)
    b = _block_cumsum(jnp.log(jnp.maximum(f, TINY)), row)
    b3 = b.reshape(nb, HG_BLOCK, HG_DIM)
    b_last3 = jnp.broadcast_to(b3[:, HG_BLOCK - 1:HG_BLOCK, :], b3.shape)
    b_last = b_last3.reshape(c, HG_DIM)

    q3 = q.reshape(nb, HG_BLOCK, HG_DIM)
    k3 = k.reshape(nb, HG_BLOCK, HG_DIM)
    v3 = hi.reshape(nb, HG_BLOCK, HG_DIM)
    irow = lax.broadcasted_iota(jnp.int32, b3.shape, 1)
    b3_log2 = b3 * math.log2(math.e)
    sub = SUBLANE
    ws = []
    for j in range(HG_BLOCK):
        lo = (j // sub) * sub
        w = q3[:, lo:, :] * k3[:, j:j + 1, :] * jnp.exp2(b3_log2[:, lo:, :] - b3_log2[:, j:j + 1, :])
        if j > lo:
            w = jnp.where(irow[:, lo:, :] >= j, w, 0.0)
        ws.append(w.reshape(nb * (HG_BLOCK - lo), HG_DIM))
    sums = jnp.dot(jnp.concatenate(ws, axis=0).astype(BF16), jnp.ones((HG_DIM, HG_DIM), BF16),
                   preferred_element_type=F32)
    o_parts = [jnp.zeros((nb, sub, HG_DIM), F32) for _ in range(HG_BLOCK // sub)]
    r_at = 0
    for j in range(HG_BLOCK):
        lo = (j // sub) * sub
        n_rows = nb * (HG_BLOCK - lo)
        a_j = sums[r_at:r_at + n_rows, :].reshape(nb, HG_BLOCK - lo, HG_DIM)
        r_at += n_rows
        wv = a_j * v3[:, j:j + 1, :]
        for part in range(lo // sub, HG_BLOCK // sub):
            r0 = part * sub - lo
            o_parts[part] = o_parts[part] + wv[:, r0:r0 + sub, :]
    o = jnp.concatenate(o_parts, axis=1).reshape(c, HG_DIM)

    q_dec = q * jnp.exp(b)
    k_dec_t = (k * jnp.exp(b_last - b)).T
    b_last_t = b_last.T
    v_bf = hi.astype(BF16)
    blk_of_col = col // HG_BLOCK
    k_stack = jnp.concatenate(
        [jnp.where(blk_of_col == j, k_dec_t, 0.0) for j in range(nb)], axis=0).astype(BF16)
    u_all = jnp.dot(k_stack, v_bf, preferred_element_type=F32)
    states = []
    for j in range(nb):
        states.append(state.astype(BF16))
        decay = jnp.exp(b_last_t[:, j * HG_BLOCK:j * HG_BLOCK + 1])
        state = decay * state + u_all[j * HG_DIM:(j + 1) * HG_DIM, :]
    s_stack = jnp.concatenate(states, axis=0)
    blk_of_row = row // HG_BLOCK
    q_exp = jnp.concatenate(
        [jnp.where(blk_of_row == j, q_dec, 0.0) for j in range(nb)], axis=1).astype(BF16)
    o = o + jnp.dot(q_exp, s_stack, preferred_element_type=F32)
    return o, state


def _hgrn_kernel(hq_ref, hf_ref, hi_ref, hg_ref, lbraw_ref, gn_ref, o_ref, state_ref, *, layer):
    @pl.when(pl.program_id(2) == 0)
    def _():
        state_ref[...] = jnp.zeros(state_ref.shape, F32)

    raw = lbraw_ref[...]
    e = jnp.exp(raw - jnp.max(raw, axis=0, keepdims=True))
    p = e / jnp.sum(e, axis=0, keepdims=True)
    lb = jnp.sum(p[:layer + 1, :], axis=0, keepdims=True) - p[0:1, :]

    def body(ci, carry):
        r = pl.ds(pl.multiple_of(ci * HG_CHUNK, HG_CHUNK), HG_CHUNK)
        o, state = _hgrn_chunk(hq_ref[r, :], hf_ref[r, :], hi_ref[r, :], lb, state_ref[...])
        state_ref[...] = state
        o = (o * _rms_scale(o)) * gn_ref[...]
        g = hg_ref[r, :]
        o_ref[r, :] = (o * (g * jax.nn.sigmoid(g))).astype(o_ref.dtype)
        return carry

    lax.fori_loop(0, hq_ref.shape[0] // HG_CHUNK, body, 0, unroll=2)


def _hgrn(h, lb_raw, gn, layer, batch, seq):
    rows = HG_ROWS
    nr = seq // rows
    depth = lb_raw.shape[0]

    def col_spec(col0):
        return pl.BlockSpec((rows, HG_DIM), lambda b, hh, c: (b * nr + c, col0 // HG_DIM + hh))

    return pl.pallas_call(
        functools.partial(_hgrn_kernel, layer=layer),
        grid=(batch, HG_HEADS, nr),
        in_specs=[col_spec(COL_HQ), col_spec(COL_HF), col_spec(COL_HI), col_spec(COL_HG),
                  pl.BlockSpec((depth, HG_DIM), lambda b, hh, c: (0, hh)),
                  pl.BlockSpec((1, HG_DIM), lambda b, hh, c: (0, hh))],
        out_specs=pl.BlockSpec((rows, HG_DIM), lambda b, hh, c: (b * nr + c, hh)),
        out_shape=jax.ShapeDtypeStruct((batch * seq, HG_WIDTH), BF16),
        scratch_shapes=[pltpu.VMEM((HG_DIM, HG_DIM), F32)],
        compiler_params=_cparams(("parallel", "parallel", "arbitrary")),
        name="hgrn",
    )(h, h, h, h, lb_raw, gn.reshape(1, -1))


def _ca_bias_rows(rel_bias):
    idx = jnp.arange(CA_BIAS_LEN)
    m = jnp.where(idx < CA_WIN, idx, idx - CA_BIAS_LEN)
    bucket = jnp.clip(2 * CA_TQ - m, -CA_REL_CLIP, CA_REL_CLIP) + CA_REL_CLIP
    return rel_bias[:, bucket].astype(F32)


def _ca_kernel(q_ref, k0_ref, k1_ref, k2_ref, v0_ref, v1_ref, v2_ref, brow_ref, gn_ref,
               o_ref, bias_ref):
    t = pl.program_id(1)
    tq = CA_TQ

    log2e = math.log2(math.e)

    @pl.when((pl.program_id(0) == 0) & (t == 0))
    def _():
        qc = lax.broadcasted_iota(jnp.int32, (tq, CA_WIN), 0) // CHUNK
        kk = lax.broadcasted_iota(jnp.int32, (tq, CA_WIN), 1)
        kc = kk // CHUNK
        band = (kc >= qc) & (kc <= qc + CA_LEFT_CHUNKS)
        for h in range(CA_HEADS):
            rows = jnp.broadcast_to(brow_ref[h:h + 1, :], (tq, CA_BIAS_LEN))
            rolled = pltpu.roll(rows, 0, 1, stride=1, stride_axis=0)
            base = jnp.where(band, rolled[:, :CA_WIN] * log2e, MASK_VALUE)
            for v in range(CA_SEQ_VARIANTS):
                first_valid = max(2 * tq - v * tq, 0)
                bias_ref[v, h] = jnp.where(kk >= first_valid, base, MASK_VALUE) if first_valid else base

    variant = jnp.minimum(t, CA_SEQ_VARIANTS - 1)
    c_exp = CA_DIM ** -0.5 * log2e
    ones = jnp.ones((CA_WIN, CA_DIM), BF16)
    outs = []
    for h in range(CA_HEADS):
        c = slice(h * CA_DIM, (h + 1) * CA_DIM)
        qh = (q_ref[:, c] * c_exp).astype(BF16)
        kh = jnp.concatenate([k0_ref[:, c], k1_ref[:, c], k2_ref[:, c]], axis=0).astype(BF16)
        vh = jnp.concatenate([v0_ref[:, c], v1_ref[:, c], v2_ref[:, c]], axis=0).astype(BF16)
        s = lax.dot_general(qh, kh, (((1,), (1,)), ((), ())), preferred_element_type=F32)
        s = s + bias_ref[variant, h]
        p = jnp.exp2(s - jnp.max(s, axis=-1, keepdims=True)).astype(BF16)
        ov = jnp.dot(p, jnp.concatenate([vh, ones], axis=1), preferred_element_type=F32)
        outs.append(ov[:, :CA_DIM] / ov[:, CA_DIM:CA_DIM + 1])
    o = jnp.concatenate(outs, axis=1)
    o_ref[...] = ((o * _rms_scale(o)) * gn_ref[...]).astype(o_ref.dtype)


def _chunk_attn(h, rel_bias, gn, batch, seq):
    tq = CA_TQ
    nq = seq // tq
    cq, ck, cv = COL_AQ // CA_WIDTH, COL_AK // CA_WIDTH, COL_AV // CA_WIDTH

    def kv_spec(col, back):
        return pl.BlockSpec((tq, CA_WIDTH), lambda b, t: (b * nq + jnp.maximum(t - back, 0), col))

    return pl.pallas_call(
        _ca_kernel,
        grid=(batch, nq),
        in_specs=[pl.BlockSpec((tq, CA_WIDTH), lambda b, t: (b * nq + t, cq)),
                  kv_spec(ck, 2), kv_spec(ck, 1), kv_spec(ck, 0),
                  kv_spec(cv, 2), kv_spec(cv, 1), kv_spec(cv, 0),
                  pl.BlockSpec((CA_HEADS, CA_BIAS_LEN), lambda b, t: (0, 0)),
                  pl.BlockSpec((1, CA_WIDTH), lambda b, t: (0, 0))],
        out_specs=pl.BlockSpec((tq, CA_WIDTH), lambda b, t: (b * nq + t, 0)),
        out_shape=jax.ShapeDtypeStruct((batch * seq, CA_WIDTH), BF16),
        scratch_shapes=[pltpu.VMEM((CA_SEQ_VARIANTS, CA_HEADS, tq, CA_WIN), F32)],
        compiler_params=_cparams(("arbitrary", "arbitrary")),
        name="chunk_attn",
    )(h, h, h, h, h, h, h, _ca_bias_rows(rel_bias), gn.reshape(1, -1))


X_SLICES = 8


def _residual_norm(o_ref, xs_ref, g_ref):
    rows, d = o_ref.shape
    w = d // X_SLICES

    step = NORM_ROWS * NORM_UNROLL

    def body(c, carry):
        base = pl.multiple_of(c * step, step)
        chunks = [pl.ds(base + u * NORM_ROWS, NORM_ROWS) for u in range(NORM_UNROLL)]
        scales = [_rms_scale(o_ref[r, :]) for r in chunks]
        for r, scale in zip(chunks, scales):
            for p in range(X_SLICES):
                cols = slice(p * w, (p + 1) * w)
                o_ref[r, cols] = xs_ref[p, r, :] + (o_ref[r, cols] * scale) * g_ref[:, cols]
        return carry

    lax.fori_loop(0, rows // step, body, 0)


def _x_slice_spec(tm, d):
    return pl.BlockSpec((tm, d // X_SLICES), lambda i, k: (i, jnp.minimum(k, X_SLICES - 1)))


def _accumulate_then_residual_norm(a_ref, w_ref, x_ref, g_ref, o_ref, xs_ref, ragged):
    k = pl.program_id(1)
    last = pl.num_programs(1) - 1

    @pl.when(k < X_SLICES)
    def _():
        xs_ref[k] = x_ref[...]

    @pl.when(k == 0)
    def _():
        o_ref[...] = jnp.dot(a_ref[...], w_ref[...], preferred_element_type=F32)

    if ragged is None:
        @pl.when(k > 0)
        def _():
            o_ref[...] += jnp.dot(a_ref[...], w_ref[...], preferred_element_type=F32)
    else:
        @pl.when((k > 0) & (k < last))
        def _():
            o_ref[...] += jnp.dot(a_ref[...], w_ref[...], preferred_element_type=F32)

        @pl.when(k == last)
        def _():
            a, w = a_ref[...], w_ref[...]
            a = jnp.where(lax.broadcasted_iota(jnp.int32, a.shape, 1) < ragged, a, jnp.zeros_like(a))
            w = jnp.where(lax.broadcasted_iota(jnp.int32, w.shape, 0) < ragged, w, jnp.zeros_like(w))
            o_ref[...] += jnp.dot(a, w, preferred_element_type=F32)

    @pl.when(k == last)
    def _():
        _residual_norm(o_ref, xs_ref, g_ref)


OUT_TK = 512


def _out_proj_kernel(mla_ref, hg_ref, ca_ref, w_ref, x_ref, gm_ref, gp_ref, o_ref, a_ref, xs_ref):
    @pl.when(pl.program_id(1) == 0)
    def _():
        rows = mla_ref.shape[0]
        n_mla, n_hg, n_ca = MLA_WIDTH // OUT_TK, HG_WIDTH // OUT_TK, CA_WIDTH // OUT_TK

        def body(c, carry):
            r = pl.ds(pl.multiple_of(c * NORM_ROWS, NORM_ROWS), NORM_ROWS)
            m = mla_ref[r, :]
            mn = ((m * _rms_scale(m)) * gm_ref[...]).astype(BF16)
            for p in range(n_mla):
                a_ref[p, r, :] = mn[:, p * OUT_TK:(p + 1) * OUT_TK]
            return carry

        lax.fori_loop(0, rows // NORM_ROWS, body, 0, unroll=NORM_UNROLL)
        for p in range(n_hg):
            a_ref[n_mla + p] = hg_ref[:, p * OUT_TK:(p + 1) * OUT_TK]
        for p in range(n_ca):
            a_ref[n_mla + n_hg + p] = ca_ref[:, p * OUT_TK:(p + 1) * OUT_TK]

    _accumulate_then_residual_norm(a_ref.at[pl.program_id(1)], w_ref, x_ref, gp_ref, o_ref, xs_ref, None)


def _out_proj(o_mla, o_hg, o_ca, x, g_mla, g_post, w, layer, tm):
    t, d = x.shape
    kdim = w.shape[1]
    nk = kdim // OUT_TK
    assert MLA_WIDTH % OUT_TK == 0 and HG_WIDTH % OUT_TK == 0 and CA_WIDTH % OUT_TK == 0
    assert kdim == MLA_WIDTH + HG_WIDTH + CA_WIDTH and nk >= X_SLICES
    return pl.pallas_call(
        _out_proj_kernel,
        grid=(t // tm, nk),
        in_specs=[pl.BlockSpec((tm, MLA_WIDTH), lambda i, k: (i, 0)),
                  pl.BlockSpec((tm, HG_WIDTH), lambda i, k: (i, 0)),
                  pl.BlockSpec((tm, CA_WIDTH), lambda i, k: (i, 0)),
                  pl.BlockSpec((None, OUT_TK, d), lambda i, k: (layer, k, 0)),
                  _x_slice_spec(tm, d),
                  pl.BlockSpec((1, MLA_WIDTH), lambda i, k: (0, 0)),
                  pl.BlockSpec((1, d), lambda i, k: (0, 0))],
        out_specs=pl.BlockSpec((tm, d), lambda i, k: (i, 0)),
        out_shape=jax.ShapeDtypeStruct((t, d), F32),
        scratch_shapes=[pltpu.VMEM((nk, tm, OUT_TK), BF16),
                        pltpu.VMEM((X_SLICES, tm, d // X_SLICES), F32)],
        compiler_params=_cparams(("parallel", "arbitrary")),
        name="out_proj",
    )(o_mla, o_hg, o_ca, w, x, g_mla.reshape(1, -1), g_post.reshape(1, -1))


def _down_proj_kernel(a_ref, w_ref, x_ref, g_ref, o_ref, xs_ref, *, ragged):
    _accumulate_then_residual_norm(a_ref, w_ref, x_ref, g_ref, o_ref, xs_ref, ragged)


def _down_proj(a, w, x, g, layer, tm, tk):
    t, d = x.shape
    kdim = a.shape[1]
    ragged = kdim % tk or None
    nk = pl.cdiv(kdim, tk)
    assert nk >= X_SLICES
    return pl.pallas_call(
        functools.partial(_down_proj_kernel, ragged=ragged),
        grid=(t // tm, nk),
        in_specs=[pl.BlockSpec((tm, tk), lambda i, k: (i, k)),
                  pl.BlockSpec((None, tk, d), lambda i, k: (layer, k, 0)),
                  _x_slice_spec(tm, d),
                  pl.BlockSpec((1, d), lambda i, k: (0, 0))],
        out_specs=pl.BlockSpec((tm, d), lambda i, k: (i, 0)),
        out_shape=jax.ShapeDtypeStruct((t, d), F32),
        scratch_shapes=[pltpu.VMEM((X_SLICES, tm, d // X_SLICES), F32)],
        compiler_params=_cparams(("parallel", "arbitrary")),
        name="down_proj",
    )(a, w, x, g.reshape(1, -1))


PREP_ROWS = 256


def _prep_w_in(w):
    depth, d, _ = w.shape
    o_ckv = MLA_Q_RANK
    o_kr = o_ckv + MLA_KV_RANK
    o_hg = o_kr + MLA_ROPE
    o_ca = o_hg + 4 * HG_WIDTH
    assert (COL_AQ, COL_AK, COL_AV) == (0, CA_WIDTH, 2 * CA_WIDTH) and COL_HQ == 3 * CA_WIDTH
    assert (COL_HF, COL_HI, COL_HG) == (COL_HQ + HG_WIDTH, COL_HQ + 2 * HG_WIDTH, COL_HQ + 3 * HG_WIDTH)
    assert COL_CKV == COL_HG + HG_WIDTH and COL_CQ == COL_CKV + MLA_KV_RANK and COL_KR == COL_CQ + MLA_Q_RANK
    n_in = w.shape[2]

    def regroup_kernel(w_ref, o_ref):
        x = w_ref[...]
        z = jnp.zeros((x.shape[0], D_IN_PAD - COL_KR - MLA_ROPE), x.dtype)
        o_ref[...] = jnp.concatenate([x[:, o_ca:], x[:, o_hg:o_ca], x[:, o_ckv:o_kr], x[:, :o_ckv],
                                      x[:, o_kr:o_hg], z], axis=1).astype(BF16)

    rows = PREP_ROWS
    return pl.pallas_call(
        regroup_kernel,
        grid=(depth, d // rows),
        in_specs=[pl.BlockSpec((None, rows, n_in), lambda l, i: (l, i, 0))],
        out_specs=pl.BlockSpec((None, rows, D_IN_PAD), lambda l, i: (l, i, 0)),
        out_shape=jax.ShapeDtypeStruct((depth, d, D_IN_PAD), BF16),
        compiler_params=_cparams(("parallel", "parallel")),
        name="w_in_regroup",
    )(w)


def _prep_w_uq(w):
    depth, r, _ = w.shape
    w4 = w.astype(BF16).reshape(depth, r, MLA_HEADS, MLA_NOPE + MLA_ROPE)
    pad = jnp.zeros((depth, r, MLA_HEADS, MLA_QK_PAD - MLA_NOPE - MLA_ROPE), BF16)
    wq = jnp.concatenate([w4, pad], axis=3).reshape(depth, r, MLA_HEADS * MLA_QK_PAD)
    return jnp.swapaxes(wq, 1, 2)


def _prep_w_ukv(w):
    depth, r, _ = w.shape
    w4 = w.astype(BF16).reshape(depth, r, MLA_HEADS, MLA_NOPE + MLA_V)
    wk = w4[:, :, :, :MLA_NOPE].reshape(depth, r, MLA_HEADS * MLA_NOPE)
    wv = w4[:, :, :, MLA_NOPE:].reshape(depth, r, MLA_HEADS * MLA_V)
    return wk, jnp.swapaxes(wv, 1, 2)


def kernel(x, positions, attn_pre_norm, attn_post_norm, w_in, mla_q_norm, mla_kv_norm, w_uq, w_ukv,
           mla_out_norm, hg_lower_bounds, hg_out_norm, ca_rel_bias, ca_out_norm, w_out, ffn_pre_norm,
           ffn_post_norm, w_gate, w_up, w_down):
    batch, seq, d = x.shape
    t = batch * seq
    depth = w_in.shape[0]
    xf = x.reshape(t, d)
    pos = positions.reshape(t, 1)
    w_in_b = _prep_w_in(w_in)
    w_uq_b = _prep_w_uq(w_uq)
    w_uk_b, w_uvt_b = _prep_w_ukv(w_ukv)
    w_out_b, w_gate_b, w_up_b, w_down_b = (w.astype(BF16) for w in (w_out, w_gate, w_up, w_down))
    for l in range(depth):
        h = _norm_matmul(xf, attn_pre_norm[l], [w_in_b], l, F32, tm=NORM_MATMUL_TM, tn=NORM_MATMUL_TN)
        q, k, vt = _mla_proj(h, pos, mla_q_norm[l], mla_kv_norm[l], w_uq_b, w_uk_b, w_uvt_b, l, tm=MLA_TK)
        o_mla = _mla_attn(q, k, vt, batch, seq)
        o_hg = _hgrn(h, hg_lower_bounds, hg_out_norm[l], l, batch, seq)
        o_ca = _chunk_attn(h, ca_rel_bias[l], ca_out_norm[l], batch, seq)
        xf = _out_proj(o_mla, o_hg, o_ca, xf, mla_out_norm[l], attn_post_norm[l], w_out_b, l, tm=RESIDUAL_TM)
        hid = _norm_matmul(xf, ffn_pre_norm[l], [w_gate_b, w_up_b], l, BF16,
                           tm=NORM_MATMUL_TM, tn=NORM_MATMUL_TN)
        xf = _down_proj(hid, w_down_b, xf, ffn_post_norm[l], l, tm=RESIDUAL_TM, tk=DOWN_TK)
    return xf.reshape(batch, seq, d)
```
